```python
import math
import jax, jax.numpy as jnp
from jax import lax
import numpy as np

D_MODEL = 1024
BATCH = 8
SEQ = 16384
DEPTH = 2

N_MEM = 256
DN_HEADS = 4
DN_DK = 128
DN_DV = 128
DN_CONV = 4
DN_CHUNK = 64
SWA_HEADS = 8
SWA_KV_HEADS = 2
SWA_DH = 64
WINDOW = 128
XA_HEADS = 4
XA_DH = 128
D_FF = 2816
N_BRANCH = 3
BRANCH_W = 512
DEEPNORM_ALPHA = (2 * DEPTH) ** 0.25
DEEPNORM_BETA = (8 * DEPTH) ** -0.25
LN_EPS = 1e-5
RMS_EPS = 1e-6
NEG_INF = -1e30
IN_SPLITS = (DN_HEADS * DN_DK, DN_HEADS * DN_DK, DN_HEADS * DN_DV, DN_HEADS, DN_HEADS,
             DN_HEADS * DN_DV, SWA_HEADS * SWA_DH, SWA_KV_HEADS * SWA_DH, SWA_KV_HEADS * SWA_DH,
             XA_HEADS * XA_DH, N_BRANCH * D_MODEL)
D_IN = sum(IN_SPLITS)
VALUE_SEGMENTS = (2, 8)

kernel_name = "hybrid_deltanet_swa_sink_memxattn_macaron_deepnorm"


def layer_norm(x, g, b):
    xf = x.astype(jnp.float32)
    mu = xf.mean(-1, keepdims=True)
    var = jnp.square(xf - mu).mean(-1, keepdims=True)
    return ((xf - mu) * lax.rsqrt(var + LN_EPS) * g.astype(jnp.float32) + b.astype(jnp.float32)).astype(x.dtype)


def swiglu(x, w_gu, w_down):
    gate, up = jnp.split(x @ w_gu, 2, axis=-1)
    return (jax.nn.silu(gate) * up) @ w_down


def causal_depthwise_conv(x, w):
    K, C = w.shape
    return lax.conv_general_dilated(x, w[:, None, :].astype(x.dtype), window_strides=(1,),
                                    padding=[(K - 1, 0)], dimension_numbers=('NWC', 'WIO', 'NWC'),
                                    feature_group_count=C)


def gated_delta_rule(q, k, v, g, beta):
    f32 = jnp.float32
    B_, S_, H, dk = q.shape
    dv = v.shape[-1]
    C = DN_CHUNK
    N = S_ // C

    def chunks(t):
        t = t.astype(f32).reshape((B_, N, C, H) + t.shape[3:])
        return jnp.moveaxis(t, 3, 1)

    q = chunks(q) * (dk ** -0.5)
    k, v, beta, g = chunks(k), chunks(v), chunks(beta), chunks(g)
    g = jnp.cumsum(g, axis=-1)
    tril = jnp.tril(jnp.ones((C, C), bool))
    strict = jnp.tril(jnp.ones((C, C), bool), -1)
    decay = jnp.exp(jnp.where(tril, g[..., :, None] - g[..., None, :], NEG_INF))

    k_beta = k * beta[..., None]
    a = jnp.where(strict, jnp.einsum('bhncd,bhnsd->bhncs', k_beta, k) * decay, 0.0)
    t_mat = a + jnp.eye(C, dtype=f32)
    rhs = jnp.concatenate([v * beta[..., None], k_beta * jnp.exp(g)[..., None]], axis=-1)
    sol = lax.linalg.triangular_solve(t_mat, rhs, left_side=True, lower=True, unit_diagonal=True)
    u, w = sol[..., :dv], sol[..., dv:]

    qk = jnp.where(tril, jnp.einsum('bhncd,bhnsd->bhncs', q, k) * decay, 0.0)
    g_last = g[..., -1]
    k_tail = k * jnp.exp(g_last[..., None] - g)[..., None]
    q_dec = q * jnp.exp(g)[..., None]

    def step(S, xs):
        q_i, qk_i, u_i, w_i, kt_i, gl_i = xs
        v_new = u_i - jnp.einsum('bhcd,bhde->bhce', w_i, S)
        o = jnp.einsum('bhcd,bhde->bhce', q_i, S) + jnp.einsum('bhcs,bhse->bhce', qk_i, v_new)
        S = S * jnp.exp(gl_i)[..., None, None] + jnp.einsum('bhcd,bhce->bhde', kt_i, v_new)
        return S, o

    xs = tuple(jnp.moveaxis(t, 2, 0) for t in (q_dec, qk, u, w, k_tail, g_last))
    S0 = jnp.zeros((B_, H, dk, dv), f32)
    _, o = lax.scan(step, S0, xs)
    o = jnp.moveaxis(o, 0, 2)
    return jnp.moveaxis(o, 1, 3).reshape(B_, S_, H, dv)


def deltanet_branch(q, k, v, b, a, z, conv_w, a_log, dt_bias, norm_w):
    f32 = jnp.float32
    B_, S_, _ = q.shape
    qkv = jax.nn.silu(causal_depthwise_conv(jnp.concatenate([q, k, v], axis=-1), conv_w))
    q, k, v = jnp.split(qkv.astype(f32), 3, axis=-1)
    q = q.reshape(B_, S_, DN_HEADS, DN_DK)
    k = k.reshape(B_, S_, DN_HEADS, DN_DK)
    v = v.reshape(B_, S_, DN_HEADS, DN_DV)
    q = q * lax.rsqrt(jnp.sum(q * q, -1, keepdims=True) + RMS_EPS)
    k = k * lax.rsqrt(jnp.sum(k * k, -1, keepdims=True) + RMS_EPS)
    beta = jax.nn.sigmoid(b.astype(f32))
    g = -jnp.exp(a_log.astype(f32)) * jax.nn.softplus(a.astype(f32) + dt_bias.astype(f32))
    o = gated_delta_rule(q, k, v, g, beta)
    o = o * lax.rsqrt(jnp.mean(o * o, -1, keepdims=True) + RMS_EPS) * norm_w.astype(f32)
    o = o * jax.nn.silu(z.astype(f32).reshape(B_, S_, DN_HEADS, DN_DV))
    return o.reshape(B_, S_, DN_HEADS * DN_DV).astype(z.dtype)


def sliding_window_attention(q, k, v, sinks):
    f32 = jnp.float32
    B_, S_, _ = q.shape
    Hkv, G, dh, W = SWA_KV_HEADS, SWA_HEADS // SWA_KV_HEADS, SWA_DH, WINDOW
    nb = S_ // W
    qb = q.reshape(B_, nb, W, Hkv, G, dh)

    def band(t):
        tb = t.reshape(B_, nb, W, Hkv, dh)
        prev = jnp.pad(tb[:, :-1], ((0, 0), (1, 0), (0, 0), (0, 0), (0, 0)))
        return jnp.concatenate([prev, tb], axis=2)

    kb, vb = band(k), band(v)
    s = jnp.einsum('bnqhgd,bnkhd->bnhgqk', qb, kb).astype(f32) * (dh ** -0.5)
    q_pos = jnp.arange(nb)[:, None] * W + jnp.arange(W)[None, :]
    k_pos = jnp.arange(nb)[:, None] * W - W + jnp.arange(2 * W)[None, :]
    diff = q_pos[:, :, None] - k_pos[:, None, :]
    mask = (diff >= 0) & (diff < W) & (k_pos[:, None, :] >= 0)
    s = jnp.where(mask[None, :, None, None], s, NEG_INF)
    sink = sinks.astype(f32).reshape(Hkv, G)[None, None, :, :, None, None]
    m = jnp.maximum(s.max(-1, keepdims=True), sink)
    p = jnp.exp(s - m)
    p = p / (p.sum(-1, keepdims=True) + jnp.exp(sink - m))
    o = jnp.einsum('bnhgqk,bnkhd->bnqhgd', p.astype(v.dtype), vb)
    return o.reshape(B_, S_, SWA_HEADS * dh)


def memory_cross_attention(q, mem_n, w_mem_kv):
    B_, S_, _ = q.shape
    M = mem_n.shape[1]
    k, v = jnp.split(mem_n @ w_mem_kv, 2, axis=-1)
    k = k.reshape(B_, M, XA_HEADS, XA_DH)
    v = v.reshape(B_, M, XA_HEADS, XA_DH)
    q = q.reshape(B_, S_, XA_HEADS, XA_DH)
    s = jnp.einsum('bshd,bmhd->bhsm', q, k).astype(jnp.float32) * (XA_DH ** -0.5)
    p = jax.nn.softmax(s, axis=-1)
    o = jnp.einsum('bhsm,bmhd->bshd', p.astype(v.dtype), v)
    return o.reshape(B_, S_, XA_HEADS * XA_DH)


def hybrid_layer(x, mem_n, ln_g, ln_b, ffn1_w_gu, ffn1_w_down, w_in, dn_conv_w, dn_a_log,
                 dn_dt_bias, dn_norm_w, swa_sinks, w_mem_kv, w_branch, w_out, ffn2_w_gu, ffn2_w_down):
    B_, S_, D = x.shape
    h = layer_norm(DEEPNORM_ALPHA * x + 0.5 * swiglu(x, ffn1_w_gu, ffn1_w_down), ln_g[0], ln_b[0])
    split_idx = [int(i) for i in np.cumsum(IN_SPLITS)[:-1]]
    (dn_q, dn_k, dn_v, dn_b, dn_a, dn_z, sw_q, sw_k, sw_v, xa_q, gates) = jnp.split(h @ w_in, split_idx, axis=-1)
    o_dn = deltanet_branch(dn_q, dn_k, dn_v, dn_b, dn_a, dn_z, dn_conv_w, dn_a_log, dn_dt_bias, dn_norm_w)
    o_sw = sliding_window_attention(sw_q, sw_k, sw_v, swa_sinks)
    o_xa = memory_cross_attention(xa_q, mem_n, w_mem_kv)
    branches = jnp.stack([o_dn, o_sw, o_xa], axis=2)
    gates = jax.nn.sigmoid(gates.reshape(B_, S_, N_BRANCH, D))
    merged = jnp.sum(gates * jnp.einsum('bsnc,ncd->bsnd', branches, w_branch), axis=2)
    h = layer_norm(DEEPNORM_ALPHA * h + merged @ w_out, ln_g[1], ln_b[1])
    return layer_norm(DEEPNORM_ALPHA * h + 0.5 * swiglu(h, ffn2_w_gu, ffn2_w_down), ln_g[2], ln_b[2])


def _fwd_setup_inputs(seed: int = 0) -> dict:
    key = jax.random.key(seed)
    ks = jax.random.split(key, 20)
    f32 = jnp.float32
    nrm = lambda k, shape, s: jax.random.normal(k, shape, f32) * s
    col_scale = np.concatenate([np.full(n, DEEPNORM_BETA if i in VALUE_SEGMENTS else 1.0, np.float32)
                                for i, n in enumerate(IN_SPLITS)])
    kv_scale = np.concatenate([np.ones(XA_HEADS * XA_DH, np.float32),
                               np.full(XA_HEADS * XA_DH, DEEPNORM_BETA, np.float32)])
    dt = jnp.exp(jax.random.uniform(ks[9], (DEPTH, DN_HEADS), f32, math.log(1e-3), math.log(1e-1)))
    return {
        "x": nrm(ks[0], (BATCH, SEQ, D_MODEL), 1.0),
        "mem": nrm(ks[1], (BATCH, N_MEM, D_MODEL), 1.0),
        "mem_ln_g": 1.0 + nrm(ks[2], (D_MODEL,), 0.02),
        "mem_ln_b": nrm(ks[3], (D_MODEL,), 0.02),
        "ln_g": 1.0 + nrm(ks[4], (DEPTH, 3, D_MODEL), 0.02),
        "ln_b": nrm(ks[5], (DEPTH, 3, D_MODEL), 0.02),
        "ffn1_w_gu": nrm(ks[6], (DEPTH, D_MODEL, 2 * D_FF), DEEPNORM_BETA * D_MODEL ** -0.5),
        "ffn1_w_down": nrm(ks[7], (DEPTH, D_FF, D_MODEL), DEEPNORM_BETA * D_FF ** -0.5),
        "w_in": nrm(ks[8], (DEPTH, D_MODEL, D_IN), D_MODEL ** -0.5) * jnp.asarray(col_scale),
        "dn_conv_w": nrm(ks[10], (DEPTH, DN_CONV, 3 * DN_HEADS * DN_DK), DN_CONV ** -0.5),
        "dn_a_log": jnp.log(jax.random.uniform(ks[11], (DEPTH, DN_HEADS), f32, 1.0, 16.0)),
        "dn_dt_bias": dt + jnp.log(-jnp.expm1(-dt)),
        "dn_norm_w": 1.0 + nrm(ks[12], (DEPTH, DN_DV), 0.02),
        "swa_sinks": nrm(ks[13], (DEPTH, SWA_HEADS), 0.5),
        "w_mem_kv": nrm(ks[14], (DEPTH, D_MODEL, 2 * XA_HEADS * XA_DH), D_MODEL ** -0.5) * jnp.asarray(kv_scale),
        "w_branch": nrm(ks[15], (DEPTH, N_BRANCH, BRANCH_W, D_MODEL), DEEPNORM_BETA * BRANCH_W ** -0.5),
        "w_out": nrm(ks[16], (DEPTH, D_MODEL, D_MODEL), DEEPNORM_BETA * D_MODEL ** -0.5),
        "ffn2_w_gu": nrm(ks[17], (DEPTH, D_MODEL, 2 * D_FF), DEEPNORM_BETA * D_MODEL ** -0.5),
        "ffn2_w_down": nrm(ks[18], (DEPTH, D_FF, D_MODEL), DEEPNORM_BETA * D_FF ** -0.5),
    }


def _fwd_reference(x, mem, mem_ln_g, mem_ln_b, ln_g, ln_b, ffn1_w_gu, ffn1_w_down, w_in, dn_conv_w,
              dn_a_log, dn_dt_bias, dn_norm_w, swa_sinks, w_mem_kv, w_branch, w_out, ffn2_w_gu, ffn2_w_down):
    mem_n = layer_norm(mem, mem_ln_g, mem_ln_b)
    for l in range(DEPTH):
        x = hybrid_layer(x, mem_n, ln_g[l], ln_b[l], ffn1_w_gu[l], ffn1_w_down[l], w_in[l], dn_conv_w[l],
                         dn_a_log[l], dn_dt_bias[l], dn_norm_w[l], swa_sinks[l], w_mem_kv[l], w_branch[l],
                         w_out[l], ffn2_w_gu[l], ffn2_w_down[l])
    return x


import jax as _jax
import jax.numpy as _jnp

TWIN_FORMAT = 'train_step'
FWD_PARAMS = ['x', 'mem', 'mem_ln_g', 'mem_ln_b', 'ln_g', 'ln_b', 'ffn1_w_gu', 'ffn1_w_down', 'w_in', 'dn_conv_w', 'dn_a_log', 'dn_dt_bias', 'dn_norm_w', 'swa_sinks', 'w_mem_kv', 'w_branch', 'w_out', 'ffn2_w_gu', 'ffn2_w_down']
TWIN_WEIGHTS = ['mem_ln_g', 'mem_ln_b', 'ln_g', 'ln_b', 'ffn1_w_gu', 'ffn1_w_down', 'w_in', 'dn_conv_w', 'dn_a_log', 'dn_dt_bias', 'dn_norm_w', 'swa_sinks', 'w_mem_kv', 'w_branch', 'w_out', 'ffn2_w_gu', 'ffn2_w_down']
TWIN_DIFF_INPUT = 'x'
TWIN_INPUTS = ['x', 'mem', 'mem_ln_g', 'mem_ln_b', 'ln_g', 'ln_b', 'ffn1_w_gu', 'ffn1_w_down', 'w_in', 'dn_conv_w', 'dn_a_log', 'dn_dt_bias', 'dn_norm_w', 'swa_sinks', 'w_mem_kv', 'w_branch', 'w_out', 'ffn2_w_gu', 'ffn2_w_down', 'loss_target', 'm_mem_ln_g', 'm_mem_ln_b', 'm_ln_g', 'm_ln_b', 'm_ffn1_w_gu', 'm_ffn1_w_down', 'm_w_in', 'm_dn_conv_w', 'm_dn_a_log', 'm_dn_dt_bias', 'm_dn_norm_w', 'm_swa_sinks', 'm_w_mem_kv', 'm_w_branch', 'm_w_out', 'm_ffn2_w_gu', 'm_ffn2_w_down', 'v_mem_ln_g', 'v_mem_ln_b', 'v_ln_g', 'v_ln_b', 'v_ffn1_w_gu', 'v_ffn1_w_down', 'v_w_in', 'v_dn_conv_w', 'v_dn_a_log', 'v_dn_dt_bias', 'v_dn_norm_w', 'v_swa_sinks', 'v_w_mem_kv', 'v_w_branch', 'v_w_out', 'v_ffn2_w_gu', 'v_ffn2_w_down']
TWIN_OUTPUTS = ['loss', 'grad_x', 'grad_mem_ln_g', 'grad_mem_ln_b', 'grad_ln_g', 'grad_ln_b', 'grad_ffn1_w_gu', 'grad_ffn1_w_down', 'grad_w_in', 'grad_dn_conv_w', 'grad_dn_a_log', 'grad_dn_dt_bias', 'grad_dn_norm_w', 'grad_swa_sinks', 'grad_w_mem_kv', 'grad_w_branch', 'grad_w_out', 'grad_ffn2_w_gu', 'grad_ffn2_w_down', 'delta_mem_ln_g', 'delta_mem_ln_b', 'delta_ln_g', 'delta_ln_b', 'delta_ffn1_w_gu', 'delta_ffn1_w_down', 'delta_w_in', 'delta_dn_conv_w', 'delta_dn_a_log', 'delta_dn_dt_bias', 'delta_dn_norm_w', 'delta_swa_sinks', 'delta_w_mem_kv', 'delta_w_branch', 'delta_w_out', 'delta_ffn2_w_gu', 'delta_ffn2_w_down', 'new_m_mem_ln_g', 'new_m_mem_ln_b', 'new_m_ln_g', 'new_m_ln_b', 'new_m_ffn1_w_gu', 'new_m_ffn1_w_down', 'new_m_w_in', 'new_m_dn_conv_w', 'new_m_dn_a_log', 'new_m_dn_dt_bias', 'new_m_dn_norm_w', 'new_m_swa_sinks', 'new_m_w_mem_kv', 'new_m_w_branch', 'new_m_w_out', 'new_m_ffn2_w_gu', 'new_m_ffn2_w_down', 'new_v_mem_ln_g', 'new_v_mem_ln_b', 'new_v_ln_g', 'new_v_ln_b', 'new_v_ffn1_w_gu', 'new_v_ffn1_w_down', 'new_v_w_in', 'new_v_dn_conv_w', 'new_v_dn_a_log', 'new_v_dn_dt_bias', 'new_v_dn_norm_w', 'new_v_swa_sinks', 'new_v_w_mem_kv', 'new_v_w_branch', 'new_v_w_out', 'new_v_ffn2_w_gu', 'new_v_ffn2_w_down']
TWIN_LEAF_KINDS = {'loss': 'loss', 'grad_x': 'grad_x', 'grad_mem_ln_g': 'grad_w', 'grad_mem_ln_b': 'grad_w', 'grad_ln_g': 'grad_w', 'grad_ln_b': 'grad_w', 'grad_ffn1_w_gu': 'grad_w', 'grad_ffn1_w_down': 'grad_w', 'grad_w_in': 'grad_w', 'grad_dn_conv_w': 'grad_w', 'grad_dn_a_log': 'grad_w', 'grad_dn_dt_bias': 'grad_w', 'grad_dn_norm_w': 'grad_w', 'grad_swa_sinks': 'grad_w', 'grad_w_mem_kv': 'grad_w', 'grad_w_branch': 'grad_w', 'grad_w_out': 'grad_w', 'grad_ffn2_w_gu': 'grad_w', 'grad_ffn2_w_down': 'grad_w', 'delta_mem_ln_g': 'delta_w', 'delta_mem_ln_b': 'delta_w', 'delta_ln_g': 'delta_w', 'delta_ln_b': 'delta_w', 'delta_ffn1_w_gu': 'delta_w', 'delta_ffn1_w_down': 'delta_w', 'delta_w_in': 'delta_w', 'delta_dn_conv_w': 'delta_w', 'delta_dn_a_log': 'delta_w', 'delta_dn_dt_bias': 'delta_w', 'delta_dn_norm_w': 'delta_w', 'delta_swa_sinks': 'delta_w', 'delta_w_mem_kv': 'delta_w', 'delta_w_branch': 'delta_w', 'delta_w_out': 'delta_w', 'delta_ffn2_w_gu': 'delta_w', 'delta_ffn2_w_down': 'delta_w', 'new_m_mem_ln_g': 'new_m', 'new_m_mem_ln_b': 'new_m', 'new_m_ln_g': 'new_m', 'new_m_ln_b': 'new_m', 'new_m_ffn1_w_gu': 'new_m', 'new_m_ffn1_w_down': 'new_m', 'new_m_w_in': 'new_m', 'new_m_dn_conv_w': 'new_m', 'new_m_dn_a_log': 'new_m', 'new_m_dn_dt_bias': 'new_m', 'new_m_dn_norm_w': 'new_m', 'new_m_swa_sinks': 'new_m', 'new_m_w_mem_kv': 'new_m', 'new_m_w_branch': 'new_m', 'new_m_w_out': 'new_m', 'new_m_ffn2_w_gu': 'new_m', 'new_m_ffn2_w_down': 'new_m', 'new_v_mem_ln_g': 'new_v', 'new_v_mem_ln_b': 'new_v', 'new_v_ln_g': 'new_v', 'new_v_ln_b': 'new_v', 'new_v_ffn1_w_gu': 'new_v', 'new_v_ffn1_w_down': 'new_v', 'new_v_w_in': 'new_v', 'new_v_dn_conv_w': 'new_v', 'new_v_dn_a_log': 'new_v', 'new_v_dn_dt_bias': 'new_v', 'new_v_dn_norm_w': 'new_v', 'new_v_swa_sinks': 'new_v', 'new_v_w_mem_kv': 'new_v', 'new_v_w_branch': 'new_v', 'new_v_w_out': 'new_v', 'new_v_ffn2_w_gu': 'new_v', 'new_v_ffn2_w_down': 'new_v'}


def _forward(args):
    return _fwd_reference(*[args[k] for k in FWD_PARAMS])


def _output_shape():
    def fwd():
        inp = _fwd_setup_inputs(0)
        return _fwd_reference(*[inp[k] for k in FWD_PARAMS])
    out = _jax.eval_shape(fwd)
    return out.shape, out.dtype

N_MICROBATCH = 1
ADAM_LR = 0.001
ADAM_B1 = 0.9
ADAM_B2 = 0.999
ADAM_EPS = 1e-08
ADAM_WD = 0.01
ADAM_STEP = 10
PER_EXAMPLE_BATCH_AXIS = {'x': 0, 'mem': 0, 'loss_target': 0}
SHARED_INPUTS = []
_WEIGHT_DTYPES = {'mem_ln_g': _jnp.float32, 'mem_ln_b': _jnp.float32, 'ln_g': _jnp.float32, 'ln_b': _jnp.float32, 'ffn1_w_gu': _jnp.float32, 'ffn1_w_down': _jnp.float32, 'w_in': _jnp.float32, 'dn_conv_w': _jnp.float32, 'dn_a_log': _jnp.float32, 'dn_dt_bias': _jnp.float32, 'dn_norm_w': _jnp.float32, 'swa_sinks': _jnp.float32, 'w_mem_kv': _jnp.float32, 'w_branch': _jnp.float32, 'w_out': _jnp.float32, 'ffn2_w_gu': _jnp.float32, 'ffn2_w_down': _jnp.float32}
MOMENT_SCALE = {'mem_ln_g': 3.947044e-03, 'mem_ln_b': 1.040302e-01, 'ln_g': 5.259648e+01, 'ln_b': 2.569079e+00, 'ffn1_w_gu': 1.030235e-02, 'ffn1_w_down': 1.689254e-02, 'w_in': 1.821665e-02, 'dn_conv_w': 2.167136e-02, 'dn_a_log': 2.208761e-01, 'dn_dt_bias': 2.107189e-01, 'dn_norm_w': 5.730704e-02, 'swa_sinks': 5.546419e-03, 'w_mem_kv': 4.732424e-03, 'w_branch': 2.378748e-02, 'w_out': 4.127783e-02, 'ffn2_w_gu': 1.027812e-02, 'ffn2_w_down': 1.691933e-02}


def _to_microbatches(a, axis):
    t = _jnp.moveaxis(a, axis, 0)
    t = t.reshape((N_MICROBATCH, t.shape[0] // N_MICROBATCH) + t.shape[1:])
    return _jnp.moveaxis(t, 1, axis + 1)


def setup_inputs(seed: int = 0) -> dict:
    inp = _fwd_setup_inputs(seed)
    key = _jax.random.fold_in(_jax.random.key(seed), 7919)
    shape, _ = _output_shape()
    out = dict(inp)
    out["loss_target"] = _jax.random.normal(_jax.random.fold_in(key, 0), shape, _jnp.float32)
    for i, name in enumerate(TWIN_WEIGHTS):
        w = inp[name].astype(_jnp.float32)
        if MOMENT_SCALE is None:
            s = _jnp.sqrt(_jnp.mean(_jnp.square(w)) + 1e-30)
        else:
            s = MOMENT_SCALE[name]
        km, kv = _jax.random.split(_jax.random.fold_in(key, i + 1))
        out[name] = w
        out["m_" + name] = s * _jax.random.normal(km, w.shape, _jnp.float32)
        out["v_" + name] = (s * s) * _jax.random.uniform(kv, w.shape, _jnp.float32, 0.5, 1.5)
    if N_MICROBATCH > 1:
        for name, axis in PER_EXAMPLE_BATCH_AXIS.items():
            out[name] = _to_microbatches(out[name], axis)
    return {'x': out['x'], 'mem': out['mem'], 'mem_ln_g': out['mem_ln_g'], 'mem_ln_b': out['mem_ln_b'], 'ln_g': out['ln_g'], 'ln_b': out['ln_b'], 'ffn1_w_gu': out['ffn1_w_gu'], 'ffn1_w_down': out['ffn1_w_down'], 'w_in': out['w_in'], 'dn_conv_w': out['dn_conv_w'], 'dn_a_log': out['dn_a_log'], 'dn_dt_bias': out['dn_dt_bias'], 'dn_norm_w': out['dn_norm_w'], 'swa_sinks': out['swa_sinks'], 'w_mem_kv': out['w_mem_kv'], 'w_branch': out['w_branch'], 'w_out': out['w_out'], 'ffn2_w_gu': out['ffn2_w_gu'], 'ffn2_w_down': out['ffn2_w_down'], 'loss_target': out['loss_target'], 'm_mem_ln_g': out['m_mem_ln_g'], 'm_mem_ln_b': out['m_mem_ln_b'], 'm_ln_g': out['m_ln_g'], 'm_ln_b': out['m_ln_b'], 'm_ffn1_w_gu': out['m_ffn1_w_gu'], 'm_ffn1_w_down': out['m_ffn1_w_down'], 'm_w_in': out['m_w_in'], 'm_dn_conv_w': out['m_dn_conv_w'], 'm_dn_a_log': out['m_dn_a_log'], 'm_dn_dt_bias': out['m_dn_dt_bias'], 'm_dn_norm_w': out['m_dn_norm_w'], 'm_swa_sinks': out['m_swa_sinks'], 'm_w_mem_kv': out['m_w_mem_kv'], 'm_w_branch': out['m_w_branch'], 'm_w_out': out['m_w_out'], 'm_ffn2_w_gu': out['m_ffn2_w_gu'], 'm_ffn2_w_down': out['m_ffn2_w_down'], 'v_mem_ln_g': out['v_mem_ln_g'], 'v_mem_ln_b': out['v_mem_ln_b'], 'v_ln_g': out['v_ln_g'], 'v_ln_b': out['v_ln_b'], 'v_ffn1_w_gu': out['v_ffn1_w_gu'], 'v_ffn1_w_down': out['v_ffn1_w_down'], 'v_w_in': out['v_w_in'], 'v_dn_conv_w': out['v_dn_conv_w'], 'v_dn_a_log': out['v_dn_a_log'], 'v_dn_dt_bias': out['v_dn_dt_bias'], 'v_dn_norm_w': out['v_dn_norm_w'], 'v_swa_sinks': out['v_swa_sinks'], 'v_w_mem_kv': out['v_w_mem_kv'], 'v_w_branch': out['v_w_branch'], 'v_w_out': out['v_w_out'], 'v_ffn2_w_gu': out['v_ffn2_w_gu'], 'v_ffn2_w_down': out['v_ffn2_w_down']}


def _loss(weights, diff, rest, loss_target):
    with _jax.named_scope("forward"):
        args = {**rest, TWIN_DIFF_INPUT: diff, **{k: w.astype(_WEIGHT_DTYPES[k]) for k, w in weights.items()}}
        y = _forward(args)
    with _jax.named_scope("loss_head"):
        err = _jnp.square(y.astype(_jnp.float32) - loss_target)
        return 0.5 * _jnp.sum(_jnp.mean(err, axis=-1)) if err.ndim else 0.5 * err


def _adamw(w, g, m, v):
    m = ADAM_B1 * m + (1.0 - ADAM_B1) * g
    v = ADAM_B2 * v + (1.0 - ADAM_B2) * _jnp.square(g)
    m_hat = m / (1.0 - ADAM_B1 ** ADAM_STEP)
    v_hat = v / (1.0 - ADAM_B2 ** ADAM_STEP)
    delta = -ADAM_LR * (m_hat / (_jnp.sqrt(v_hat) + ADAM_EPS) + ADAM_WD * w)
    return delta, m, v


def reference(x, mem, mem_ln_g, mem_ln_b, ln_g, ln_b, ffn1_w_gu, ffn1_w_down, w_in, dn_conv_w, dn_a_log, dn_dt_bias, dn_norm_w, swa_sinks, w_mem_kv, w_branch, w_out, ffn2_w_gu, ffn2_w_down, loss_target, m_mem_ln_g, m_mem_ln_b, m_ln_g, m_ln_b, m_ffn1_w_gu, m_ffn1_w_down, m_w_in, m_dn_conv_w, m_dn_a_log, m_dn_dt_bias, m_dn_norm_w, m_swa_sinks, m_w_mem_kv, m_w_branch, m_w_out, m_ffn2_w_gu, m_ffn2_w_down, v_mem_ln_g, v_mem_ln_b, v_ln_g, v_ln_b, v_ffn1_w_gu, v_ffn1_w_down, v_w_in, v_dn_conv_w, v_dn_a_log, v_dn_dt_bias, v_dn_norm_w, v_swa_sinks, v_w_mem_kv, v_w_branch, v_w_out, v_ffn2_w_gu, v_ffn2_w_down):
    given = dict(x=x, mem=mem, mem_ln_g=mem_ln_g, mem_ln_b=mem_ln_b, ln_g=ln_g, ln_b=ln_b, ffn1_w_gu=ffn1_w_gu, ffn1_w_down=ffn1_w_down, w_in=w_in, dn_conv_w=dn_conv_w, dn_a_log=dn_a_log, dn_dt_bias=dn_dt_bias, dn_norm_w=dn_norm_w, swa_sinks=swa_sinks, w_mem_kv=w_mem_kv, w_branch=w_branch, w_out=w_out, ffn2_w_gu=ffn2_w_gu, ffn2_w_down=ffn2_w_down, loss_target=loss_target, m_mem_ln_g=m_mem_ln_g, m_mem_ln_b=m_mem_ln_b, m_ln_g=m_ln_g, m_ln_b=m_ln_b, m_ffn1_w_gu=m_ffn1_w_gu, m_ffn1_w_down=m_ffn1_w_down, m_w_in=m_w_in, m_dn_conv_w=m_dn_conv_w, m_dn_a_log=m_dn_a_log, m_dn_dt_bias=m_dn_dt_bias, m_dn_norm_w=m_dn_norm_w, m_swa_sinks=m_swa_sinks, m_w_mem_kv=m_w_mem_kv, m_w_branch=m_w_branch, m_w_out=m_w_out, m_ffn2_w_gu=m_ffn2_w_gu, m_ffn2_w_down=m_ffn2_w_down, v_mem_ln_g=v_mem_ln_g, v_mem_ln_b=v_mem_ln_b, v_ln_g=v_ln_g, v_ln_b=v_ln_b, v_ffn1_w_gu=v_ffn1_w_gu, v_ffn1_w_down=v_ffn1_w_down, v_w_in=v_w_in, v_dn_conv_w=v_dn_conv_w, v_dn_a_log=v_dn_a_log, v_dn_dt_bias=v_dn_dt_bias, v_dn_norm_w=v_dn_norm_w, v_swa_sinks=v_swa_sinks, v_w_mem_kv=v_w_mem_kv, v_w_branch=v_w_branch, v_w_out=v_w_out, v_ffn2_w_gu=v_ffn2_w_gu, v_ffn2_w_down=v_ffn2_w_down)
    weights = {n: given[n] for n in TWIN_WEIGHTS}
    shared = {n: given[n] for n in SHARED_INPUTS}
    per_example = {n: given[n] for n in ['x', 'mem']}
    grad_fn = _jax.value_and_grad(_loss, argnums=(0, 1))

    def one_microbatch(ex, loss_target):
        ex = dict(ex)
        diff = ex.pop(TWIN_DIFF_INPUT)
        return grad_fn(weights, diff, {**shared, **ex}, loss_target)

    if N_MICROBATCH == 1:
        loss, (grad_w, grad_x) = one_microbatch(per_example, given["loss_target"])
    else:
        def body(carry, xs):
            loss_sum, grad_sum = carry
            l_k, (gw_k, gx_k) = one_microbatch(xs[0], xs[1])
            with _jax.named_scope("update"):
                return (loss_sum + l_k, _jax.tree.map(_jnp.add, grad_sum, gw_k)), gx_k

        init = (_jnp.zeros((), _jnp.float32), _jax.tree.map(_jnp.zeros_like, weights))
        (loss, grad_w), grad_x = _jax.lax.scan(body, init, (per_example, given["loss_target"]))
    with _jax.named_scope("update"):
        delta_w, new_m, new_v = {}, {}, {}
        for n in TWIN_WEIGHTS:
            delta_w[n], new_m[n], new_v[n] = _adamw(weights[n], grad_w[n], given["m_" + n], given["v_" + n])
    return (loss, grad_x, *[grad_w[n] for n in TWIN_WEIGHTS], *[delta_w[n] for n in TWIN_WEIGHTS],
            *[new_m[n] for n in TWIN_WEIGHTS], *[new_v[n] for n in TWIN_WEIGHTS])
```

```python
import functools

import numpy as np
import jax
import jax.numpy as jnp
from jax import lax
from jax.experimental import pallas as pl
from jax.experimental.pallas import tpu as pltpu

F32 = jnp.float32
BF16 = jnp.bfloat16
MESH = pl.DeviceIdType.MESH
N_DEV = 8
V7X_VMEM_LIMIT_BYTES = 56 * 1024 * 1024
LANES = 128
PACK_W = 1024

D_MODEL = 1024
DEPTH = 2
DN_HEADS, DN_DK, DN_DV, DN_CONV, DN_CHUNK = 4, 128, 128, 4, 64
SWA_HEADS, SWA_KV_HEADS, SWA_DH, WINDOW = 8, 2, 64, 128
XA_HEADS, XA_DH = 4, 128
D_FF = 2816
N_BRANCH, BRANCH_W = 3, 512
DEEPNORM_ALPHA = (2 * DEPTH) ** 0.25
LN_EPS = 1e-5
RMS_EPS = 1e-6
NEG_INF = -1e30
D_IN = 6408
D_IN_PAD = 6528
SEG = dict(dn_qkv=(0, 1536), dn_z=(1536, 2048), sw_q=(2048, 2560), sw_kv=(2560, 2816), xa_q=(2816, 3328),
           gates=(3328, 6400), dn_ba=(6400, 6528))

ADAM_LR, ADAM_B1, ADAM_B2, ADAM_EPS, ADAM_WD, ADAM_STEP = 0.001, 0.9, 0.999, 1e-08, 0.01, 10

BIG = ("ffn1_w_gu", "ffn1_w_down", "w_in", "w_mem_kv", "w_branch", "w_out", "ffn2_w_gu", "ffn2_w_down")
BIG_AXIS = dict(ffn1_w_gu=2, ffn1_w_down=1, w_in=2, w_mem_kv=1, w_branch=3, w_out=1, ffn2_w_gu=2, ffn2_w_down=1)
SMALL_SHARDED = ("ln_g", "ln_b", "dn_conv_w")
SMALL_REPL = ("mem_ln_g", "mem_ln_b", "dn_a_log", "dn_dt_bias", "dn_norm_w", "swa_sinks")
WEIGHTS = ("mem_ln_g", "mem_ln_b", "ln_g", "ln_b", "ffn1_w_gu", "ffn1_w_down", "w_in", "dn_conv_w", "dn_a_log",
           "dn_dt_bias", "dn_norm_w", "swa_sinks", "w_mem_kv", "w_branch", "w_out", "ffn2_w_gu", "ffn2_w_down")

HI = lax.Precision.HIGHEST
NN = (((1,), (0,)), ((), ()))
NT = (((1,), (1,)), ((), ()))
TN = (((0,), (0,)), ((), ()))


def _tile(n, target, align):
    best = None
    for d in range(align, min(n, target) + 1, align):
        if n % d == 0:
            best = d
    return n if best is None else best


def _my_index():
    return 4 * lax.axis_index("x") + 2 * lax.axis_index("y") + lax.axis_index("c")


def _peer(k):
    x, y, c = lax.axis_index("x"), lax.axis_index("y"), lax.axis_index("c")
    px = 1 - x if (k >> 2) & 1 else x
    py = 1 - y if (k >> 1) & 1 else y
    pc = 1 - c if k & 1 else c
    return (px, py, pc), 4 * px + 2 * py + pc


def _matmul(a, b, *, ta=False, tb=False, name, tm=512, tn=2176, tk=2048):
    m, k = (a.shape[1], a.shape[0]) if ta else a.shape
    k2, n = (b.shape[1], b.shape[0]) if tb else b.shape
    assert k == k2, (a.shape, b.shape, ta, tb)
    tm = _tile(m, tm, LANES if ta else 8)
    tn = _tile(n, tn, LANES)
    tk = _tile(k, tk, LANES)
    nk = k // tk
    dims = (((0 if ta else 1,), (1 if tb else 0,)), ((), ()))

    def body(a_ref, b_ref, o_ref):
        kk = pl.program_id(2)
        p = lax.dot_general(a_ref[...].astype(BF16), b_ref[...].astype(BF16), dims, preferred_element_type=F32)

        @pl.when(kk == 0)
        def _():
            o_ref[...] = p

        @pl.when(kk != 0)
        def _():
            o_ref[...] += p

    a_spec = pl.BlockSpec((tk, tm), lambda j, i, kk: (kk, i)) if ta else pl.BlockSpec((tm, tk), lambda j, i, kk: (i, kk))
    b_spec = pl.BlockSpec((tn, tk), lambda j, i, kk: (j, kk)) if tb else pl.BlockSpec((tk, tn), lambda j, i, kk: (kk, j))
    return pl.pallas_call(
        body, name=name, grid=(n // tn, m // tm, nk), in_specs=[a_spec, b_spec],
        out_specs=pl.BlockSpec((tm, tn), lambda j, i, kk: (i, j)),
        out_shape=jax.ShapeDtypeStruct((m, n), F32),
        compiler_params=pltpu.CompilerParams(dimension_semantics=("parallel", "parallel", "arbitrary"),
                                             vmem_limit_bytes=V7X_VMEM_LIMIT_BYTES),
    )(a, b)


@functools.partial(jax.custom_vjp, nondiff_argnums=(2,))
def pmm(x, w, tag):
    return _matmul(x, w, name=tag + "_fwd")


def _pmm_fwd(x, w, tag):
    return _matmul(x, w, name=tag + "_fwd"), (x, w)


def _pmm_bwd(tag, res, dy):
    x, w = res
    dx = _matmul(dy, w, tb=True, name=tag + "_dx")
    dw = _matmul(x, dy, ta=True, name=tag + "_dw", tm=1408, tn=2176, tk=512)
    return dx, dw


pmm.defvjp(_pmm_fwd, _pmm_bwd)


def _all_gather_packed(packed):
    r, w = packed.shape

    def body(x_ref, out_ref, send_sems, recv_sems, local_sem):
        me = _my_index()
        mine = pltpu.make_async_copy(x_ref, out_ref.at[me], local_sem)
        mine.start()
        sends = []
        for k in range(1, N_DEV):
            peer, _ = _peer(k)
            cp = pltpu.make_async_remote_copy(src_ref=x_ref, dst_ref=out_ref.at[me], send_sem=send_sems.at[k - 1],
                                              recv_sem=recv_sems.at[k - 1], device_id=peer, device_id_type=MESH)
            cp.start()
            sends.append(cp)
        for k in range(1, N_DEV):
            peer, p = _peer(k)
            pltpu.make_async_remote_copy(src_ref=x_ref, dst_ref=out_ref.at[p], send_sem=send_sems.at[k - 1],
                                         recv_sem=recv_sems.at[k - 1], device_id=peer, device_id_type=MESH).wait_recv()
        for cp in sends:
            cp.wait_send()
        mine.wait()

    return pl.pallas_call(
        body, name="all_gather_weights",
        out_shape=jax.ShapeDtypeStruct((N_DEV, r, w), packed.dtype),
        in_specs=[pl.BlockSpec(memory_space=pl.ANY)], out_specs=pl.BlockSpec(memory_space=pl.ANY),
        scratch_shapes=[pltpu.SemaphoreType.DMA((N_DEV - 1,)), pltpu.SemaphoreType.DMA((N_DEV - 1,)),
                        pltpu.SemaphoreType.DMA],
    )(packed)


def _exchange_slots(slots):
    _, r, w = slots.shape

    def body(x_ref, out_ref, send_sems, recv_sems, local_sem):
        me = _my_index()
        mine = pltpu.make_async_copy(x_ref.at[me], out_ref.at[me], local_sem)
        mine.start()
        sends = []
        for k in range(1, N_DEV):
            peer, p = _peer(k)
            cp = pltpu.make_async_remote_copy(src_ref=x_ref.at[p], dst_ref=out_ref.at[me], send_sem=send_sems.at[k - 1],
                                              recv_sem=recv_sems.at[k - 1], device_id=peer, device_id_type=MESH)
            cp.start()
            sends.append(cp)
        for k in range(1, N_DEV):
            peer, p = _peer(k)
            pltpu.make_async_remote_copy(src_ref=x_ref.at[p], dst_ref=out_ref.at[p], send_sem=send_sems.at[k - 1],
                                         recv_sem=recv_sems.at[k - 1], device_id=peer, device_id_type=MESH).wait_recv()
        for cp in sends:
            cp.wait_send()
        mine.wait()

    return pl.pallas_call(
        body, name="exchange_grad_slots",
        out_shape=jax.ShapeDtypeStruct(slots.shape, slots.dtype),
        in_specs=[pl.BlockSpec(memory_space=pl.ANY)], out_specs=pl.BlockSpec(memory_space=pl.ANY),
        scratch_shapes=[pltpu.SemaphoreType.DMA((N_DEV - 1,)), pltpu.SemaphoreType.DMA((N_DEV - 1,)),
                        pltpu.SemaphoreType.DMA],
    )(slots)


def _sum_slots(slots, rows_per_step):
    _, r, w = slots.shape
    tr = _tile(r, rows_per_step, 8)

    def body(x_ref, o_ref):
        acc = x_ref[0].astype(F32)
        for p in range(1, N_DEV):
            acc = acc + x_ref[p].astype(F32)
        o_ref[...] = acc

    return pl.pallas_call(
        body, name="sum_grad_slots", grid=(r // tr,),
        in_specs=[pl.BlockSpec((N_DEV, tr, w), lambda i: (0, i, 0))],
        out_specs=pl.BlockSpec((tr, w), lambda i: (i, 0)),
        out_shape=jax.ShapeDtypeStruct((r, w), F32),
        compiler_params=pltpu.CompilerParams(dimension_semantics=("parallel",), vmem_limit_bytes=V7X_VMEM_LIMIT_BYTES),
    )(slots)


def _small_all_gather(v, reduce, name):
    rows, w = v.shape

    def body(x_ref, out_ref, land_ref, send_sems, recv_sems):
        me = _my_index()
        sends = []
        for k in range(1, N_DEV):
            peer, _ = _peer(k)
            cp = pltpu.make_async_remote_copy(src_ref=x_ref, dst_ref=land_ref.at[me], send_sem=send_sems.at[k - 1],
                                              recv_sem=recv_sems.at[k - 1], device_id=peer, device_id_type=MESH)
            cp.start()
            sends.append(cp)
        land_ref[me] = x_ref[...]
        for k in range(1, N_DEV):
            peer, p = _peer(k)
            pltpu.make_async_remote_copy(src_ref=x_ref, dst_ref=land_ref.at[p], send_sem=send_sems.at[k - 1],
                                         recv_sem=recv_sems.at[k - 1], device_id=peer, device_id_type=MESH).wait_recv()
        for cp in sends:
            cp.wait_send()
        if reduce:
            acc = land_ref[0]
            for p in range(1, N_DEV):
                acc = acc + land_ref[p]
            out_ref[...] = acc
        else:
            out_ref[...] = land_ref[...]

    out_shape = (rows, w) if reduce else (N_DEV, rows, w)
    return pl.pallas_call(
        body, name=name,
        out_shape=jax.ShapeDtypeStruct(out_shape, F32),
        in_specs=[pl.BlockSpec(memory_space=pltpu.VMEM)], out_specs=pl.BlockSpec(memory_space=pltpu.VMEM),
        scratch_shapes=[pltpu.VMEM((N_DEV, rows, w), F32), pltpu.SemaphoreType.DMA((N_DEV - 1,)),
                        pltpu.SemaphoreType.DMA((N_DEV - 1,))],
    )(v)


def _adamw(w, g, m, v, name):
    shape = w.shape
    cols = shape[-1] if len(shape) > 1 else shape[0]
    w2, g2, m2, v2 = (t.reshape(-1, cols) for t in (w, g, m, v))
    rows = w2.shape[0]
    tr = _tile(rows, 256, 8)

    def body(w_ref, g_ref, m_ref, v_ref, d_ref, mo_ref, vo_ref):
        gv = g_ref[...]
        mn = ADAM_B1 * m_ref[...] + (1.0 - ADAM_B1) * gv
        vn = ADAM_B2 * v_ref[...] + (1.0 - ADAM_B2) * (gv * gv)
        m_hat = mn / (1.0 - ADAM_B1 ** ADAM_STEP)
        v_hat = vn / (1.0 - ADAM_B2 ** ADAM_STEP)
        d_ref[...] = -ADAM_LR * (m_hat / (jnp.sqrt(v_hat) + ADAM_EPS) + ADAM_WD * w_ref[...])
        mo_ref[...] = mn
        vo_ref[...] = vn

    spec = pl.BlockSpec((tr, cols), lambda i: (i, 0))
    outs = pl.pallas_call(
        body, name=name, grid=(rows // tr,), in_specs=[spec] * 4, out_specs=[spec] * 3,
        out_shape=[jax.ShapeDtypeStruct((rows, cols), F32)] * 3,
        compiler_params=pltpu.CompilerParams(dimension_semantics=("parallel",), vmem_limit_bytes=V7X_VMEM_LIMIT_BYTES),
    )(w2, g2, m2, v2)
    return tuple(o.reshape(shape) for o in outs)


def _bdot_raw(a, b, dims):
    return lax.dot_general(a.astype(BF16), b.astype(BF16), dims, preferred_element_type=F32)


@jax.custom_vjp
def _bdot_nn(a, b):
    return _bdot_raw(a, b, NN)


@jax.custom_vjp
def _bdot_nt(a, b):
    return _bdot_raw(a, b, NT)


@jax.custom_vjp
def _bdot_tn(a, b):
    return _bdot_raw(a, b, TN)


_bdot_nn.defvjp(lambda a, b: (_bdot_raw(a, b, NN), (a, b)), lambda r, ct: (_bdot_nt(ct, r[1]), _bdot_tn(r[0], ct)))
_bdot_nt.defvjp(lambda a, b: (_bdot_raw(a, b, NT), (a, b)), lambda r, ct: (_bdot_nn(ct, r[1]), _bdot_tn(ct, r[0])))
_bdot_tn.defvjp(lambda a, b: (_bdot_raw(a, b, TN), (a, b)), lambda r, ct: (_bdot_nt(r[1], ct), _bdot_nn(r[0], ct)))


def _hdot(a, b, dims=NN):
    return lax.dot_general(a, b, dims, precision=HI, preferred_element_type=F32)


def _rows(width, t):
    return (t, width), lambda i: (i, 0)


def _whole(shape):
    return tuple(shape), lambda i: (0,) * len(shape)


def _tiled_op(name, fn, steps, in_blocks, kinds, outs, grad_blocks=None, grad_fix=None):
    n_in, n_out = len(in_blocks), len(outs)
    diff = [j for j, kd in enumerate(kinds) if kd != "const"]
    has_param = any(kd == "param" for kd in kinds)
    in_specs = [pl.BlockSpec(b, m) for b, m in in_blocks]
    out_specs = [pl.BlockSpec(b, m) for _, b, m in outs]

    def forward(*arrays):
        def body(*refs):
            res = fn(pl.program_id(0), *[r[...] for r in refs[:n_in]])
            for o_ref, val in zip(refs[n_in:], res):
                o_ref[...] = val

        return pl.pallas_call(
            body, name=name + "_fwd", grid=(steps,), in_specs=in_specs, out_specs=out_specs,
            out_shape=[jax.ShapeDtypeStruct(s, F32) for s, _, _ in outs],
            compiler_params=pltpu.CompilerParams(dimension_semantics=("parallel",),
                                                 vmem_limit_bytes=V7X_VMEM_LIMIT_BYTES),
        )(*arrays)

    def backward(arrays, douts):
        g_blocks = [_whole(arrays[j].shape) if kinds[j] == "param" else (grad_blocks or {}).get(j, in_blocks[j])
                    for j in diff]

        def body(*refs):
            i = pl.program_id(0)
            vals = [r[...] for r in refs[:n_in]]
            dos = tuple(r[...] for r in refs[n_in:n_in + n_out])
            g_refs = refs[n_in + n_out:]

            def of_diff(*dv):
                full = list(vals)
                for j, val in zip(diff, dv):
                    full[j] = val
                return tuple(fn(i, *full))

            _, vjp = jax.vjp(of_diff, *[vals[j] for j in diff])
            grads = vjp(dos)
            for j, g_ref, g in zip(diff, g_refs, grads):
                if kinds[j] == "param":
                    @pl.when(i == 0)
                    def _(g_ref=g_ref, g=g):
                        g_ref[...] = g

                    @pl.when(i != 0)
                    def _(g_ref=g_ref, g=g):
                        g_ref[...] += g
                else:
                    g_ref[...] = g

        return pl.pallas_call(
            body, name=name + "_bwd", grid=(steps,), in_specs=in_specs + out_specs,
            out_specs=[pl.BlockSpec(b, m) for b, m in g_blocks],
            out_shape=[jax.ShapeDtypeStruct(arrays[j].shape, F32) for j in diff],
            compiler_params=pltpu.CompilerParams(dimension_semantics=("arbitrary" if has_param else "parallel",),
                                                 vmem_limit_bytes=V7X_VMEM_LIMIT_BYTES),
        )(*arrays, *douts)

    @jax.custom_vjp
    def op(*arrays):
        return tuple(forward(*arrays))

    def op_fwd(*arrays):
        return tuple(forward(*arrays)), arrays

    def op_bwd(arrays, douts):
        grads = backward(arrays, douts)
        full = [jnp.zeros_like(a) for a in arrays]
        for j, g in zip(diff, grads):
            full[j] = grad_fix[j](g) if grad_fix and j in grad_fix else g
        return tuple(full)

    op.defvjp(op_fwd, op_bwd)
    return op


def _sigmoid(x):
    return jax.nn.sigmoid(x)


def _silu(x):
    return x * jax.nn.sigmoid(x)


def _softplus(x):
    return jnp.maximum(x, 0.0) + jnp.log(1.0 + jnp.exp(-jnp.abs(x)))


def _heads(t, width):
    return [t[:, h * width:(h + 1) * width] for h in range(t.shape[1] // width)]


def _ffn_act(gu, name):
    s, two_f = gu.shape
    f = two_f // 2
    t = _tile(s, 128, 8)

    def fn(i, gu_t):
        return (_silu(gu_t[:, :f]) * gu_t[:, f:],)

    return _tiled_op(name, fn, s // t, [_rows(two_f, t)], ["row"], [((s, f),) + _rows(f, t)])(gu)[0]


def _res_ln(x, f, g, b, scale, name):
    s, d = x.shape
    t = _tile(s, 512, 8)

    def fn(i, x_t, f_t, g_t, b_t):
        r = DEEPNORM_ALPHA * x_t + scale * f_t
        mu = jnp.mean(r, axis=-1, keepdims=True)
        var = jnp.mean(jnp.square(r - mu), axis=-1, keepdims=True)
        return ((r - mu) * lax.rsqrt(var + LN_EPS) * g_t + b_t,)

    op = _tiled_op(name, fn, s // t, [_rows(d, t), _rows(d, t), _whole((1, d)), _whole((1, d))],
                   ["row", "row", "param", "param"], [((s, d),) + _rows(d, t)])
    return op(x, f, g.reshape(1, d), b.reshape(1, d))[0]


def _row_loss(y, target, name):
    s, d = y.shape
    t = _tile(s, 512, 8)

    def fn(i, y_t, t_t):
        e = y_t - t_t
        return (jnp.broadcast_to(0.5 * jnp.mean(e * e, axis=-1, keepdims=True), (t, LANES)),)

    return _tiled_op(name, fn, s // t, [_rows(d, t), _rows(d, t)], ["row", "const"],
                     [((s, LANES),) + _rows(LANES, t)])(y, target)[0]


def _gate_merge(gates, b0, b1, b2, name):
    s, d = b0.shape
    t = _tile(s, 256, 8)

    def fn(i, g_t, b0_t, b1_t, b2_t):
        return (_sigmoid(g_t[:, :d]) * b0_t + _sigmoid(g_t[:, d:2 * d]) * b1_t + _sigmoid(g_t[:, 2 * d:]) * b2_t,)

    op = _tiled_op(name, fn, s // t, [_rows(3 * d, t)] + [_rows(d, t)] * 3, ["row"] * 4, [((s, d),) + _rows(d, t)])
    return op(gates, b0, b1, b2)[0]


def _dn_pre(conv, ba, a_log, dt_bias, name):
    s = conv.shape[0]
    w = DN_HEADS * DN_DK
    t = _tile(s, 256, 8)
    pa = jnp.zeros((1, LANES), F32).at[0, DN_HEADS:2 * DN_HEADS].set(a_log)
    pd = jnp.zeros((1, LANES), F32).at[0, DN_HEADS:2 * DN_HEADS].set(dt_bias)

    def unit(x):
        return x * lax.rsqrt(jnp.sum(x * x, axis=-1, keepdims=True) + RMS_EPS)

    def fn(i, c_t, ba_t, pa_t, pd_t):
        act = _silu(c_t)
        q = jnp.concatenate([unit(x) for x in _heads(act[:, :w], DN_DK)], axis=1)
        k = jnp.concatenate([unit(x) for x in _heads(act[:, w:2 * w], DN_DK)], axis=1)
        beta = _sigmoid(ba_t)
        g = -jnp.exp(pa_t) * _softplus(ba_t + pd_t)
        src = lax.broadcasted_iota(jnp.int32, (LANES, w), 0)
        head = lax.broadcasted_iota(jnp.int32, (LANES, w), 1) >> int(np.log2(DN_DK))
        be = _hdot(beta, (src == head).astype(F32))
        ge = _hdot(g, (src == head + DN_HEADS).astype(F32))
        return q, k, act[:, 2 * w:], be, ge

    op = _tiled_op(name, fn, s // t, [_rows(3 * w, t), _rows(LANES, t), _whole((1, LANES)), _whole((1, LANES))],
                   ["row", "row", "param", "param"], [((s, w),) + _rows(w, t)] * 5)
    return op(conv, ba, pa, pd)


def _dn_post(o, z, norm_w, name):
    s, w = o.shape
    t = _tile(s, 512, 8)

    def fn(i, o_t, z_t, nw_t):
        outs = [x * lax.rsqrt(jnp.mean(x * x, axis=-1, keepdims=True) + RMS_EPS) * nw_t * _silu(zz)
                for x, zz in zip(_heads(o_t, DN_DV), _heads(z_t, DN_DV))]
        return (jnp.concatenate(outs, axis=1),)

    op = _tiled_op(name, fn, s // t, [_rows(w, t), _rows(w, t), _whole((1, DN_DV))], ["row", "row", "param"],
                   [((s, w),) + _rows(w, t)])
    return op(o, z, norm_w.reshape(1, DN_DV))[0]


def _xattn(q, k_mem, v_mem, name):
    s, w = q.shape
    t = _tile(s, 512, 8)

    def fn(i, q_t, k_t, v_t):
        outs = []
        for qh, kh, vh in zip(_heads(q_t, XA_DH), _heads(k_t, XA_DH), _heads(v_t, XA_DH)):
            sc = _bdot_nt(qh, kh) * (XA_DH ** -0.5)
            e = jnp.exp(sc - jnp.max(sc, axis=-1, keepdims=True))
            outs.append(_bdot_nn(e / jnp.sum(e, axis=-1, keepdims=True), vh))
        return (jnp.concatenate(outs, axis=1),)

    op = _tiled_op(name, fn, s // t, [_rows(w, t), _whole(k_mem.shape), _whole(v_mem.shape)],
                   ["row", "param", "param"], [((s, w),) + _rows(w, t)])
    return op(q, k_mem, v_mem)[0]


def _swa(q, kv, sinks, name):
    s, w = q.shape
    wd = WINDOW
    grp = SWA_HEADS // SWA_KV_HEADS
    dh = SWA_DH
    steps = s // wd
    sink_rows = jnp.broadcast_to(sinks[:, None], (SWA_HEADS, LANES))

    def fn(i, q_t, kc_t, kp_t, sk_t):
        qr = lax.broadcasted_iota(jnp.int32, (grp * wd, 2 * wd), 0) & (wd - 1)
        kc = lax.broadcasted_iota(jnp.int32, (grp * wd, 2 * wd), 1)
        dist = qr + wd - kc
        mask = (dist >= 0) & (dist < wd) & ((kc >= wd) | (i > 0))
        head_row = lax.broadcasted_iota(jnp.int32, (SWA_HEADS, LANES), 0)
        outs = [None] * SWA_HEADS
        for hk in range(SWA_KV_HEADS):
            kh = jnp.concatenate([kp_t[:, hk * dh:(hk + 1) * dh], kc_t[:, hk * dh:(hk + 1) * dh]], axis=0)
            vh = jnp.concatenate([kp_t[:, (SWA_KV_HEADS + hk) * dh:(SWA_KV_HEADS + hk + 1) * dh],
                                  kc_t[:, (SWA_KV_HEADS + hk) * dh:(SWA_KV_HEADS + hk + 1) * dh]], axis=0)
            qg = jnp.concatenate([q_t[:, (hk * grp + g) * dh:(hk * grp + g + 1) * dh] for g in range(grp)], axis=0)
            sink = jnp.concatenate(
                [jnp.broadcast_to(jnp.sum(jnp.where(head_row == hk * grp + g, sk_t, 0.0), axis=0, keepdims=True),
                                  (wd, LANES)) for g in range(grp)], axis=0)
            sink = jnp.max(sink, axis=-1, keepdims=True)
            sc = jnp.where(mask, _bdot_nt(qg, kh) * (dh ** -0.5), NEG_INF)
            m = jnp.maximum(jnp.max(sc, axis=-1, keepdims=True), sink)
            p = jnp.exp(sc - m)
            p = p / (jnp.sum(p, axis=-1, keepdims=True) + jnp.exp(sink - m))
            o = _bdot_nn(p, vh)
            for g in range(grp):
                outs[hk * grp + g] = o[g * wd:(g + 1) * wd]
        return (jnp.concatenate(outs, axis=1),)

    kvw = kv.shape[1]
    prev_block = ((wd, kvw), lambda i: (jnp.maximum(i - 1, 0), 0))
    shift = lambda g: jnp.concatenate([g[wd:], jnp.zeros((wd, kvw), F32)], axis=0)
    op = _tiled_op(name, fn, steps, [_rows(w, wd), _rows(kvw, wd), prev_block, _whole((SWA_HEADS, LANES))],
                   ["row", "row", "row", "param"], [((s, w),) + _rows(w, wd)],
                   grad_blocks={2: _rows(kvw, wd)}, grad_fix={2: shift})
    return op(q, kv, kv, sink_rows)[0]


def _dn_chunk(q, k, v, be, ge, s):
    c = q.shape[0]
    row = lax.broadcasted_iota(jnp.int32, (c, c), 0)
    col = lax.broadcasted_iota(jnp.int32, (c, c), 1)
    tril = row >= col
    gc = _hdot(tril.astype(F32), ge)
    g_row = _hdot(jnp.full((c, LANES), 1.0 / LANES, F32), gc, NT)
    decay = jnp.where(tril, jnp.exp(jnp.where(tril, gc[:, :c] - g_row, 0.0)), 0.0)
    kb = k * be
    a = jnp.where(row > col, _bdot_nt(kb, k) * decay, 0.0)
    x = (row == col).astype(F32)
    for lvl in range(int(np.log2(c))):
        off = ((row >> lvl) == (col >> lvl) + 1) & ((row >> (lvl + 1)) == (col >> (lvl + 1)))
        x = x - _hdot(x, _hdot(jnp.where(off, a, 0.0), x))
    eg = jnp.exp(gc)
    u = _hdot(x, v * be)
    w = _hdot(x, kb * eg)
    qs = q * (q.shape[1] ** -0.5)
    qk = jnp.where(tril, _bdot_nt(qs, k) * decay, 0.0)
    gl = jnp.sum(ge, axis=0, keepdims=True)
    v_new = u - _bdot_nn(w, s)
    o = _bdot_nn(qs * eg, s) + _bdot_nn(qk, v_new)
    s_next = s * jnp.exp(gl) + _bdot_tn(k * jnp.exp(gl - gc), v_new)
    return o, s_next


def _dn_forward(q, k, v, be, ge, name, chunks_per_step=4):
    s_len, width = q.shape
    heads = width // DN_DK
    n = s_len // DN_CHUNK
    cb = min(chunks_per_step, n)
    rows = cb * DN_CHUNK

    def body(q_ref, k_ref, v_ref, be_ref, ge_ref, o_ref, st_ref, s_scr):
        @pl.when(pl.program_id(0) == 0)
        def _():
            s_scr[...] = jnp.zeros_like(s_scr)

        for ci in range(cb):
            r = slice(ci * DN_CHUNK, (ci + 1) * DN_CHUNK)
            for h in range(heads):
                l = slice(h * DN_DK, (h + 1) * DN_DK)
                s = s_scr[h]
                st_ref[ci, h] = s
                o, s_next = _dn_chunk(q_ref[r, l], k_ref[r, l], v_ref[r, l], be_ref[r, l], ge_ref[r, l], s)
                o_ref[r, l] = o
                s_scr[h] = s_next

    tok = pl.BlockSpec((rows, width), lambda i: (i, 0))
    return pl.pallas_call(
        body, name=name + "_fwd", grid=(n // cb,), in_specs=[tok] * 5,
        out_specs=[tok, pl.BlockSpec((cb, heads, DN_DK, DN_DV), lambda i: (i, 0, 0, 0))],
        out_shape=[jax.ShapeDtypeStruct((s_len, width), F32), jax.ShapeDtypeStruct((n, heads, DN_DK, DN_DV), F32)],
        scratch_shapes=[pltpu.VMEM((heads, DN_DK, DN_DV), F32)],
        compiler_params=pltpu.CompilerParams(dimension_semantics=("arbitrary",), vmem_limit_bytes=V7X_VMEM_LIMIT_BYTES),
    )(q, k, v, be, ge)


def _dn_backward(q, k, v, be, ge, states, do, name, chunks_per_step=4):
    s_len, width = q.shape
    heads = width // DN_DK
    n = s_len // DN_CHUNK
    cb = min(chunks_per_step, n)
    rows = cb * DN_CHUNK
    steps = n // cb

    def body(q_ref, k_ref, v_ref, be_ref, ge_ref, st_ref, do_ref, dq_ref, dk_ref, dv_ref, dbe_ref, dge_ref, ds_scr):
        @pl.when(pl.program_id(0) == 0)
        def _():
            ds_scr[...] = jnp.zeros_like(ds_scr)

        for ci in reversed(range(cb)):
            r = slice(ci * DN_CHUNK, (ci + 1) * DN_CHUNK)
            for h in range(heads):
                l = slice(h * DN_DK, (h + 1) * DN_DK)
                _, vjp = jax.vjp(_dn_chunk, q_ref[r, l], k_ref[r, l], v_ref[r, l], be_ref[r, l], ge_ref[r, l],
                                 st_ref[ci, h])
                dq, dk, dv, dbe, dge, ds = vjp((do_ref[r, l], ds_scr[h]))
                dq_ref[r, l] = dq
                dk_ref[r, l] = dk
                dv_ref[r, l] = dv
                dbe_ref[r, l] = dbe
                dge_ref[r, l] = dge
                ds_scr[h] = ds

    tok = pl.BlockSpec((rows, width), lambda i: (steps - 1 - i, 0))
    return pl.pallas_call(
        body, name=name + "_bwd", grid=(steps,),
        in_specs=[tok] * 5 + [pl.BlockSpec((cb, heads, DN_DK, DN_DV), lambda i: (steps - 1 - i, 0, 0, 0)), tok],
        out_specs=[tok] * 5,
        out_shape=[jax.ShapeDtypeStruct((s_len, width), F32)] * 5,
        scratch_shapes=[pltpu.VMEM((heads, DN_DK, DN_DV), F32)],
        compiler_params=pltpu.CompilerParams(dimension_semantics=("arbitrary",), vmem_limit_bytes=V7X_VMEM_LIMIT_BYTES),
    )(q, k, v, be, ge, states, do)


@functools.partial(jax.custom_vjp, nondiff_argnums=(5,))
def _dn_core(q, k, v, be, ge, name):
    return _dn_forward(q, k, v, be, ge, name)[0]


def _dn_core_fwd(q, k, v, be, ge, name):
    o, states = _dn_forward(q, k, v, be, ge, name)
    return o, (q, k, v, be, ge, states)


def _dn_core_bwd(name, res, do):
    return tuple(_dn_backward(*res, do, name))


_dn_core.defvjp(_dn_core_fwd, _dn_core_bwd)


def _layer_norm(x, g, b):
    mu = x.mean(-1, keepdims=True)
    var = jnp.square(x - mu).mean(-1, keepdims=True)
    return (x - mu) * lax.rsqrt(var + LN_EPS) * g + b


def _causal_depthwise_conv(x, w):
    k, c = w.shape
    return lax.conv_general_dilated(x[None], w[:, None, :], window_strides=(1,), padding=[(k - 1, 0)],
                                    dimension_numbers=("NWC", "WIO", "NWC"), feature_group_count=c)[0]


def _swiglu(x, w_gu, w_down, tag):
    return pmm(_ffn_act(pmm(x, w_gu, tag + "_gu"), tag + "_act"), w_down, tag + "_down")


def _hybrid_layer(x, mem_n, p, l):
    tag = f"l{l}"
    ln_g, ln_b = p["ln_g"][l], p["ln_b"][l]
    h = _res_ln(x, _swiglu(x, p["ffn1_w_gu"][l], p["ffn1_w_down"][l], tag + "_ffn1"), ln_g[0], ln_b[0], 0.5,
                tag + "_ln0")
    proj = pmm(h, p["w_in"][l], tag + "_in")
    seg = {name: proj[:, lo:hi] for name, (lo, hi) in SEG.items()}
    conv = _causal_depthwise_conv(seg["dn_qkv"], p["dn_conv_w"][l])
    q, k, v, be, ge = _dn_pre(conv, seg["dn_ba"], p["dn_a_log"][l], p["dn_dt_bias"][l], tag + "_dn_pre")
    o_dn = _dn_post(_dn_core(q, k, v, be, ge, tag + "_dn"), seg["dn_z"], p["dn_norm_w"][l], tag + "_dn_post")
    o_sw = _swa(seg["sw_q"], seg["sw_kv"], p["swa_sinks"][l], tag + "_swa")
    kv_mem = pmm(mem_n, p["w_mem_kv"][l], tag + "_memkv")
    half = XA_HEADS * XA_DH
    o_xa = _xattn(seg["xa_q"], kv_mem[:, :half], kv_mem[:, half:], tag + "_xattn")
    branch = [pmm(o, p["w_branch"][l, i], tag + f"_branch{i}") for i, o in enumerate((o_dn, o_sw, o_xa))]
    merged = _gate_merge(seg["gates"], *branch, tag + "_merge")
    h = _res_ln(h, pmm(merged, p["w_out"][l], tag + "_out"), ln_g[1], ln_b[1], 1.0, tag + "_ln1")
    return _res_ln(h, _swiglu(h, p["ffn2_w_gu"][l], p["ffn2_w_down"][l], tag + "_ffn2"), ln_g[2], ln_b[2], 0.5,
                   tag + "_ln2")


def _local_loss(p, x, mem, target):
    mem_n = _layer_norm(mem, p["mem_ln_g"], p["mem_ln_b"])
    y = x
    for l in range(DEPTH):
        y = _hybrid_layer(y, mem_n, p, l)
    return jnp.sum(_row_loss(y, target, "row_loss")[:, 0])


def _w_in_permute(w):
    pad = jnp.zeros(w.shape[:-1] + (D_IN_PAD - D_IN,), w.dtype)
    return jnp.concatenate([w[..., :1536], w[..., 1544:], w[..., 1536:1544], pad], axis=-1)


def _w_in_unpermute(w):
    return jnp.concatenate([w[..., :1536], w[..., 6400:6408], w[..., 1536:6400]], axis=-1)


def _packed_rows(shards):
    total = sum(int(np.prod(shards[n].shape)) for n in BIG)
    rows = -(-total // PACK_W)
    return -(-rows // 8) * 8, total


def _pack_shards(shards, dtype):
    rows, total = _packed_rows(shards)
    flat = jnp.concatenate([shards[n].reshape(-1).astype(dtype) for n in BIG]
                           + [jnp.zeros((rows * PACK_W - total,), dtype)])
    return flat.reshape(rows, PACK_W)


def _unpack_gathered(gathered, shards):
    flat = gathered.reshape(N_DEV, -1)
    out, off = {}, 0
    for n in BIG:
        shp, ax = shards[n].shape, BIG_AXIS[n]
        size = int(np.prod(shp))
        t = flat[:, off:off + size].reshape((N_DEV,) + shp).astype(F32)
        t = jnp.moveaxis(t, 0, ax)
        out[n] = t.reshape(shp[:ax] + (N_DEV * shp[ax],) + shp[ax + 1:])
        off += size
    return out


def _pack_full_grads(grads, shards):
    rows, total = _packed_rows(shards)
    parts = []
    for n in BIG:
        shp, ax = shards[n].shape, BIG_AXIS[n]
        t = grads[n].reshape(shp[:ax] + (N_DEV, shp[ax]) + shp[ax + 1:])
        parts.append(jnp.moveaxis(t, ax, 0).reshape(N_DEV, -1))
    parts.append(jnp.zeros((N_DEV, rows * PACK_W - total), F32))
    return jnp.concatenate(parts, axis=1).reshape(N_DEV, rows, PACK_W)


def _unpack_shards(packed, shards):
    flat = packed.reshape(-1)
    out, off = {}, 0
    for n in BIG:
        size = int(np.prod(shards[n].shape))
        out[n] = flat[off:off + size].reshape(shards[n].shape)
        off += size
    return out


def _pack_small(parts):
    flat = jnp.concatenate([t.reshape(-1).astype(F32) for t in parts])
    rows = -(-flat.shape[0] // LANES)
    rows = -(-rows // 8) * 8
    return jnp.concatenate([flat, jnp.zeros((rows * LANES - flat.shape[0],), F32)]).reshape(rows, LANES)


def _unpack_small(packed, shapes):
    flat = packed.reshape(packed.shape[:-2] + (-1,))
    out, off = [], 0
    for shp in shapes:
        size = int(np.prod(shp))
        out.append(flat[..., off:off + size].reshape(flat.shape[:-1] + tuple(shp)))
        off += size
    return out


def kernel(x, mem, mem_ln_g, mem_ln_b, ln_g, ln_b, ffn1_w_gu, ffn1_w_down, w_in, dn_conv_w, dn_a_log, dn_dt_bias, dn_norm_w, swa_sinks, w_mem_kv, w_branch, w_out, ffn2_w_gu, ffn2_w_down, loss_target, m_mem_ln_g, m_mem_ln_b, m_ln_g, m_ln_b, m_ffn1_w_gu, m_ffn1_w_down, m_w_in, m_dn_conv_w, m_dn_a_log, m_dn_dt_bias, m_dn_norm_w, m_swa_sinks, m_w_mem_kv, m_w_branch, m_w_out, m_ffn2_w_gu, m_ffn2_w_down, v_mem_ln_g, v_mem_ln_b, v_ln_g, v_ln_b, v_ffn1_w_gu, v_ffn1_w_down, v_w_in, v_dn_conv_w, v_dn_a_log, v_dn_dt_bias, v_dn_norm_w, v_swa_sinks, v_w_mem_kv, v_w_branch, v_w_out, v_ffn2_w_gu, v_ffn2_w_down):
    given = dict(locals())
    w_loc = {n: given[n] for n in WEIGHTS}
    m_loc = {n: given["m_" + n] for n in WEIGHTS}
    v_loc = {n: given["v_" + n] for n in WEIGHTS}
    me = _my_index()

    big_shards = {n: w_loc[n] for n in BIG}
    gathered = _all_gather_packed(_pack_shards(big_shards, BF16))
    params = _unpack_gathered(gathered, big_shards)
    params["w_in"] = _w_in_permute(params["w_in"])
    small_shard_shapes = [w_loc[n].shape for n in SMALL_SHARDED]
    small_gathered = _small_all_gather(_pack_small([w_loc[n] for n in SMALL_SHARDED]), False, "all_gather_small")
    for n, t in zip(SMALL_SHARDED, _unpack_small(small_gathered, small_shard_shapes)):
        t = jnp.moveaxis(t, 0, -2)
        params[n] = t.reshape(t.shape[:-2] + (-1,))
    for n in SMALL_REPL:
        params[n] = w_loc[n]

    loss_local, (g_params, grad_x) = jax.value_and_grad(_local_loss, argnums=(0, 1))(
        params, x[0], mem[0], loss_target[0])
    grad_x = grad_x[None]
    g_params["w_in"] = _w_in_unpermute(g_params["w_in"])

    g_big = _unpack_shards(_sum_slots(_exchange_slots(_pack_full_grads(g_params, big_shards)), 232), big_shards)
    small_names = SMALL_SHARDED + SMALL_REPL
    small_full_shapes = [g_params[n].shape for n in small_names] + [()]
    small_sum = _small_all_gather(_pack_small([g_params[n] for n in small_names] + [loss_local]), True, "all_sum_small")
    small_full = dict(zip(small_names + ("loss",), _unpack_small(small_sum, small_full_shapes)))
    loss = small_full["loss"]
    grads = dict(g_big)
    for n in SMALL_SHARDED:
        shard = w_loc[n].shape[-1]
        grads[n] = lax.dynamic_slice_in_dim(small_full[n], me * shard, shard, axis=small_full[n].ndim - 1)
    for n in SMALL_REPL:
        grads[n] = small_full[n]

    delta, new_m, new_v = {}, {}, {}
    for n in BIG:
        delta[n], new_m[n], new_v[n] = _adamw(w_loc[n], grads[n], m_loc[n], v_loc[n], "adamw_" + n)
    shapes = [w_loc[n].shape for n in small_names]
    packed = [_pack_small([src[n] for n in small_names]) for src in (w_loc, grads, m_loc, v_loc)]
    for dst, t in zip((delta, new_m, new_v), _adamw(*packed, "adamw_small")):
        dst.update(zip(small_names, _unpack_small(t, shapes)))

    return (loss, grad_x, *[grads[n] for n in WEIGHTS], *[delta[n] for n in WEIGHTS],
            *[new_m[n] for n in WEIGHTS], *[new_v[n] for n in WEIGHTS])
```

```python
import functools

import numpy as np
import jax
import jax.numpy as jnp
from jax import lax
from jax.experimental import pallas as pl
from jax.experimental.pallas import tpu as pltpu

F32 = jnp.float32
BF16 = jnp.bfloat16
MESH = pl.DeviceIdType.MESH
N_DEV = 8
V7X_VMEM_LIMIT_BYTES = 56 * 1024 * 1024
LANES = 128
PACK_W = 1024

D_MODEL = 1024
DEPTH = 2
DN_HEADS, DN_DK, DN_DV, DN_CONV, DN_CHUNK = 4, 128, 128, 4, 64
SWA_HEADS, SWA_KV_HEADS, SWA_DH, WINDOW = 8, 2, 64, 128
XA_HEADS, XA_DH = 4, 128
D_FF = 2816
N_BRANCH, BRANCH_W = 3, 512
DEEPNORM_ALPHA = (2 * DEPTH) ** 0.25
LN_EPS = 1e-5
RMS_EPS = 1e-6
NEG_INF = -1e30
D_IN = 6408
D_IN_PAD = 6528
SEG = dict(dn_qkv=(0, 1536), dn_z=(1536, 2048), sw_q=(2048, 2560), sw_kv=(2560, 2816), xa_q=(2816, 3328),
           gates=(3328, 6400), dn_ba=(6400, 6528))

ADAM_LR, ADAM_B1, ADAM_B2, ADAM_EPS, ADAM_WD, ADAM_STEP = 0.001, 0.9, 0.999, 1e-08, 0.01, 10

BIG = ("ffn1_w_gu", "ffn1_w_down", "w_in", "w_mem_kv", "w_branch", "w_out", "ffn2_w_gu", "ffn2_w_down")
BIG_AXIS = dict(ffn1_w_gu=2, ffn1_w_down=1, w_in=2, w_mem_kv=1, w_branch=3, w_out=1, ffn2_w_gu=2, ffn2_w_down=1)
SMALL_SHARDED = ("ln_g", "ln_b", "dn_conv_w")
SMALL_REPL = ("mem_ln_g", "mem_ln_b", "dn_a_log", "dn_dt_bias", "dn_norm_w", "swa_sinks")
WEIGHTS = ("mem_ln_g", "mem_ln_b", "ln_g", "ln_b", "ffn1_w_gu", "ffn1_w_down", "w_in", "dn_conv_w", "dn_a_log",
           "dn_dt_bias", "dn_norm_w", "swa_sinks", "w_mem_kv", "w_branch", "w_out", "ffn2_w_gu", "ffn2_w_down")

HI = lax.Precision.HIGHEST
NN = (((1,), (0,)), ((), ()))
NT = (((1,), (1,)), ((), ()))
TN = (((0,), (0,)), ((), ()))


def _tile(n, target, align):
    best = None
    for d in range(align, min(n, target) + 1, align):
        if n % d == 0:
            best = d
    return n if best is None else best


def _my_index():
    return 4 * lax.axis_index("x") + 2 * lax.axis_index("y") + lax.axis_index("c")


def _peer(k):
    x, y, c = lax.axis_index("x"), lax.axis_index("y"), lax.axis_index("c")
    px = 1 - x if (k >> 2) & 1 else x
    py = 1 - y if (k >> 1) & 1 else y
    pc = 1 - c if k & 1 else c
    return (px, py, pc), 4 * px + 2 * py + pc


def _matmul(a, b, *, ta=False, tb=False, name, tm=512, tn=2176, tk=2048):
    m, k = (a.shape[1], a.shape[0]) if ta else a.shape
    k2, n = (b.shape[1], b.shape[0]) if tb else b.shape
    assert k == k2, (a.shape, b.shape, ta, tb)
    tm = _tile(m, tm, LANES if ta else 8)
    tn = _tile(n, tn, LANES)
    tk = _tile(k, tk, LANES)
    nk = k // tk
    dims = (((0 if ta else 1,), (1 if tb else 0,)), ((), ()))

    def body(a_ref, b_ref, o_ref):
        kk = pl.program_id(2)
        p = lax.dot_general(a_ref[...].astype(BF16), b_ref[...].astype(BF16), dims, preferred_element_type=F32)

        @pl.when(kk == 0)
        def _():
            o_ref[...] = p

        @pl.when(kk != 0)
        def _():
            o_ref[...] += p

    a_spec = pl.BlockSpec((tk, tm), lambda j, i, kk: (kk, i)) if ta else pl.BlockSpec((tm, tk), lambda j, i, kk: (i, kk))
    b_spec = pl.BlockSpec((tn, tk), lambda j, i, kk: (j, kk)) if tb else pl.BlockSpec((tk, tn), lambda j, i, kk: (kk, j))
    return pl.pallas_call(
        body, name=name, grid=(n // tn, m // tm, nk), in_specs=[a_spec, b_spec],
        out_specs=pl.BlockSpec((tm, tn), lambda j, i, kk: (i, j)),
        out_shape=jax.ShapeDtypeStruct((m, n), F32),
        compiler_params=pltpu.CompilerParams(dimension_semantics=("parallel", "parallel", "arbitrary"),
                                             vmem_limit_bytes=V7X_VMEM_LIMIT_BYTES),
    )(a, b)


@functools.partial(jax.custom_vjp, nondiff_argnums=(2,))
def pmm(x, w, tag):
    return _matmul(x, w, name=tag + "_fwd")


def _pmm_fwd(x, w, tag):
    return _matmul(x, w, name=tag + "_fwd"), (x, w)


def _pmm_bwd(tag, res, dy):
    x, w = res
    dx = _matmul(dy, w, tb=True, name=tag + "_dx")
    dw = _matmul(x, dy, ta=True, name=tag + "_dw", tm=1408, tn=2176, tk=512)
    return dx, dw


pmm.defvjp(_pmm_fwd, _pmm_bwd)


def _all_gather_packed(packed):
    r, w = packed.shape

    def body(x_ref, out_ref, send_sems, recv_sems, local_sem):
        me = _my_index()
        mine = pltpu.make_async_copy(x_ref, out_ref.at[me], local_sem)
        mine.start()
        sends = []
        for k in range(1, N_DEV):
            peer, _ = _peer(k)
            cp = pltpu.make_async_remote_copy(src_ref=x_ref, dst_ref=out_ref.at[me], send_sem=send_sems.at[k - 1],
                                              recv_sem=recv_sems.at[k - 1], device_id=peer, device_id_type=MESH)
            cp.start()
            sends.append(cp)
        for k in range(1, N_DEV):
            peer, p = _peer(k)
            pltpu.make_async_remote_copy(src_ref=x_ref, dst_ref=out_ref.at[p], send_sem=send_sems.at[k - 1],
                                         recv_sem=recv_sems.at[k - 1], device_id=peer, device_id_type=MESH).wait_recv()
        for cp in sends:
            cp.wait_send()
        mine.wait()

    return pl.pallas_call(
        body, name="all_gather_weights",
        out_shape=jax.ShapeDtypeStruct((N_DEV, r, w), packed.dtype),
        in_specs=[pl.BlockSpec(memory_space=pl.ANY)], out_specs=pl.BlockSpec(memory_space=pl.ANY),
        scratch_shapes=[pltpu.SemaphoreType.DMA((N_DEV - 1,)), pltpu.SemaphoreType.DMA((N_DEV - 1,)),
                        pltpu.SemaphoreType.DMA],
    )(packed)


def _exchange_slots(slots):
    _, r, w = slots.shape

    def body(x_ref, out_ref, send_sems, recv_sems, local_sem):
        me = _my_index()
        mine = pltpu.make_async_copy(x_ref.at[me], out_ref.at[me], local_sem)
        mine.start()
        sends = []
        for k in range(1, N_DEV):
            peer, p = _peer(k)
            cp = pltpu.make_async_remote_copy(src_ref=x_ref.at[p], dst_ref=out_ref.at[me], send_sem=send_sems.at[k - 1],
                                              recv_sem=recv_sems.at[k - 1], device_id=peer, device_id_type=MESH)
            cp.start()
            sends.append(cp)
        for k in range(1, N_DEV):
            peer, p = _peer(k)
            pltpu.make_async_remote_copy(src_ref=x_ref.at[p], dst_ref=out_ref.at[p], send_sem=send_sems.at[k - 1],
                                         recv_sem=recv_sems.at[k - 1], device_id=peer, device_id_type=MESH).wait_recv()
        for cp in sends:
            cp.wait_send()
        mine.wait()

    return pl.pallas_call(
        body, name="exchange_grad_slots",
        out_shape=jax.ShapeDtypeStruct(slots.shape, slots.dtype),
        in_specs=[pl.BlockSpec(memory_space=pl.ANY)], out_specs=pl.BlockSpec(memory_space=pl.ANY),
        scratch_shapes=[pltpu.SemaphoreType.DMA((N_DEV - 1,)), pltpu.SemaphoreType.DMA((N_DEV - 1,)),
                        pltpu.SemaphoreType.DMA],
    )(slots)


def _sum_slots(slots, rows_per_step):
    _, r, w = slots.shape
    tr = _tile(r, rows_per_step, 8)

    def body(x_ref, o_ref):
        acc = x_ref[0].astype(F32)
        for p in range(1, N_DEV):
            acc = acc + x_ref[p].astype(F32)
        o_ref[...] = acc

    return pl.pallas_call(
        body, name="sum_grad_slots", grid=(r // tr,),
        in_specs=[pl.BlockSpec((N_DEV, tr, w), lambda i: (0, i, 0))],
        out_specs=pl.BlockSpec((tr, w), lambda i: (i, 0)),
        out_shape=jax.ShapeDtypeStruct((r, w), F32),
        compiler_params=pltpu.CompilerParams(dimension_semantics=("parallel",), vmem_limit_bytes=V7X_VMEM_LIMIT_BYTES),
    )(slots)


def _small_all_gather(v, reduce, name):
    rows, w = v.shape

    def body(x_ref, out_ref, land_ref, send_sems, recv_sems):
        me = _my_index()
        sends = []
        for k in range(1, N_DEV):
            peer, _ = _peer(k)
            cp = pltpu.make_async_remote_copy(src_ref=x_ref, dst_ref=land_ref.at[me], send_sem=send_sems.at[k - 1],
                                              recv_sem=recv_sems.at[k - 1], device_id=peer, device_id_type=MESH)
            cp.start()
            sends.append(cp)
        land_ref[me] = x_ref[...]
        for k in range(1, N_DEV):
            peer, p = _peer(k)
            pltpu.make_async_remote_copy(src_ref=x_ref, dst_ref=land_ref.at[p], send_sem=send_sems.at[k - 1],
                                         recv_sem=recv_sems.at[k - 1], device_id=peer, device_id_type=MESH).wait_recv()
        for cp in sends:
            cp.wait_send()
        if reduce:
            acc = land_ref[0]
            for p in range(1, N_DEV):
                acc = acc + land_ref[p]
            out_ref[...] = acc
        else:
            out_ref[...] = land_ref[...]

    out_shape = (rows, w) if reduce else (N_DEV, rows, w)
    return pl.pallas_call(
        body, name=name,
        out_shape=jax.ShapeDtypeStruct(out_shape, F32),
        in_specs=[pl.BlockSpec(memory_space=pltpu.VMEM)], out_specs=pl.BlockSpec(memory_space=pltpu.VMEM),
        scratch_shapes=[pltpu.VMEM((N_DEV, rows, w), F32), pltpu.SemaphoreType.DMA((N_DEV - 1,)),
                        pltpu.SemaphoreType.DMA((N_DEV - 1,))],
    )(v)


def _adamw(w, g, m, v, name):
    shape = w.shape
    cols = shape[-1] if len(shape) > 1 else shape[0]
    w2, g2, m2, v2 = (t.reshape(-1, cols) for t in (w, g, m, v))
    rows = w2.shape[0]
    tr = _tile(rows, 256, 8)

    def body(w_ref, g_ref, m_ref, v_ref, d_ref, mo_ref, vo_ref):
        gv = g_ref[...]
        mn = ADAM_B1 * m_ref[...] + (1.0 - ADAM_B1) * gv
        vn = ADAM_B2 * v_ref[...] + (1.0 - ADAM_B2) * (gv * gv)
        m_hat = mn / (1.0 - ADAM_B1 ** ADAM_STEP)
        v_hat = vn / (1.0 - ADAM_B2 ** ADAM_STEP)
        d_ref[...] = -ADAM_LR * (m_hat / (jnp.sqrt(v_hat) + ADAM_EPS) + ADAM_WD * w_ref[...])
        mo_ref[...] = mn
        vo_ref[...] = vn

    spec = pl.BlockSpec((tr, cols), lambda i: (i, 0))
    outs = pl.pallas_call(
        body, name=name, grid=(rows // tr,), in_specs=[spec] * 4, out_specs=[spec] * 3,
        out_shape=[jax.ShapeDtypeStruct((rows, cols), F32)] * 3,
        compiler_params=pltpu.CompilerParams(dimension_semantics=("parallel",), vmem_limit_bytes=V7X_VMEM_LIMIT_BYTES),
    )(w2, g2, m2, v2)
    return tuple(o.reshape(shape) for o in outs)


_DOT_DIMS = {2: dict(nn=NN, nt=NT, tn=TN),
             3: dict(nn=(((2,), (1,)), ((0,), (0,))), nt=(((2,), (2,)), ((0,), (0,))), tn=(((1,), (1,)), ((0,), (0,))))}


def _bf16_dot(a, b, kind):
    if a.ndim == 3:
        return jnp.stack([_bf16_dot(a[i], b[i] if b.ndim == 3 else b, kind) for i in range(a.shape[0])])
    return lax.dot_general(a.astype(BF16), b.astype(BF16), _DOT_DIMS[2][kind], preferred_element_type=F32)


def _f32_dot(a, b, kind):
    if a.ndim == 3:
        return jnp.stack([_f32_dot(a[i], b[i] if b.ndim == 3 else b, kind) for i in range(a.shape[0])])
    return lax.dot_general(a, b, _DOT_DIMS[2][kind], precision=HI, preferred_element_type=F32)


def _dot_family(raw):
    @jax.custom_vjp
    def nn(a, b):
        return raw(a, b, "nn")

    @jax.custom_vjp
    def nt(a, b):
        return raw(a, b, "nt")

    @jax.custom_vjp
    def tn(a, b):
        return raw(a, b, "tn")

    nn.defvjp(lambda a, b: (raw(a, b, "nn"), (a, b)), lambda r, ct: (nt(ct, r[1]), tn(r[0], ct)))
    nt.defvjp(lambda a, b: (raw(a, b, "nt"), (a, b)), lambda r, ct: (nn(ct, r[1]), tn(ct, r[0])))
    tn.defvjp(lambda a, b: (raw(a, b, "tn"), (a, b)), lambda r, ct: (nt(r[1], ct), nn(r[0], ct)))
    return nn, nt, tn


_bdot_nn, _bdot_nt, _bdot_tn = _dot_family(_bf16_dot)
_hdot_nn, _hdot_nt, _hdot_tn = _dot_family(_f32_dot)


def _rows(width, t):
    return (t, width), lambda i: (i, 0)


def _whole(shape):
    return tuple(shape), lambda i: (0,) * len(shape)


def _tiled_op(name, fn, steps, in_blocks, kinds, outs, grad_blocks=None, grad_fix=None):
    n_in, n_out = len(in_blocks), len(outs)
    diff = [j for j, kd in enumerate(kinds) if kd != "const"]
    has_param = any(kd == "param" for kd in kinds)
    in_specs = [pl.BlockSpec(b, m) for b, m in in_blocks]
    out_specs = [pl.BlockSpec(b, m) for _, b, m in outs]

    def forward(*arrays):
        def body(*refs):
            res = fn(pl.program_id(0), *[r[...] for r in refs[:n_in]])
            for o_ref, val in zip(refs[n_in:], res):
                o_ref[...] = val

        return pl.pallas_call(
            body, name=name + "_fwd", grid=(steps,), in_specs=in_specs, out_specs=out_specs,
            out_shape=[jax.ShapeDtypeStruct(s, F32) for s, _, _ in outs],
            compiler_params=pltpu.CompilerParams(dimension_semantics=("parallel",),
                                                 vmem_limit_bytes=V7X_VMEM_LIMIT_BYTES),
        )(*arrays)

    def backward(arrays, douts):
        g_specs = []
        for j in diff:
            if kinds[j] == "param":
                g_specs.append((arrays[j].shape,) + _whole(arrays[j].shape))
            else:
                g_specs.append((grad_blocks or {}).get(j, (arrays[j].shape,) + in_blocks[j]))

        def body(*refs):
            i = pl.program_id(0)
            vals = [r[...] for r in refs[:n_in]]
            dos = tuple(r[...] for r in refs[n_in:n_in + n_out])
            g_refs = refs[n_in + n_out:]

            def of_diff(*dv):
                full = list(vals)
                for j, val in zip(diff, dv):
                    full[j] = val
                return tuple(fn(i, *full))

            _, vjp = jax.vjp(of_diff, *[vals[j] for j in diff])
            grads = vjp(dos)
            for j, g_ref, g in zip(diff, g_refs, grads):
                if kinds[j] == "param":
                    @pl.when(i == 0)
                    def _(g_ref=g_ref, g=g):
                        g_ref[...] = g

                    @pl.when(i != 0)
                    def _(g_ref=g_ref, g=g):
                        g_ref[...] += g
                else:
                    g_ref[...] = g

        return pl.pallas_call(
            body, name=name + "_bwd", grid=(steps,), in_specs=in_specs + out_specs,
            out_specs=[pl.BlockSpec(b, m) for _, b, m in g_specs],
            out_shape=[jax.ShapeDtypeStruct(shp, F32) for shp, _, _ in g_specs],
            compiler_params=pltpu.CompilerParams(dimension_semantics=("arbitrary" if has_param else "parallel",),
                                                 vmem_limit_bytes=V7X_VMEM_LIMIT_BYTES),
        )(*arrays, *douts)

    @jax.custom_vjp
    def op(*arrays):
        return tuple(forward(*arrays))

    def op_fwd(*arrays):
        return tuple(forward(*arrays)), arrays

    def op_bwd(arrays, douts):
        grads = backward(arrays, douts)
        full = [jnp.zeros_like(a) for a in arrays]
        for j, g in zip(diff, grads):
            full[j] = grad_fix[j](g) if grad_fix and j in grad_fix else g
        return tuple(full)

    op.defvjp(op_fwd, op_bwd)
    return op


def _sigmoid(x):
    return jax.nn.sigmoid(x)


def _silu(x):
    return x * jax.nn.sigmoid(x)


def _softplus(x):
    return jnp.maximum(x, 0.0) + jnp.log(1.0 + jnp.exp(-jnp.abs(x)))


def _heads(t, width):
    return [t[:, h * width:(h + 1) * width] for h in range(t.shape[1] // width)]


def _ffn_act(gu, name):
    s, two_f = gu.shape
    f = two_f // 2
    t = _tile(s, 128, 8)

    def fn(i, gu_t):
        return (_silu(gu_t[:, :f]) * gu_t[:, f:],)

    return _tiled_op(name, fn, s // t, [_rows(two_f, t)], ["row"], [((s, f),) + _rows(f, t)])(gu)[0]


def _res_ln(x, f, g, b, scale, name):
    s, d = x.shape
    t = _tile(s, 512, 8)

    def fn(i, x_t, f_t, g_t, b_t):
        r = DEEPNORM_ALPHA * x_t + scale * f_t
        mu = jnp.mean(r, axis=-1, keepdims=True)
        var = jnp.mean(jnp.square(r - mu), axis=-1, keepdims=True)
        return ((r - mu) * lax.rsqrt(var + LN_EPS) * g_t + b_t,)

    op = _tiled_op(name, fn, s // t, [_rows(d, t), _rows(d, t), _whole((1, d)), _whole((1, d))],
                   ["row", "row", "param", "param"], [((s, d),) + _rows(d, t)])
    return op(x, f, g.reshape(1, d), b.reshape(1, d))[0]


def _row_loss(y, target, name):
    s, d = y.shape
    t = _tile(s, 512, 8)

    def fn(i, y_t, t_t):
        e = y_t - t_t
        return (jnp.broadcast_to(0.5 * jnp.mean(e * e, axis=-1, keepdims=True), (t, LANES)),)

    return _tiled_op(name, fn, s // t, [_rows(d, t), _rows(d, t)], ["row", "const"],
                     [((s, LANES),) + _rows(LANES, t)])(y, target)[0]


def _gate_merge(gates, b0, b1, b2, name):
    s, d = b0.shape
    t = _tile(s, 256, 8)

    def fn(i, g_t, b0_t, b1_t, b2_t):
        return (_sigmoid(g_t[:, :d]) * b0_t + _sigmoid(g_t[:, d:2 * d]) * b1_t + _sigmoid(g_t[:, 2 * d:]) * b2_t,)

    op = _tiled_op(name, fn, s // t, [_rows(3 * d, t)] + [_rows(d, t)] * 3, ["row"] * 4, [((s, d),) + _rows(d, t)])
    return op(gates, b0, b1, b2)[0]


def _dn_pre(x, ba, conv_w, a_log, dt_bias, name):
    s, xw = x.shape
    w = DN_HEADS * DN_DK
    t = _tile(s, 256, 8)
    steps = s // t
    lane_pad = jnp.zeros((LANES - 2 * DN_HEADS,), F32)
    pa = jnp.concatenate([jnp.zeros((DN_HEADS,), F32), a_log, lane_pad]).reshape(1, LANES)
    pd = jnp.concatenate([jnp.zeros((DN_HEADS,), F32), dt_bias, lane_pad]).reshape(1, LANES)
    cw = jnp.concatenate([conv_w, jnp.zeros((8 - DN_CONV, xw), F32)], axis=0)
    taps = [jnp.pad(x, ((DN_CONV - 1 - j, 0), (0, 0)))[:s] for j in range(DN_CONV - 1)] + [x]

    def unit(x):
        return x * lax.rsqrt(jnp.sum(x * x, axis=-1, keepdims=True) + RMS_EPS)

    def fn(i, x0_t, x1_t, x2_t, x3_t, ba_t, cw_t, pa_t, pd_t):
        tap = lax.broadcasted_iota(jnp.int32, cw_t.shape, 0)
        c_t = sum(x_t * jnp.sum(jnp.where(tap == j, cw_t, 0.0), axis=0, keepdims=True)
                  for j, x_t in enumerate((x0_t, x1_t, x2_t, x3_t)))
        act = _silu(c_t)
        q = jnp.concatenate([unit(x) for x in _heads(act[:, :w], DN_DK)], axis=1)
        k = jnp.concatenate([unit(x) for x in _heads(act[:, w:2 * w], DN_DK)], axis=1)
        beta = _sigmoid(ba_t)
        g = -jnp.exp(pa_t) * _softplus(ba_t + pd_t)
        src = lax.broadcasted_iota(jnp.int32, (LANES, w), 0)
        head = lax.broadcasted_iota(jnp.int32, (LANES, w), 1) >> int(np.log2(DN_DK))
        be = _hdot_nn(beta, (src == head).astype(F32))
        ge = _hdot_nn(g, (src == head + DN_HEADS).astype(F32))
        return q, k, act[:, 2 * w:], be, ge

    op = _tiled_op(name, fn, steps,
                   [_rows(xw, t)] * DN_CONV + [_rows(LANES, t), _whole((8, xw)), _whole((1, LANES)), _whole((1, LANES))],
                   ["row"] * (DN_CONV + 1) + ["param"] * 3, [((s, w),) + _rows(w, t)] * 5)
    return op(*taps, ba, cw, pa, pd)


def _dn_post(o, z, norm_w, name):
    s, w = o.shape
    t = _tile(s, 512, 8)

    def fn(i, o_t, z_t, nw_t):
        outs = [x * lax.rsqrt(jnp.mean(x * x, axis=-1, keepdims=True) + RMS_EPS) * nw_t * _silu(zz)
                for x, zz in zip(_heads(o_t, DN_DV), _heads(z_t, DN_DV))]
        return (jnp.concatenate(outs, axis=1),)

    op = _tiled_op(name, fn, s // t, [_rows(w, t), _rows(w, t), _whole((1, DN_DV))], ["row", "row", "param"],
                   [((s, w),) + _rows(w, t)])
    return op(o, z, norm_w.reshape(1, DN_DV))[0]


def _xattn(q, k_mem, v_mem, name):
    s, w = q.shape
    t = _tile(s, 512, 8)

    def fn(i, q_t, k_t, v_t):
        outs = []
        for qh, kh, vh in zip(_heads(q_t, XA_DH), _heads(k_t, XA_DH), _heads(v_t, XA_DH)):
            sc = _bdot_nt(qh, kh) * (XA_DH ** -0.5)
            e = jnp.exp(sc - jnp.max(sc, axis=-1, keepdims=True))
            outs.append(_bdot_nn(e / jnp.sum(e, axis=-1, keepdims=True), vh))
        return (jnp.concatenate(outs, axis=1),)

    op = _tiled_op(name, fn, s // t, [_rows(w, t), _whole(k_mem.shape), _whole(v_mem.shape)],
                   ["row", "param", "param"], [((s, w),) + _rows(w, t)])
    return op(q, k_mem, v_mem)[0]


def _swa(q, kv, sinks, name):
    s, w = q.shape
    wd = WINDOW
    grp = SWA_HEADS // SWA_KV_HEADS
    dh = SWA_DH
    steps = s // wd
    sink_rows = jnp.broadcast_to(sinks[:, None], (SWA_HEADS, LANES))

    def fn(i, q_t, kc_t, kp_t, sk_t):
        qr = lax.broadcasted_iota(jnp.int32, (grp * wd, 2 * wd), 0) & (wd - 1)
        kc = lax.broadcasted_iota(jnp.int32, (grp * wd, 2 * wd), 1)
        dist = qr + wd - kc
        mask = (dist >= 0) & (dist < wd) & ((kc >= wd) | (i > 0))
        head_row = lax.broadcasted_iota(jnp.int32, (SWA_HEADS, LANES), 0)
        outs = [None] * SWA_HEADS
        for hk in range(SWA_KV_HEADS):
            kh = jnp.concatenate([kp_t[:, hk * dh:(hk + 1) * dh], kc_t[:, hk * dh:(hk + 1) * dh]], axis=0)
            vh = jnp.concatenate([kp_t[:, (SWA_KV_HEADS + hk) * dh:(SWA_KV_HEADS + hk + 1) * dh],
                                  kc_t[:, (SWA_KV_HEADS + hk) * dh:(SWA_KV_HEADS + hk + 1) * dh]], axis=0)
            qg = jnp.concatenate([q_t[:, (hk * grp + g) * dh:(hk * grp + g + 1) * dh] for g in range(grp)], axis=0)
            sink = jnp.concatenate(
                [jnp.broadcast_to(jnp.sum(jnp.where(head_row == hk * grp + g, sk_t, 0.0), axis=0, keepdims=True),
                                  (wd, LANES)) for g in range(grp)], axis=0)
            sink = jnp.max(sink, axis=-1, keepdims=True)
            sc = jnp.where(mask, _bdot_nt(qg, kh) * (dh ** -0.5), NEG_INF)
            m = jnp.maximum(jnp.max(sc, axis=-1, keepdims=True), sink)
            p = jnp.exp(sc - m)
            p = p / (jnp.sum(p, axis=-1, keepdims=True) + jnp.exp(sink - m))
            o = _bdot_nn(p, vh)
            for g in range(grp):
                outs[hk * grp + g] = o[g * wd:(g + 1) * wd]
        return (jnp.concatenate(outs, axis=1),)

    kvw = kv.shape[1]
    prev_block = ((wd, kvw), lambda i: (jnp.maximum(i - 1, 0), 0))
    shift = lambda g: jnp.concatenate([g[wd:], jnp.zeros((wd, kvw), F32)], axis=0)
    op = _tiled_op(name, fn, steps, [_rows(w, wd), _rows(kvw, wd), prev_block, _whole((SWA_HEADS, LANES))],
                   ["row", "row", "row", "param"], [((s, w),) + _rows(w, wd)],
                   grad_blocks={2: (kv.shape,) + _rows(kvw, wd)}, grad_fix={2: shift})
    return op(q, kv, kv, sink_rows)[0]


def _dn_prepare(q, k, v, be, ge):
    b, c, dk = q.shape
    row = lax.broadcasted_iota(jnp.int32, (c, c), 0)
    col = lax.broadcasted_iota(jnp.int32, (c, c), 1)
    tril = row >= col
    gc = _hdot_nn(jnp.broadcast_to(tril.astype(F32), (b, c, c)), ge)
    g_col = _hdot_nn(gc, jnp.full((b, LANES, c), 1.0 / LANES, F32))
    g_row = _hdot_nt(jnp.full((b, c, LANES), 1.0 / LANES, F32), gc)
    decay = jnp.where(tril, jnp.exp(jnp.where(tril, g_col - g_row, 0.0)), 0.0)
    kb = k * be
    a = jnp.where(row > col, _bdot_nt(kb, k) * decay, 0.0)
    x = jnp.broadcast_to((row == col).astype(F32), (b, c, c))
    for lvl in range(int(np.log2(c))):
        off = ((row >> lvl) == (col >> lvl) + 1) & ((row >> (lvl + 1)) == (col >> (lvl + 1)))
        x = x - _hdot_nn(x, _hdot_nn(jnp.where(off, a, 0.0), x))
    eg = jnp.exp(gc)
    qs = q * (dk ** -0.5)
    qk = jnp.where(tril, _bdot_nt(qs, k) * decay, 0.0)
    gl_c = _hdot_nn(jnp.ones((b, c, c), F32), ge)
    gl_s = _hdot_nn(jnp.ones((b, dk, c), F32), ge)
    return _hdot_nn(x, v * be), _hdot_nn(x, kb * eg), qs * eg, k * jnp.exp(gl_c - gc), qk, jnp.exp(gl_s)


def _dn_stack(ref, chunks, heads):
    return jnp.stack([ref[ci * DN_CHUNK:(ci + 1) * DN_CHUNK, h * DN_DK:(h + 1) * DN_DK]
                      for ci in range(chunks) for h in range(heads)])


def _dn_step(u, w, q_dec, k_tail, qk, decay, s):
    v_new = u - _bdot_nn(w, s)
    o = _bdot_nn(q_dec, s) + _bdot_nn(qk, v_new)
    return o, s * decay + _bdot_tn(k_tail, v_new)


def _hdot(a, b, dims=NN):
    return lax.dot_general(a, b, dims, precision=HI, preferred_element_type=F32)


def _dn_chunk(q, k, v, be, ge, s):
    c = q.shape[0]
    row = lax.broadcasted_iota(jnp.int32, (c, c), 0)
    col = lax.broadcasted_iota(jnp.int32, (c, c), 1)
    tril = row >= col
    gc = _hdot(tril.astype(F32), ge)
    g_row = _hdot(jnp.full((c, LANES), 1.0 / LANES, F32), gc, NT)
    decay = jnp.where(tril, jnp.exp(jnp.where(tril, gc[:, :c] - g_row, 0.0)), 0.0)
    kb = k * be
    a = jnp.where(row > col, _bdot_nt(kb, k) * decay, 0.0)
    x = (row == col).astype(F32)
    for lvl in range(int(np.log2(c))):
        off = ((row >> lvl) == (col >> lvl) + 1) & ((row >> (lvl + 1)) == (col >> (lvl + 1)))
        x = x - _hdot(x, _hdot(jnp.where(off, a, 0.0), x))
    eg = jnp.exp(gc)
    u = _hdot(x, v * be)
    w = _hdot(x, kb * eg)
    qs = q * (q.shape[1] ** -0.5)
    qk = jnp.where(tril, _bdot_nt(qs, k) * decay, 0.0)
    gl = jnp.sum(ge, axis=0, keepdims=True)
    v_new = u - _bdot_nn(w, s)
    o = _bdot_nn(qs * eg, s) + _bdot_nn(qk, v_new)
    s_next = s * jnp.exp(gl) + _bdot_tn(k * jnp.exp(gl - gc), v_new)
    return o, s_next


def _dn_forward(q, k, v, be, ge, name, chunks_per_step=4):
    s_len, width = q.shape
    heads = width // DN_DK
    n = s_len // DN_CHUNK
    cb = min(chunks_per_step, n)
    rows = cb * DN_CHUNK

    def body(q_ref, k_ref, v_ref, be_ref, ge_ref, o_ref, st_ref, s_scr):
        @pl.when(pl.program_id(0) == 0)
        def _():
            s_scr[...] = jnp.zeros_like(s_scr)

        parts = _dn_prepare(*[_dn_stack(ref, cb, heads) for ref in (q_ref, k_ref, v_ref, be_ref, ge_ref)])
        for ci in range(cb):
            s = s_scr[...]
            o, s_next = _dn_step(*[t[ci * heads:(ci + 1) * heads] for t in parts], s)
            for h in range(heads):
                st_ref[ci, h] = s[h]
                o_ref[ci * DN_CHUNK:(ci + 1) * DN_CHUNK, h * DN_DK:(h + 1) * DN_DK] = o[h]
            s_scr[...] = s_next

    tok = pl.BlockSpec((rows, width), lambda i: (i, 0))
    return pl.pallas_call(
        body, name=name + "_fwd", grid=(n // cb,), in_specs=[tok] * 5,
        out_specs=[tok, pl.BlockSpec((cb, heads, DN_DK, DN_DV), lambda i: (i, 0, 0, 0))],
        out_shape=[jax.ShapeDtypeStruct((s_len, width), F32), jax.ShapeDtypeStruct((n, heads, DN_DK, DN_DV), F32)],
        scratch_shapes=[pltpu.VMEM((heads, DN_DK, DN_DV), F32)],
        compiler_params=pltpu.CompilerParams(dimension_semantics=("arbitrary",), vmem_limit_bytes=V7X_VMEM_LIMIT_BYTES),
    )(q, k, v, be, ge)


def _dn_backward(q, k, v, be, ge, states, do, name, chunks_per_step=4):
    s_len, width = q.shape
    heads = width // DN_DK
    n = s_len // DN_CHUNK
    cb = min(chunks_per_step, n)
    rows = cb * DN_CHUNK
    steps = n // cb

    def body(q_ref, k_ref, v_ref, be_ref, ge_ref, st_ref, do_ref, dq_ref, dk_ref, dv_ref, dbe_ref, dge_ref, ds_scr):
        @pl.when(pl.program_id(0) == 0)
        def _():
            ds_scr[...] = jnp.zeros_like(ds_scr)

        for ci in reversed(range(cb)):
            r = slice(ci * DN_CHUNK, (ci + 1) * DN_CHUNK)
            for h in range(heads):
                l = slice(h * DN_DK, (h + 1) * DN_DK)
                _, vjp = jax.vjp(_dn_chunk, q_ref[r, l], k_ref[r, l], v_ref[r, l], be_ref[r, l], ge_ref[r, l],
                                 st_ref[ci, h])
                dq, dk, dv, dbe, dge, ds = vjp((do_ref[r, l], ds_scr[h]))
                dq_ref[r, l] = dq
                dk_ref[r, l] = dk
                dv_ref[r, l] = dv
                dbe_ref[r, l] = dbe
                dge_ref[r, l] = dge
                ds_scr[h] = ds

    tok = pl.BlockSpec((rows, width), lambda i: (steps - 1 - i, 0))
    return pl.pallas_call(
        body, name=name + "_bwd", grid=(steps,),
        in_specs=[tok] * 5 + [pl.BlockSpec((cb, heads, DN_DK, DN_DV), lambda i: (steps - 1 - i, 0, 0, 0)), tok],
        out_specs=[tok] * 5,
        out_shape=[jax.ShapeDtypeStruct((s_len, width), F32)] * 5,
        scratch_shapes=[pltpu.VMEM((heads, DN_DK, DN_DV), F32)],
        compiler_params=pltpu.CompilerParams(dimension_semantics=("arbitrary",), vmem_limit_bytes=V7X_VMEM_LIMIT_BYTES),
    )(q, k, v, be, ge, states, do)


@functools.partial(jax.custom_vjp, nondiff_argnums=(5,))
def _dn_core(q, k, v, be, ge, name):
    return _dn_forward(q, k, v, be, ge, name)[0]


def _dn_core_fwd(q, k, v, be, ge, name):
    o, states = _dn_forward(q, k, v, be, ge, name)
    return o, (q, k, v, be, ge, states)


def _dn_core_bwd(name, res, do):
    return tuple(_dn_backward(*res, do, name))


_dn_core.defvjp(_dn_core_fwd, _dn_core_bwd)


def _layer_norm(x, g, b):
    mu = x.mean(-1, keepdims=True)
    var = jnp.square(x - mu).mean(-1, keepdims=True)
    return (x - mu) * lax.rsqrt(var + LN_EPS) * g + b


def _swiglu(x, w_gu, w_down, tag):
    return pmm(_ffn_act(pmm(x, w_gu, tag + "_gu"), tag + "_act"), w_down, tag + "_down")


def _hybrid_layer(x, mem_n, p, l):
    tag = f"l{l}"
    ln_g, ln_b = p["ln_g"][l], p["ln_b"][l]
    h = _res_ln(x, _swiglu(x, p["ffn1_w_gu"][l], p["ffn1_w_down"][l], tag + "_ffn1"), ln_g[0], ln_b[0], 0.5,
                tag + "_ln0")
    proj = pmm(h, p["w_in"][l], tag + "_in")
    seg = {name: proj[:, lo:hi] for name, (lo, hi) in SEG.items()}
    q, k, v, be, ge = _dn_pre(seg["dn_qkv"], seg["dn_ba"], p["dn_conv_w"][l], p["dn_a_log"][l], p["dn_dt_bias"][l],
                              tag + "_dn_pre")
    o_dn = _dn_post(_dn_core(q, k, v, be, ge, tag + "_dn"), seg["dn_z"], p["dn_norm_w"][l], tag + "_dn_post")
    o_sw = _swa(seg["sw_q"], seg["sw_kv"], p["swa_sinks"][l], tag + "_swa")
    kv_mem = pmm(mem_n, p["w_mem_kv"][l], tag + "_memkv")
    half = XA_HEADS * XA_DH
    o_xa = _xattn(seg["xa_q"], kv_mem[:, :half], kv_mem[:, half:], tag + "_xattn")
    branch = [pmm(o, p["w_branch"][l, i], tag + f"_branch{i}") for i, o in enumerate((o_dn, o_sw, o_xa))]
    merged = _gate_merge(seg["gates"], *branch, tag + "_merge")
    h = _res_ln(h, pmm(merged, p["w_out"][l], tag + "_out"), ln_g[1], ln_b[1], 1.0, tag + "_ln1")
    return _res_ln(h, _swiglu(h, p["ffn2_w_gu"][l], p["ffn2_w_down"][l], tag + "_ffn2"), ln_g[2], ln_b[2], 0.5,
                   tag + "_ln2")


def _local_loss(p, x, mem, target):
    mem_n = _layer_norm(mem, p["mem_ln_g"], p["mem_ln_b"])
    y = x
    for l in range(DEPTH):
        y = _hybrid_layer(y, mem_n, p, l)
    return jnp.sum(_row_loss(y, target, "row_loss")[:, 0])


def _w_in_permute(w):
    pad = jnp.zeros(w.shape[:-1] + (D_IN_PAD - D_IN,), w.dtype)
    return jnp.concatenate([w[..., :1536], w[..., 1544:], w[..., 1536:1544], pad], axis=-1)


def _w_in_unpermute(w):
    return jnp.concatenate([w[..., :1536], w[..., 6400:6408], w[..., 1536:6400]], axis=-1)


def _packed_rows(shards):
    total = sum(int(np.prod(shards[n].shape)) for n in BIG)
    rows = -(-total // PACK_W)
    return -(-rows // 8) * 8, total


def _pack_shards(shards, dtype):
    rows, total = _packed_rows(shards)
    flat = jnp.concatenate([shards[n].reshape(-1).astype(dtype) for n in BIG]
                           + [jnp.zeros((rows * PACK_W - total,), dtype)])
    return flat.reshape(rows, PACK_W)


def _unpack_gathered(gathered, shards):
    flat = gathered.reshape(N_DEV, -1)
    out, off = {}, 0
    for n in BIG:
        shp, ax = shards[n].shape, BIG_AXIS[n]
        size = int(np.prod(shp))
        t = flat[:, off:off + size].reshape((N_DEV,) + shp).astype(F32)
        t = jnp.moveaxis(t, 0, ax)
        out[n] = t.reshape(shp[:ax] + (N_DEV * shp[ax],) + shp[ax + 1:])
        off += size
    return out


def _pack_full_grads(grads, shards):
    rows, total = _packed_rows(shards)
    parts = []
    for n in BIG:
        shp, ax = shards[n].shape, BIG_AXIS[n]
        t = grads[n].reshape(shp[:ax] + (N_DEV, shp[ax]) + shp[ax + 1:])
        parts.append(jnp.moveaxis(t, ax, 0).reshape(N_DEV, -1))
    parts.append(jnp.zeros((N_DEV, rows * PACK_W - total), F32))
    return jnp.concatenate(parts, axis=1).reshape(N_DEV, rows, PACK_W)


def _unpack_shards(packed, shards):
    flat = packed.reshape(-1)
    out, off = {}, 0
    for n in BIG:
        size = int(np.prod(shards[n].shape))
        out[n] = flat[off:off + size].reshape(shards[n].shape)
        off += size
    return out


def _pack_small(parts):
    flat = jnp.concatenate([t.reshape(-1).astype(F32) for t in parts])
    rows = -(-flat.shape[0] // LANES)
    rows = -(-rows // 8) * 8
    return jnp.concatenate([flat, jnp.zeros((rows * LANES - flat.shape[0],), F32)]).reshape(rows, LANES)


def _unpack_small(packed, shapes):
    flat = packed.reshape(packed.shape[:-2] + (-1,))
    out, off = [], 0
    for shp in shapes:
        size = int(np.prod(shp))
        out.append(flat[..., off:off + size].reshape(flat.shape[:-1] + tuple(shp)))
        off += size
    return out


def kernel(x, mem, mem_ln_g, mem_ln_b, ln_g, ln_b, ffn1_w_gu, ffn1_w_down, w_in, dn_conv_w, dn_a_log, dn_dt_bias, dn_norm_w, swa_sinks, w_mem_kv, w_branch, w_out, ffn2_w_gu, ffn2_w_down, loss_target, m_mem_ln_g, m_mem_ln_b, m_ln_g, m_ln_b, m_ffn1_w_gu, m_ffn1_w_down, m_w_in, m_dn_conv_w, m_dn_a_log, m_dn_dt_bias, m_dn_norm_w, m_swa_sinks, m_w_mem_kv, m_w_branch, m_w_out, m_ffn2_w_gu, m_ffn2_w_down, v_mem_ln_g, v_mem_ln_b, v_ln_g, v_ln_b, v_ffn1_w_gu, v_ffn1_w_down, v_w_in, v_dn_conv_w, v_dn_a_log, v_dn_dt_bias, v_dn_norm_w, v_swa_sinks, v_w_mem_kv, v_w_branch, v_w_out, v_ffn2_w_gu, v_ffn2_w_down):
    given = dict(locals())
    w_loc = {n: given[n] for n in WEIGHTS}
    m_loc = {n: given["m_" + n] for n in WEIGHTS}
    v_loc = {n: given["v_" + n] for n in WEIGHTS}
    me = _my_index()

    big_shards = {n: w_loc[n] for n in BIG}
    gathered = _all_gather_packed(_pack_shards(big_shards, BF16))
    params = _unpack_gathered(gathered, big_shards)
    params["w_in"] = _w_in_permute(params["w_in"])
    small_shard_shapes = [w_loc[n].shape for n in SMALL_SHARDED]
    small_gathered = _small_all_gather(_pack_small([w_loc[n] for n in SMALL_SHARDED]), False, "all_gather_small")
    for n, t in zip(SMALL_SHARDED, _unpack_small(small_gathered, small_shard_shapes)):
        t = jnp.moveaxis(t, 0, -2)
        params[n] = t.reshape(t.shape[:-2] + (-1,))
    for n in SMALL_REPL:
        params[n] = w_loc[n]

    loss_local, (g_params, grad_x) = jax.value_and_grad(_local_loss, argnums=(0, 1))(
        params, x[0], mem[0], loss_target[0])
    grad_x = grad_x[None]
    g_params["w_in"] = _w_in_unpermute(g_params["w_in"])

    g_big = _unpack_shards(_sum_slots(_exchange_slots(_pack_full_grads(g_params, big_shards)), 232), big_shards)
    small_names = SMALL_SHARDED + SMALL_REPL
    small_full_shapes = [g_params[n].shape for n in small_names] + [()]
    small_sum = _small_all_gather(_pack_small([g_params[n] for n in small_names] + [loss_local]), True, "all_sum_small")
    small_full = dict(zip(small_names + ("loss",), _unpack_small(small_sum, small_full_shapes)))
    loss = small_full["loss"]
    grads = dict(g_big)
    for n in SMALL_SHARDED:
        shard = w_loc[n].shape[-1]
        grads[n] = lax.dynamic_slice_in_dim(small_full[n], me * shard, shard, axis=small_full[n].ndim - 1)
    for n in SMALL_REPL:
        grads[n] = small_full[n]

    delta, new_m, new_v = {}, {}, {}
    for n in BIG:
        delta[n], new_m[n], new_v[n] = _adamw(w_loc[n], grads[n], m_loc[n], v_loc[n], "adamw_" + n)
    shapes = [w_loc[n].shape for n in small_names]
    packed = [_pack_small([src[n] for n in small_names]) for src in (w_loc, grads, m_loc, v_loc)]
    for dst, t in zip((delta, new_m, new_v), _adamw(*packed, "adamw_small")):
        dst.update(zip(small_names, _unpack_small(t, shapes)))

    return (loss, grad_x, *[grads[n] for n in WEIGHTS], *[delta[n] for n in WEIGHTS],
            *[new_m[n] for n in WEIGHTS], *[new_v[n] for n in WEIGHTS])
```

```python
import functools

import numpy as np
import jax
import jax.numpy as jnp
from jax import lax
from jax.experimental import pallas as pl
from jax.experimental.pallas import tpu as pltpu

F32 = jnp.float32
BF16 = jnp.bfloat16
MESH = pl.DeviceIdType.MESH
N_DEV = 8
V7X_VMEM_LIMIT_BYTES = 56 * 1024 * 1024
LANES = 128
PACK_W = 1024

D_MODEL = 1024
DEPTH = 2
DN_HEADS, DN_DK, DN_DV, DN_CONV, DN_CHUNK = 4, 128, 128, 4, 64
SWA_HEADS, SWA_KV_HEADS, SWA_DH, WINDOW = 8, 2, 64, 128
XA_HEADS, XA_DH = 4, 128
D_FF = 2816
N_BRANCH, BRANCH_W = 3, 512
DEEPNORM_ALPHA = (2 * DEPTH) ** 0.25
LN_EPS = 1e-5
RMS_EPS = 1e-6
NEG_INF = -1e30
D_IN = 6408
D_IN_PAD = 6528
SEG = dict(dn_qkv=(0, 1536), dn_z=(1536, 2048), sw_q=(2048, 2560), sw_kv=(2560, 2816), xa_q=(2816, 3328),
           gates=(3328, 6400), dn_ba=(6400, 6528))

ADAM_LR, ADAM_B1, ADAM_B2, ADAM_EPS, ADAM_WD, ADAM_STEP = 0.001, 0.9, 0.999, 1e-08, 0.01, 10

BIG = ("ffn1_w_gu", "ffn1_w_down", "w_in", "w_mem_kv", "w_branch", "w_out", "ffn2_w_gu", "ffn2_w_down")
BIG_AXIS = dict(ffn1_w_gu=2, ffn1_w_down=1, w_in=2, w_mem_kv=1, w_branch=3, w_out=1, ffn2_w_gu=2, ffn2_w_down=1)
SMALL_SHARDED = ("ln_g", "ln_b", "dn_conv_w")
SMALL_REPL = ("mem_ln_g", "mem_ln_b", "dn_a_log", "dn_dt_bias", "dn_norm_w", "swa_sinks")
WEIGHTS = ("mem_ln_g", "mem_ln_b", "ln_g", "ln_b", "ffn1_w_gu", "ffn1_w_down", "w_in", "dn_conv_w", "dn_a_log",
           "dn_dt_bias", "dn_norm_w", "swa_sinks", "w_mem_kv", "w_branch", "w_out", "ffn2_w_gu", "ffn2_w_down")

HI = lax.Precision.HIGHEST
NN = (((1,), (0,)), ((), ()))
NT = (((1,), (1,)), ((), ()))
TN = (((0,), (0,)), ((), ()))


def _tile(n, target, align):
    best = None
    for d in range(align, min(n, target) + 1, align):
        if n % d == 0:
            best = d
    return n if best is None else best


def _my_index():
    return 4 * lax.axis_index("x") + 2 * lax.axis_index("y") + lax.axis_index("c")


def _peer(k):
    x, y, c = lax.axis_index("x"), lax.axis_index("y"), lax.axis_index("c")
    px = 1 - x if (k >> 2) & 1 else x
    py = 1 - y if (k >> 1) & 1 else y
    pc = 1 - c if k & 1 else c
    return (px, py, pc), 4 * px + 2 * py + pc


def _matmul(a, b, *, ta=False, tb=False, name, tm=512, tn=2176, tk=2048):
    m, k = (a.shape[1], a.shape[0]) if ta else a.shape
    k2, n = (b.shape[1], b.shape[0]) if tb else b.shape
    assert k == k2, (a.shape, b.shape, ta, tb)
    tm = _tile(m, tm, LANES if ta else 8)
    tn = _tile(n, tn, LANES)
    tk = _tile(k, tk, LANES)
    nk = k // tk
    dims = (((0 if ta else 1,), (1 if tb else 0,)), ((), ()))

    def body(a_ref, b_ref, o_ref):
        kk = pl.program_id(2)
        p = lax.dot_general(a_ref[...].astype(BF16), b_ref[...].astype(BF16), dims, preferred_element_type=F32)

        @pl.when(kk == 0)
        def _():
            o_ref[...] = p

        @pl.when(kk != 0)
        def _():
            o_ref[...] += p

    a_spec = pl.BlockSpec((tk, tm), lambda j, i, kk: (kk, i)) if ta else pl.BlockSpec((tm, tk), lambda j, i, kk: (i, kk))
    b_spec = pl.BlockSpec((tn, tk), lambda j, i, kk: (j, kk)) if tb else pl.BlockSpec((tk, tn), lambda j, i, kk: (kk, j))
    return pl.pallas_call(
        body, name=name, grid=(n // tn, m // tm, nk), in_specs=[a_spec, b_spec],
        out_specs=pl.BlockSpec((tm, tn), lambda j, i, kk: (i, j)),
        out_shape=jax.ShapeDtypeStruct((m, n), F32),
        compiler_params=pltpu.CompilerParams(dimension_semantics=("parallel", "parallel", "arbitrary"),
                                             vmem_limit_bytes=V7X_VMEM_LIMIT_BYTES),
    )(a, b)


@functools.partial(jax.custom_vjp, nondiff_argnums=(2,))
def pmm(x, w, tag):
    return _matmul(x, w, name=tag + "_fwd")


def _pmm_fwd(x, w, tag):
    return _matmul(x, w, name=tag + "_fwd"), (x, w)


def _pmm_bwd(tag, res, dy):
    x, w = res
    dx = _matmul(dy, w, tb=True, name=tag + "_dx")
    dw = _matmul(x, dy, ta=True, name=tag + "_dw", tm=1408, tn=2176, tk=512)
    return dx, dw


pmm.defvjp(_pmm_fwd, _pmm_bwd)


def _all_gather(block, name):
    def body(x_ref, out_ref, send_sems, recv_sems, local_sem):
        me = _my_index()
        mine = pltpu.make_async_copy(x_ref, out_ref.at[me], local_sem)
        mine.start()
        sends = []
        for k in range(1, N_DEV):
            peer, _ = _peer(k)
            cp = pltpu.make_async_remote_copy(src_ref=x_ref, dst_ref=out_ref.at[me], send_sem=send_sems.at[k - 1],
                                              recv_sem=recv_sems.at[k - 1], device_id=peer, device_id_type=MESH)
            cp.start()
            sends.append(cp)
        for k in range(1, N_DEV):
            peer, p = _peer(k)
            pltpu.make_async_remote_copy(src_ref=x_ref, dst_ref=out_ref.at[p], send_sem=send_sems.at[k - 1],
                                         recv_sem=recv_sems.at[k - 1], device_id=peer, device_id_type=MESH).wait_recv()
        for cp in sends:
            cp.wait_send()
        mine.wait()

    return pl.pallas_call(
        body, name=name,
        out_shape=jax.ShapeDtypeStruct((N_DEV,) + block.shape, block.dtype),
        in_specs=[pl.BlockSpec(memory_space=pl.ANY)], out_specs=pl.BlockSpec(memory_space=pl.ANY),
        scratch_shapes=[pltpu.SemaphoreType.DMA((N_DEV - 1,)), pltpu.SemaphoreType.DMA((N_DEV - 1,)),
                        pltpu.SemaphoreType.DMA],
    )(block)


def _exchange_slots(slots, name):
    def body(x_ref, out_ref, send_sems, recv_sems, local_sem):
        me = _my_index()
        mine = pltpu.make_async_copy(x_ref.at[me], out_ref.at[me], local_sem)
        mine.start()
        sends = []
        for k in range(1, N_DEV):
            peer, p = _peer(k)
            cp = pltpu.make_async_remote_copy(src_ref=x_ref.at[p], dst_ref=out_ref.at[me], send_sem=send_sems.at[k - 1],
                                              recv_sem=recv_sems.at[k - 1], device_id=peer, device_id_type=MESH)
            cp.start()
            sends.append(cp)
        for k in range(1, N_DEV):
            peer, p = _peer(k)
            pltpu.make_async_remote_copy(src_ref=x_ref.at[p], dst_ref=out_ref.at[p], send_sem=send_sems.at[k - 1],
                                         recv_sem=recv_sems.at[k - 1], device_id=peer, device_id_type=MESH).wait_recv()
        for cp in sends:
            cp.wait_send()
        mine.wait()

    return pl.pallas_call(
        body, name=name,
        out_shape=jax.ShapeDtypeStruct(slots.shape, slots.dtype),
        in_specs=[pl.BlockSpec(memory_space=pl.ANY)], out_specs=pl.BlockSpec(memory_space=pl.ANY),
        scratch_shapes=[pltpu.SemaphoreType.DMA((N_DEV - 1,)), pltpu.SemaphoreType.DMA((N_DEV - 1,)),
                        pltpu.SemaphoreType.DMA],
    )(slots)


def _sum_slots(slots, name):
    shape = slots.shape[1:]
    w = shape[-1]
    slots = slots.reshape(N_DEV, -1, w)
    r = slots.shape[1]
    tr = _tile(r, 256, 8)

    def body(x_ref, o_ref):
        acc = x_ref[0].astype(F32)
        for p in range(1, N_DEV):
            acc = acc + x_ref[p].astype(F32)
        o_ref[...] = acc

    return pl.pallas_call(
        body, name=name, grid=(r // tr,),
        in_specs=[pl.BlockSpec((N_DEV, tr, w), lambda i: (0, i, 0))],
        out_specs=pl.BlockSpec((tr, w), lambda i: (i, 0)),
        out_shape=jax.ShapeDtypeStruct((r, w), F32),
        compiler_params=pltpu.CompilerParams(dimension_semantics=("parallel",), vmem_limit_bytes=V7X_VMEM_LIMIT_BYTES),
    )(slots).reshape(shape)


def _small_all_gather(v, reduce, name):
    rows, w = v.shape

    def body(x_ref, out_ref, land_ref, send_sems, recv_sems):
        me = _my_index()
        sends = []
        for k in range(1, N_DEV):
            peer, _ = _peer(k)
            cp = pltpu.make_async_remote_copy(src_ref=x_ref, dst_ref=land_ref.at[me], send_sem=send_sems.at[k - 1],
                                              recv_sem=recv_sems.at[k - 1], device_id=peer, device_id_type=MESH)
            cp.start()
            sends.append(cp)
        land_ref[me] = x_ref[...]
        for k in range(1, N_DEV):
            peer, p = _peer(k)
            pltpu.make_async_remote_copy(src_ref=x_ref, dst_ref=land_ref.at[p], send_sem=send_sems.at[k - 1],
                                         recv_sem=recv_sems.at[k - 1], device_id=peer, device_id_type=MESH).wait_recv()
        for cp in sends:
            cp.wait_send()
        if reduce:
            acc = land_ref[0]
            for p in range(1, N_DEV):
                acc = acc + land_ref[p]
            out_ref[...] = acc
        else:
            out_ref[...] = land_ref[...]

    out_shape = (rows, w) if reduce else (N_DEV, rows, w)
    return pl.pallas_call(
        body, name=name,
        out_shape=jax.ShapeDtypeStruct(out_shape, F32),
        in_specs=[pl.BlockSpec(memory_space=pltpu.VMEM)], out_specs=pl.BlockSpec(memory_space=pltpu.VMEM),
        scratch_shapes=[pltpu.VMEM((N_DEV, rows, w), F32), pltpu.SemaphoreType.DMA((N_DEV - 1,)),
                        pltpu.SemaphoreType.DMA((N_DEV - 1,))],
    )(v)


def _adamw(w, g, m, v, name):
    shape = w.shape
    cols = shape[-1] if len(shape) > 1 else shape[0]
    w2, g2, m2, v2 = (t.reshape(-1, cols) for t in (w, g, m, v))
    rows = w2.shape[0]
    tr = _tile(rows, 256, 8)

    def body(w_ref, g_ref, m_ref, v_ref, d_ref, mo_ref, vo_ref):
        gv = g_ref[...]
        mn = ADAM_B1 * m_ref[...] + (1.0 - ADAM_B1) * gv
        vn = ADAM_B2 * v_ref[...] + (1.0 - ADAM_B2) * (gv * gv)
        m_hat = mn / (1.0 - ADAM_B1 ** ADAM_STEP)
        v_hat = vn / (1.0 - ADAM_B2 ** ADAM_STEP)
        d_ref[...] = -ADAM_LR * (m_hat / (jnp.sqrt(v_hat) + ADAM_EPS) + ADAM_WD * w_ref[...])
        mo_ref[...] = mn
        vo_ref[...] = vn

    spec = pl.BlockSpec((tr, cols), lambda i: (i, 0))
    outs = pl.pallas_call(
        body, name=name, grid=(rows // tr,), in_specs=[spec] * 4, out_specs=[spec] * 3,
        out_shape=[jax.ShapeDtypeStruct((rows, cols), F32)] * 3,
        compiler_params=pltpu.CompilerParams(dimension_semantics=("parallel",), vmem_limit_bytes=V7X_VMEM_LIMIT_BYTES),
    )(w2, g2, m2, v2)
    return tuple(o.reshape(shape) for o in outs)


_DOT_DIMS = {2: dict(nn=NN, nt=NT, tn=TN),
             3: dict(nn=(((2,), (1,)), ((0,), (0,))), nt=(((2,), (2,)), ((0,), (0,))), tn=(((1,), (1,)), ((0,), (0,))))}


def _bf16_dot(a, b, kind):
    if a.ndim == 3:
        return jnp.stack([_bf16_dot(a[i], b[i] if b.ndim == 3 else b, kind) for i in range(a.shape[0])])
    return lax.dot_general(a.astype(BF16), b.astype(BF16), _DOT_DIMS[2][kind], preferred_element_type=F32)


def _f32_dot(a, b, kind):
    if a.ndim == 3:
        return jnp.stack([_f32_dot(a[i], b[i] if b.ndim == 3 else b, kind) for i in range(a.shape[0])])
    return lax.dot_general(a, b, _DOT_DIMS[2][kind], precision=HI, preferred_element_type=F32)


def _dot_family(raw):
    @jax.custom_vjp
    def nn(a, b):
        return raw(a, b, "nn")

    @jax.custom_vjp
    def nt(a, b):
        return raw(a, b, "nt")

    @jax.custom_vjp
    def tn(a, b):
        return raw(a, b, "tn")

    nn.defvjp(lambda a, b: (raw(a, b, "nn"), (a, b)), lambda r, ct: (nt(ct, r[1]), tn(r[0], ct)))
    nt.defvjp(lambda a, b: (raw(a, b, "nt"), (a, b)), lambda r, ct: (nn(ct, r[1]), tn(ct, r[0])))
    tn.defvjp(lambda a, b: (raw(a, b, "tn"), (a, b)), lambda r, ct: (nt(r[1], ct), nn(r[0], ct)))
    return nn, nt, tn


_bdot_nn, _bdot_nt, _bdot_tn = _dot_family(_bf16_dot)
_hdot_nn, _hdot_nt, _hdot_tn = _dot_family(_f32_dot)


def _rows(width, t):
    return (t, width), lambda i: (i, 0)


def _whole(shape):
    return tuple(shape), lambda i: (0,) * len(shape)


def _tiled_op(name, fn, steps, in_blocks, kinds, outs, grad_blocks=None, grad_fix=None):
    n_in, n_out = len(in_blocks), len(outs)
    diff = [j for j, kd in enumerate(kinds) if kd != "const"]
    has_param = any(kd == "param" for kd in kinds)
    in_specs = [pl.BlockSpec(b, m) for b, m in in_blocks]
    out_specs = [pl.BlockSpec(b, m) for _, b, m in outs]

    def forward(*arrays):
        def body(*refs):
            res = fn(pl.program_id(0), *[r[...] for r in refs[:n_in]])
            for o_ref, val in zip(refs[n_in:], res):
                o_ref[...] = val

        return pl.pallas_call(
            body, name=name + "_fwd", grid=(steps,), in_specs=in_specs, out_specs=out_specs,
            out_shape=[jax.ShapeDtypeStruct(s, F32) for s, _, _ in outs],
            compiler_params=pltpu.CompilerParams(dimension_semantics=("parallel",),
                                                 vmem_limit_bytes=V7X_VMEM_LIMIT_BYTES),
        )(*arrays)

    def backward(arrays, douts):
        g_specs = []
        for j in diff:
            if kinds[j] == "param":
                g_specs.append((arrays[j].shape,) + _whole(arrays[j].shape))
            else:
                g_specs.append((grad_blocks or {}).get(j, (arrays[j].shape,) + in_blocks[j]))

        def body(*refs):
            i = pl.program_id(0)
            vals = [r[...] for r in refs[:n_in]]
            dos = tuple(r[...] for r in refs[n_in:n_in + n_out])
            g_refs = refs[n_in + n_out:]

            def of_diff(*dv):
                full = list(vals)
                for j, val in zip(diff, dv):
                    full[j] = val
                return tuple(fn(i, *full))

            _, vjp = jax.vjp(of_diff, *[vals[j] for j in diff])
            grads = vjp(dos)
            for j, g_ref, g in zip(diff, g_refs, grads):
                if kinds[j] == "param":
                    @pl.when(i == 0)
                    def _(g_ref=g_ref, g=g):
                        g_ref[...] = g

                    @pl.when(i != 0)
                    def _(g_ref=g_ref, g=g):
                        g_ref[...] += g
                else:
                    g_ref[...] = g

        return pl.pallas_call(
            body, name=name + "_bwd", grid=(steps,), in_specs=in_specs + out_specs,
            out_specs=[pl.BlockSpec(b, m) for _, b, m in g_specs],
            out_shape=[jax.ShapeDtypeStruct(shp, F32) for shp, _, _ in g_specs],
            compiler_params=pltpu.CompilerParams(dimension_semantics=("arbitrary" if has_param else "parallel",),
                                                 vmem_limit_bytes=V7X_VMEM_LIMIT_BYTES),
        )(*arrays, *douts)

    @jax.custom_vjp
    def op(*arrays):
        return tuple(forward(*arrays))

    def op_fwd(*arrays):
        return tuple(forward(*arrays)), arrays

    def op_bwd(arrays, douts):
        grads = backward(arrays, douts)
        full = [jnp.zeros_like(a) for a in arrays]
        for j, g in zip(diff, grads):
            full[j] = grad_fix[j](g) if grad_fix and j in grad_fix else g
        return tuple(full)

    op.defvjp(op_fwd, op_bwd)
    return op


def _sigmoid(x):
    return jax.nn.sigmoid(x)


def _silu(x):
    return x * jax.nn.sigmoid(x)


def _softplus(x):
    return jnp.maximum(x, 0.0) + jnp.log(1.0 + jnp.exp(-jnp.abs(x)))


def _heads(t, width):
    return [t[:, h * width:(h + 1) * width] for h in range(t.shape[1] // width)]


def _ffn_act(gu, name):
    s, two_f = gu.shape
    f = two_f // 2
    t = _tile(s, 128, 8)

    def fn(i, gu_t):
        return (_silu(gu_t[:, :f]) * gu_t[:, f:],)

    return _tiled_op(name, fn, s // t, [_rows(two_f, t)], ["row"], [((s, f),) + _rows(f, t)])(gu)[0]


def _res_ln(x, f, g, b, scale, name):
    s, d = x.shape
    t = _tile(s, 512, 8)

    def fn(i, x_t, f_t, g_t, b_t):
        r = DEEPNORM_ALPHA * x_t + scale * f_t
        mu = jnp.mean(r, axis=-1, keepdims=True)
        var = jnp.mean(jnp.square(r - mu), axis=-1, keepdims=True)
        return ((r - mu) * lax.rsqrt(var + LN_EPS) * g_t + b_t,)

    op = _tiled_op(name, fn, s // t, [_rows(d, t), _rows(d, t), _whole((1, d)), _whole((1, d))],
                   ["row", "row", "param", "param"], [((s, d),) + _rows(d, t)])
    return op(x, f, g.reshape(1, d), b.reshape(1, d))[0]


def _row_loss(y, target, name):
    s, d = y.shape
    t = _tile(s, 512, 8)

    def fn(i, y_t, t_t):
        e = y_t - t_t
        return (jnp.broadcast_to(0.5 * jnp.mean(e * e, axis=-1, keepdims=True), (t, LANES)),)

    return _tiled_op(name, fn, s // t, [_rows(d, t), _rows(d, t)], ["row", "const"],
                     [((s, LANES),) + _rows(LANES, t)])(y, target)[0]


def _gate_merge(gates, b0, b1, b2, name):
    s, d = b0.shape
    t = _tile(s, 256, 8)

    def fn(i, g_t, b0_t, b1_t, b2_t):
        return (_sigmoid(g_t[:, :d]) * b0_t + _sigmoid(g_t[:, d:2 * d]) * b1_t + _sigmoid(g_t[:, 2 * d:]) * b2_t,)

    op = _tiled_op(name, fn, s // t, [_rows(3 * d, t)] + [_rows(d, t)] * 3, ["row"] * 4, [((s, d),) + _rows(d, t)])
    return op(gates, b0, b1, b2)[0]


def _dn_pre(x, ba, conv_w, a_log, dt_bias, name):
    s, xw = x.shape
    w = DN_HEADS * DN_DK
    t = _tile(s, 256, 8)
    steps = s // t
    lane_pad = jnp.zeros((LANES - 2 * DN_HEADS,), F32)
    pa = jnp.concatenate([jnp.zeros((DN_HEADS,), F32), a_log, lane_pad]).reshape(1, LANES)
    pd = jnp.concatenate([jnp.zeros((DN_HEADS,), F32), dt_bias, lane_pad]).reshape(1, LANES)
    cw = jnp.concatenate([conv_w, jnp.zeros((8 - DN_CONV, xw), F32)], axis=0)
    taps = [jnp.pad(x, ((DN_CONV - 1 - j, 0), (0, 0)))[:s] for j in range(DN_CONV - 1)] + [x]

    def unit(x):
        return x * lax.rsqrt(jnp.sum(x * x, axis=-1, keepdims=True) + RMS_EPS)

    def fn(i, x0_t, x1_t, x2_t, x3_t, ba_t, cw_t, pa_t, pd_t):
        tap = lax.broadcasted_iota(jnp.int32, cw_t.shape, 0)
        c_t = sum(x_t * jnp.sum(jnp.where(tap == j, cw_t, 0.0), axis=0, keepdims=True)
                  for j, x_t in enumerate((x0_t, x1_t, x2_t, x3_t)))
        act = _silu(c_t)
        q = jnp.concatenate([unit(x) for x in _heads(act[:, :w], DN_DK)], axis=1)
        k = jnp.concatenate([unit(x) for x in _heads(act[:, w:2 * w], DN_DK)], axis=1)
        beta = _sigmoid(ba_t)
        g = -jnp.exp(pa_t) * _softplus(ba_t + pd_t)
        src = lax.broadcasted_iota(jnp.int32, (LANES, w), 0)
        head = lax.broadcasted_iota(jnp.int32, (LANES, w), 1) >> int(np.log2(DN_DK))
        be = _hdot_nn(beta, (src == head).astype(F32))
        ge = _hdot_nn(g, (src == head + DN_HEADS).astype(F32))
        return q, k, act[:, 2 * w:], be, ge

    op = _tiled_op(name, fn, steps,
                   [_rows(xw, t)] * DN_CONV + [_rows(LANES, t), _whole((8, xw)), _whole((1, LANES)), _whole((1, LANES))],
                   ["row"] * (DN_CONV + 1) + ["param"] * 3, [((s, w),) + _rows(w, t)] * 5)
    return op(*taps, ba, cw, pa, pd)


def _dn_post(o, z, norm_w, name):
    s, w = o.shape
    t = _tile(s, 512, 8)

    def fn(i, o_t, z_t, nw_t):
        outs = [x * lax.rsqrt(jnp.mean(x * x, axis=-1, keepdims=True) + RMS_EPS) * nw_t * _silu(zz)
                for x, zz in zip(_heads(o_t, DN_DV), _heads(z_t, DN_DV))]
        return (jnp.concatenate(outs, axis=1),)

    op = _tiled_op(name, fn, s // t, [_rows(w, t), _rows(w, t), _whole((1, DN_DV))], ["row", "row", "param"],
                   [((s, w),) + _rows(w, t)])
    return op(o, z, norm_w.reshape(1, DN_DV))[0]


def _xattn(q, k_mem, v_mem, name):
    s, w = q.shape
    t = _tile(s, 512, 8)

    def fn(i, q_t, k_t, v_t):
        outs = []
        for qh, kh, vh in zip(_heads(q_t, XA_DH), _heads(k_t, XA_DH), _heads(v_t, XA_DH)):
            sc = _bdot_nt(qh, kh) * (XA_DH ** -0.5)
            e = jnp.exp(sc - jnp.max(sc, axis=-1, keepdims=True))
            outs.append(_bdot_nn(e / jnp.sum(e, axis=-1, keepdims=True), vh))
        return (jnp.concatenate(outs, axis=1),)

    op = _tiled_op(name, fn, s // t, [_rows(w, t), _whole(k_mem.shape), _whole(v_mem.shape)],
                   ["row", "param", "param"], [((s, w),) + _rows(w, t)])
    return op(q, k_mem, v_mem)[0]


def _swa(q, kv, sinks, name):
    s, w = q.shape
    wd = WINDOW
    grp = SWA_HEADS // SWA_KV_HEADS
    dh = SWA_DH
    steps = s // wd
    sink_rows = jnp.broadcast_to(sinks[:, None], (SWA_HEADS, LANES))

    def fn(i, q_t, kc_t, kp_t, sk_t):
        qr = lax.broadcasted_iota(jnp.int32, (grp * wd, 2 * wd), 0) & (wd - 1)
        kc = lax.broadcasted_iota(jnp.int32, (grp * wd, 2 * wd), 1)
        dist = qr + wd - kc
        mask = (dist >= 0) & (dist < wd) & ((kc >= wd) | (i > 0))
        head_row = lax.broadcasted_iota(jnp.int32, (SWA_HEADS, LANES), 0)
        outs = [None] * SWA_HEADS
        for hk in range(SWA_KV_HEADS):
            kh = jnp.concatenate([kp_t[:, hk * dh:(hk + 1) * dh], kc_t[:, hk * dh:(hk + 1) * dh]], axis=0)
            vh = jnp.concatenate([kp_t[:, (SWA_KV_HEADS + hk) * dh:(SWA_KV_HEADS + hk + 1) * dh],
                                  kc_t[:, (SWA_KV_HEADS + hk) * dh:(SWA_KV_HEADS + hk + 1) * dh]], axis=0)
            qg = jnp.concatenate([q_t[:, (hk * grp + g) * dh:(hk * grp + g + 1) * dh] for g in range(grp)], axis=0)
            sink = jnp.concatenate(
                [jnp.broadcast_to(jnp.sum(jnp.where(head_row == hk * grp + g, sk_t, 0.0), axis=0, keepdims=True),
                                  (wd, LANES)) for g in range(grp)], axis=0)
            sink = jnp.max(sink, axis=-1, keepdims=True)
            sc = jnp.where(mask, _bdot_nt(qg, kh) * (dh ** -0.5), NEG_INF)
            m = jnp.maximum(jnp.max(sc, axis=-1, keepdims=True), sink)
            p = jnp.exp(sc - m)
            p = p / (jnp.sum(p, axis=-1, keepdims=True) + jnp.exp(sink - m))
            o = _bdot_nn(p, vh)
            for g in range(grp):
                outs[hk * grp + g] = o[g * wd:(g + 1) * wd]
        return (jnp.concatenate(outs, axis=1),)

    kvw = kv.shape[1]
    prev_block = ((wd, kvw), lambda i: (jnp.maximum(i - 1, 0), 0))
    shift = lambda g: jnp.concatenate([g[wd:], jnp.zeros((wd, kvw), F32)], axis=0)
    op = _tiled_op(name, fn, steps, [_rows(w, wd), _rows(kvw, wd), prev_block, _whole((SWA_HEADS, LANES))],
                   ["row", "row", "row", "param"], [((s, w),) + _rows(w, wd)],
                   grad_blocks={2: (kv.shape,) + _rows(kvw, wd)}, grad_fix={2: shift})
    return op(q, kv, kv, sink_rows)[0]


def _dn_prepare(q, k, v, be, ge):
    b, c, dk = q.shape
    row = lax.broadcasted_iota(jnp.int32, (c, c), 0)
    col = lax.broadcasted_iota(jnp.int32, (c, c), 1)
    tril = row >= col
    gc = _hdot_nn(jnp.broadcast_to(tril.astype(F32), (b, c, c)), ge)
    g_col = _hdot_nn(gc, jnp.full((b, LANES, c), 1.0 / LANES, F32))
    g_row = _hdot_nt(jnp.full((b, c, LANES), 1.0 / LANES, F32), gc)
    decay = jnp.where(tril, jnp.exp(jnp.where(tril, g_col - g_row, 0.0)), 0.0)
    kb = k * be
    a = jnp.where(row > col, _bdot_nt(kb, k) * decay, 0.0)
    x = jnp.broadcast_to((row == col).astype(F32), (b, c, c))
    for lvl in range(int(np.log2(c))):
        off = ((row >> lvl) == (col >> lvl) + 1) & ((row >> (lvl + 1)) == (col >> (lvl + 1)))
        x = x - _hdot_nn(x, _hdot_nn(jnp.where(off, a, 0.0), x))
    eg = jnp.exp(gc)
    qs = q * (dk ** -0.5)
    qk = jnp.where(tril, _bdot_nt(qs, k) * decay, 0.0)
    gl_c = _hdot_nn(jnp.ones((b, c, c), F32), ge)
    gl_s = _hdot_nn(jnp.ones((b, dk, c), F32), ge)
    return _hdot_nn(x, v * be), _hdot_nn(x, kb * eg), qs * eg, k * jnp.exp(gl_c - gc), qk, jnp.exp(gl_s)


def _dn_stack(ref, chunks, heads):
    return jnp.stack([ref[ci * DN_CHUNK:(ci + 1) * DN_CHUNK, h * DN_DK:(h + 1) * DN_DK]
                      for ci in range(chunks) for h in range(heads)])


def _dn_step(u, w, q_dec, k_tail, qk, decay, s):
    v_new = u - _bdot_nn(w, s)
    o = _bdot_nn(q_dec, s) + _bdot_nn(qk, v_new)
    return o, s * decay + _bdot_tn(k_tail, v_new)


def _hdot(a, b, dims=NN):
    return lax.dot_general(a, b, dims, precision=HI, preferred_element_type=F32)


def _dn_chunk(q, k, v, be, ge, s):
    c = q.shape[0]
    row = lax.broadcasted_iota(jnp.int32, (c, c), 0)
    col = lax.broadcasted_iota(jnp.int32, (c, c), 1)
    tril = row >= col
    gc = _hdot(tril.astype(F32), ge)
    g_row = _hdot(jnp.full((c, LANES), 1.0 / LANES, F32), gc, NT)
    decay = jnp.where(tril, jnp.exp(jnp.where(tril, gc[:, :c] - g_row, 0.0)), 0.0)
    kb = k * be
    a = jnp.where(row > col, _bdot_nt(kb, k) * decay, 0.0)
    x = (row == col).astype(F32)
    for lvl in range(int(np.log2(c))):
        off = ((row >> lvl) == (col >> lvl) + 1) & ((row >> (lvl + 1)) == (col >> (lvl + 1)))
        x = x - _hdot(x, _hdot(jnp.where(off, a, 0.0), x))
    eg = jnp.exp(gc)
    u = _hdot(x, v * be)
    w = _hdot(x, kb * eg)
    qs = q * (q.shape[1] ** -0.5)
    qk = jnp.where(tril, _bdot_nt(qs, k) * decay, 0.0)
    gl = jnp.sum(ge, axis=0, keepdims=True)
    v_new = u - _bdot_nn(w, s)
    o = _bdot_nn(qs * eg, s) + _bdot_nn(qk, v_new)
    s_next = s * jnp.exp(gl) + _bdot_tn(k * jnp.exp(gl - gc), v_new)
    return o, s_next


def _dn_forward(q, k, v, be, ge, name, chunks_per_step=4):
    s_len, width = q.shape
    heads = width // DN_DK
    n = s_len // DN_CHUNK
    cb = min(chunks_per_step, n)
    rows = cb * DN_CHUNK

    def body(q_ref, k_ref, v_ref, be_ref, ge_ref, o_ref, st_ref, s_scr):
        @pl.when(pl.program_id(0) == 0)
        def _():
            s_scr[...] = jnp.zeros_like(s_scr)

        parts = _dn_prepare(*[_dn_stack(ref, cb, heads) for ref in (q_ref, k_ref, v_ref, be_ref, ge_ref)])
        for ci in range(cb):
            s = s_scr[...]
            o, s_next = _dn_step(*[t[ci * heads:(ci + 1) * heads] for t in parts], s)
            for h in range(heads):
                st_ref[ci, h] = s[h]
                o_ref[ci * DN_CHUNK:(ci + 1) * DN_CHUNK, h * DN_DK:(h + 1) * DN_DK] = o[h]
            s_scr[...] = s_next

    tok = pl.BlockSpec((rows, width), lambda i: (i, 0))
    return pl.pallas_call(
        body, name=name + "_fwd", grid=(n // cb,), in_specs=[tok] * 5,
        out_specs=[tok, pl.BlockSpec((cb, heads, DN_DK, DN_DV), lambda i: (i, 0, 0, 0))],
        out_shape=[jax.ShapeDtypeStruct((s_len, width), F32), jax.ShapeDtypeStruct((n, heads, DN_DK, DN_DV), F32)],
        scratch_shapes=[pltpu.VMEM((heads, DN_DK, DN_DV), F32)],
        compiler_params=pltpu.CompilerParams(dimension_semantics=("arbitrary",), vmem_limit_bytes=V7X_VMEM_LIMIT_BYTES),
    )(q, k, v, be, ge)


def _dn_backward(q, k, v, be, ge, states, do, name, chunks_per_step=4):
    s_len, width = q.shape
    heads = width // DN_DK
    n = s_len // DN_CHUNK
    cb = min(chunks_per_step, n)
    rows = cb * DN_CHUNK
    steps = n // cb

    def body(q_ref, k_ref, v_ref, be_ref, ge_ref, st_ref, do_ref, dq_ref, dk_ref, dv_ref, dbe_ref, dge_ref, ds_scr):
        @pl.when(pl.program_id(0) == 0)
        def _():
            ds_scr[...] = jnp.zeros_like(ds_scr)

        for ci in reversed(range(cb)):
            r = slice(ci * DN_CHUNK, (ci + 1) * DN_CHUNK)
            for h in range(heads):
                l = slice(h * DN_DK, (h + 1) * DN_DK)
                _, vjp = jax.vjp(_dn_chunk, q_ref[r, l], k_ref[r, l], v_ref[r, l], be_ref[r, l], ge_ref[r, l],
                                 st_ref[ci, h])
                dq, dk, dv, dbe, dge, ds = vjp((do_ref[r, l], ds_scr[h]))
                dq_ref[r, l] = dq
                dk_ref[r, l] = dk
                dv_ref[r, l] = dv
                dbe_ref[r, l] = dbe
                dge_ref[r, l] = dge
                ds_scr[h] = ds

    tok = pl.BlockSpec((rows, width), lambda i: (steps - 1 - i, 0))
    return pl.pallas_call(
        body, name=name + "_bwd", grid=(steps,),
        in_specs=[tok] * 5 + [pl.BlockSpec((cb, heads, DN_DK, DN_DV), lambda i: (steps - 1 - i, 0, 0, 0)), tok],
        out_specs=[tok] * 5,
        out_shape=[jax.ShapeDtypeStruct((s_len, width), F32)] * 5,
        scratch_shapes=[pltpu.VMEM((heads, DN_DK, DN_DV), F32)],
        compiler_params=pltpu.CompilerParams(dimension_semantics=("arbitrary",), vmem_limit_bytes=V7X_VMEM_LIMIT_BYTES),
    )(q, k, v, be, ge, states, do)


@functools.partial(jax.custom_vjp, nondiff_argnums=(5,))
def _dn_core(q, k, v, be, ge, name):
    return _dn_forward(q, k, v, be, ge, name)[0]


def _dn_core_fwd(q, k, v, be, ge, name):
    o, states = _dn_forward(q, k, v, be, ge, name)
    return o, (q, k, v, be, ge, states)


def _dn_core_bwd(name, res, do):
    return tuple(_dn_backward(*res, do, name))


_dn_core.defvjp(_dn_core_fwd, _dn_core_bwd)


def _layer_norm(x, g, b):
    mu = x.mean(-1, keepdims=True)
    var = jnp.square(x - mu).mean(-1, keepdims=True)
    return (x - mu) * lax.rsqrt(var + LN_EPS) * g + b


def _swiglu(x, w_gu, w_down, tag):
    return pmm(_ffn_act(pmm(x, w_gu, tag + "_gu"), tag + "_act"), w_down, tag + "_down")


def _hybrid_layer(x, mem_n, p, l):
    tag = f"l{l}"
    ln_g, ln_b = p["ln_g"][l], p["ln_b"][l]
    h = _res_ln(x, _swiglu(x, p["ffn1_w_gu"][l], p["ffn1_w_down"][l], tag + "_ffn1"), ln_g[0], ln_b[0], 0.5,
                tag + "_ln0")
    proj = pmm(h, p["w_in"][l], tag + "_in")
    seg = {name: proj[:, lo:hi] for name, (lo, hi) in SEG.items()}
    q, k, v, be, ge = _dn_pre(seg["dn_qkv"], seg["dn_ba"], p["dn_conv_w"][l], p["dn_a_log"][l], p["dn_dt_bias"][l],
                              tag + "_dn_pre")
    o_dn = _dn_post(_dn_core(q, k, v, be, ge, tag + "_dn"), seg["dn_z"], p["dn_norm_w"][l], tag + "_dn_post")
    o_sw = _swa(seg["sw_q"], seg["sw_kv"], p["swa_sinks"][l], tag + "_swa")
    kv_mem = pmm(mem_n, p["w_mem_kv"][l], tag + "_memkv")
    half = XA_HEADS * XA_DH
    o_xa = _xattn(seg["xa_q"], kv_mem[:, :half], kv_mem[:, half:], tag + "_xattn")
    branch = [pmm(o, p["w_branch"][l, i], tag + f"_branch{i}") for i, o in enumerate((o_dn, o_sw, o_xa))]
    merged = _gate_merge(seg["gates"], *branch, tag + "_merge")
    h = _res_ln(h, pmm(merged, p["w_out"][l], tag + "_out"), ln_g[1], ln_b[1], 1.0, tag + "_ln1")
    return _res_ln(h, _swiglu(h, p["ffn2_w_gu"][l], p["ffn2_w_down"][l], tag + "_ffn2"), ln_g[2], ln_b[2], 0.5,
                   tag + "_ln2")


def _local_loss(p, x, mem, target):
    mem_n = _layer_norm(mem, p["mem_ln_g"], p["mem_ln_b"])
    y = x
    for l in range(DEPTH):
        y = _hybrid_layer(y, mem_n, p, l)
    return jnp.sum(_row_loss(y, target, "row_loss")[:, 0])


def _w_in_permute(w):
    pad = jnp.zeros(w.shape[:-1] + (D_IN_PAD - D_IN,), w.dtype)
    return jnp.concatenate([w[..., :1536], w[..., 1544:], w[..., 1536:1544], pad], axis=-1)


def _w_in_unpermute(w):
    return jnp.concatenate([w[..., :1536], w[..., 6400:6408], w[..., 1536:6400]], axis=-1)


def _join_shards(gathered, ax):
    shp = gathered.shape[1:]
    t = jnp.moveaxis(gathered.astype(F32), 0, ax)
    return t.reshape(shp[:ax] + (N_DEV * shp[ax],) + shp[ax + 1:])


def _split_shards(whole, ax):
    shp = whole.shape
    t = whole.reshape(shp[:ax] + (N_DEV, shp[ax] // N_DEV) + shp[ax + 1:])
    return jnp.moveaxis(t, ax, 0)


def _pack_small(parts):
    flat = jnp.concatenate([t.reshape(-1).astype(F32) for t in parts])
    rows = -(-flat.shape[0] // LANES)
    rows = -(-rows // 8) * 8
    return jnp.concatenate([flat, jnp.zeros((rows * LANES - flat.shape[0],), F32)]).reshape(rows, LANES)


def _unpack_small(packed, shapes):
    flat = packed.reshape(packed.shape[:-2] + (-1,))
    out, off = [], 0
    for shp in shapes:
        size = int(np.prod(shp))
        out.append(flat[..., off:off + size].reshape(flat.shape[:-1] + tuple(shp)))
        off += size
    return out


def kernel(x, mem, mem_ln_g, mem_ln_b, ln_g, ln_b, ffn1_w_gu, ffn1_w_down, w_in, dn_conv_w, dn_a_log, dn_dt_bias, dn_norm_w, swa_sinks, w_mem_kv, w_branch, w_out, ffn2_w_gu, ffn2_w_down, loss_target, m_mem_ln_g, m_mem_ln_b, m_ln_g, m_ln_b, m_ffn1_w_gu, m_ffn1_w_down, m_w_in, m_dn_conv_w, m_dn_a_log, m_dn_dt_bias, m_dn_norm_w, m_swa_sinks, m_w_mem_kv, m_w_branch, m_w_out, m_ffn2_w_gu, m_ffn2_w_down, v_mem_ln_g, v_mem_ln_b, v_ln_g, v_ln_b, v_ffn1_w_gu, v_ffn1_w_down, v_w_in, v_dn_conv_w, v_dn_a_log, v_dn_dt_bias, v_dn_norm_w, v_swa_sinks, v_w_mem_kv, v_w_branch, v_w_out, v_ffn2_w_gu, v_ffn2_w_down):
    given = dict(locals())
    w_loc = {n: given[n] for n in WEIGHTS}
    m_loc = {n: given["m_" + n] for n in WEIGHTS}
    v_loc = {n: given["v_" + n] for n in WEIGHTS}
    me = _my_index()

    params = {n: _join_shards(_all_gather(w_loc[n].astype(BF16), "all_gather_" + n), BIG_AXIS[n]) for n in BIG}
    params["w_in"] = _w_in_permute(params["w_in"])
    small_shard_shapes = [w_loc[n].shape for n in SMALL_SHARDED]
    small_gathered = _small_all_gather(_pack_small([w_loc[n] for n in SMALL_SHARDED]), False, "all_gather_small")
    for n, t in zip(SMALL_SHARDED, _unpack_small(small_gathered, small_shard_shapes)):
        t = jnp.moveaxis(t, 0, -2)
        params[n] = t.reshape(t.shape[:-2] + (-1,))
    for n in SMALL_REPL:
        params[n] = w_loc[n]

    loss_local, (g_params, grad_x) = jax.value_and_grad(_local_loss, argnums=(0, 1))(
        params, x[0], mem[0], loss_target[0])
    grad_x = grad_x[None]
    g_params["w_in"] = _w_in_unpermute(g_params["w_in"])

    g_big = {n: _sum_slots(_exchange_slots(_split_shards(g_params[n], BIG_AXIS[n]), "exchange_grad_" + n),
                           "sum_grad_" + n) for n in BIG}
    small_names = SMALL_SHARDED + SMALL_REPL
    small_full_shapes = [g_params[n].shape for n in small_names] + [()]
    small_sum = _small_all_gather(_pack_small([g_params[n] for n in small_names] + [loss_local]), True, "all_sum_small")
    small_full = dict(zip(small_names + ("loss",), _unpack_small(small_sum, small_full_shapes)))
    loss = small_full["loss"]
    grads = dict(g_big)
    for n in SMALL_SHARDED:
        shard = w_loc[n].shape[-1]
        grads[n] = lax.dynamic_slice_in_dim(small_full[n], me * shard, shard, axis=small_full[n].ndim - 1)
    for n in SMALL_REPL:
        grads[n] = small_full[n]

    delta, new_m, new_v = {}, {}, {}
    for n in BIG:
        delta[n], new_m[n], new_v[n] = _adamw(w_loc[n], grads[n], m_loc[n], v_loc[n], "adamw_" + n)
    shapes = [w_loc[n].shape for n in small_names]
    packed = [_pack_small([src[n] for n in small_names]) for src in (w_loc, grads, m_loc, v_loc)]
    for dst, t in zip((delta, new_m, new_v), _adamw(*packed, "adamw_small")):
        dst.update(zip(small_names, _unpack_small(t, shapes)))

    return (loss, grad_x, *[grads[n] for n in WEIGHTS], *[delta[n] for n in WEIGHTS],
            *[new_m[n] for n in WEIGHTS], *[new_v[n] for n in WEIGHTS])
```

```python
import functools

import numpy as np
import jax
import jax.numpy as jnp
from jax import lax
from jax.experimental import pallas as pl
from jax.experimental.pallas import tpu as pltpu

F32 = jnp.float32
BF16 = jnp.bfloat16
MESH = pl.DeviceIdType.MESH
N_DEV = 8
V7X_VMEM_LIMIT_BYTES = 56 * 1024 * 1024
LANES = 128
PACK_W = 1024

D_MODEL = 1024
DEPTH = 2
DN_HEADS, DN_DK, DN_DV, DN_CONV, DN_CHUNK = 4, 128, 128, 4, 64
SWA_HEADS, SWA_KV_HEADS, SWA_DH, WINDOW = 8, 2, 64, 128
XA_HEADS, XA_DH = 4, 128
D_FF = 2816
N_BRANCH, BRANCH_W = 3, 512
DEEPNORM_ALPHA = (2 * DEPTH) ** 0.25
LN_EPS = 1e-5
RMS_EPS = 1e-6
NEG_INF = -1e30
D_IN = 6408
D_IN_PAD = 6528
SEG = dict(dn_qkv=(0, 1536), dn_z=(1536, 2048), sw_q=(2048, 2560), sw_kv=(2560, 2816), xa_q=(2816, 3328),
           gates=(3328, 6400), dn_ba=(6400, 6528))

ADAM_LR, ADAM_B1, ADAM_B2, ADAM_EPS, ADAM_WD, ADAM_STEP = 0.001, 0.9, 0.999, 1e-08, 0.01, 10

BIG = ("ffn1_w_gu", "ffn1_w_down", "w_in", "w_mem_kv", "w_branch", "w_out", "ffn2_w_gu", "ffn2_w_down")
BIG_AXIS = dict(ffn1_w_gu=2, ffn1_w_down=1, w_in=2, w_mem_kv=1, w_branch=3, w_out=1, ffn2_w_gu=2, ffn2_w_down=1)
SMALL_SHARDED = ("ln_g", "ln_b", "dn_conv_w")
SMALL_REPL = ("mem_ln_g", "mem_ln_b", "dn_a_log", "dn_dt_bias", "dn_norm_w", "swa_sinks")
WEIGHTS = ("mem_ln_g", "mem_ln_b", "ln_g", "ln_b", "ffn1_w_gu", "ffn1_w_down", "w_in", "dn_conv_w", "dn_a_log",
           "dn_dt_bias", "dn_norm_w", "swa_sinks", "w_mem_kv", "w_branch", "w_out", "ffn2_w_gu", "ffn2_w_down")

HI = lax.Precision.HIGHEST
NN = (((1,), (0,)), ((), ()))
NT = (((1,), (1,)), ((), ()))
TN = (((0,), (0,)), ((), ()))


def _tile(n, target, align):
    best = None
    for d in range(align, min(n, target) + 1, align):
        if n % d == 0:
            best = d
    return n if best is None else best


def _my_index():
    return 4 * lax.axis_index("x") + 2 * lax.axis_index("y") + lax.axis_index("c")


def _peer(k):
    x, y, c = lax.axis_index("x"), lax.axis_index("y"), lax.axis_index("c")
    px = 1 - x if (k >> 2) & 1 else x
    py = 1 - y if (k >> 1) & 1 else y
    pc = 1 - c if k & 1 else c
    return (px, py, pc), 4 * px + 2 * py + pc


def _matmul(a, b, *, ta=False, tb=False, name, tm=512, tn=2176, tk=2304):
    m, k = (a.shape[1], a.shape[0]) if ta else a.shape
    k2, n = (b.shape[1], b.shape[0]) if tb else b.shape
    assert k == k2, (a.shape, b.shape, ta, tb)
    tm = _tile(m, tm, LANES if ta else 8)
    tn = _tile(n, tn, LANES)
    tk = _tile(k, tk, LANES)
    nk = k // tk
    dims = (((0 if ta else 1,), (1 if tb else 0,)), ((), ()))

    def body(a_ref, b_ref, o_ref):
        kk = pl.program_id(2)
        p = lax.dot_general(a_ref[...].astype(BF16), b_ref[...].astype(BF16), dims, preferred_element_type=F32)

        @pl.when(kk == 0)
        def _():
            o_ref[...] = p

        @pl.when(kk != 0)
        def _():
            o_ref[...] += p

    a_spec = pl.BlockSpec((tk, tm), lambda j, i, kk: (kk, i)) if ta else pl.BlockSpec((tm, tk), lambda j, i, kk: (i, kk))
    b_spec = pl.BlockSpec((tn, tk), lambda j, i, kk: (j, kk)) if tb else pl.BlockSpec((tk, tn), lambda j, i, kk: (kk, j))
    return pl.pallas_call(
        body, name=name, grid=(n // tn, m // tm, nk), in_specs=[a_spec, b_spec],
        out_specs=pl.BlockSpec((tm, tn), lambda j, i, kk: (i, j)),
        out_shape=jax.ShapeDtypeStruct((m, n), F32),
        compiler_params=pltpu.CompilerParams(dimension_semantics=("parallel", "parallel", "arbitrary"),
                                             vmem_limit_bytes=V7X_VMEM_LIMIT_BYTES),
    )(a, b)


@functools.partial(jax.custom_vjp, nondiff_argnums=(2,))
def pmm(x, w, tag):
    return _matmul(x, w, name=tag + "_fwd")


def _pmm_fwd(x, w, tag):
    return _matmul(x, w, name=tag + "_fwd"), (x, w)


def _pmm_bwd(tag, res, dy):
    x, w = res
    dx = _matmul(dy, w, tb=True, name=tag + "_dx")
    dw = _matmul(x, dy, ta=True, name=tag + "_dw", tm=1408, tn=2176, tk=512)
    return dx, dw


pmm.defvjp(_pmm_fwd, _pmm_bwd)


def _all_gather(block, name):
    def body(x_ref, out_ref, send_sems, recv_sems, local_sem):
        me = _my_index()
        mine = pltpu.make_async_copy(x_ref, out_ref.at[me], local_sem)
        mine.start()
        sends = []
        for k in range(1, N_DEV):
            peer, _ = _peer(k)
            cp = pltpu.make_async_remote_copy(src_ref=x_ref, dst_ref=out_ref.at[me], send_sem=send_sems.at[k - 1],
                                              recv_sem=recv_sems.at[k - 1], device_id=peer, device_id_type=MESH)
            cp.start()
            sends.append(cp)
        for k in range(1, N_DEV):
            peer, p = _peer(k)
            pltpu.make_async_remote_copy(src_ref=x_ref, dst_ref=out_ref.at[p], send_sem=send_sems.at[k - 1],
                                         recv_sem=recv_sems.at[k - 1], device_id=peer, device_id_type=MESH).wait_recv()
        for cp in sends:
            cp.wait_send()
        mine.wait()

    return pl.pallas_call(
        body, name=name,
        out_shape=jax.ShapeDtypeStruct((N_DEV,) + block.shape, block.dtype),
        in_specs=[pl.BlockSpec(memory_space=pl.ANY)], out_specs=pl.BlockSpec(memory_space=pl.ANY),
        scratch_shapes=[pltpu.SemaphoreType.DMA((N_DEV - 1,)), pltpu.SemaphoreType.DMA((N_DEV - 1,)),
                        pltpu.SemaphoreType.DMA],
    )(block)


def _exchange_slots(slots, name):
    def body(x_ref, out_ref, send_sems, recv_sems, local_sem):
        me = _my_index()
        mine = pltpu.make_async_copy(x_ref.at[me], out_ref.at[me], local_sem)
        mine.start()
        sends = []
        for k in range(1, N_DEV):
            peer, p = _peer(k)
            cp = pltpu.make_async_remote_copy(src_ref=x_ref.at[p], dst_ref=out_ref.at[me], send_sem=send_sems.at[k - 1],
                                              recv_sem=recv_sems.at[k - 1], device_id=peer, device_id_type=MESH)
            cp.start()
            sends.append(cp)
        for k in range(1, N_DEV):
            peer, p = _peer(k)
            pltpu.make_async_remote_copy(src_ref=x_ref.at[p], dst_ref=out_ref.at[p], send_sem=send_sems.at[k - 1],
                                         recv_sem=recv_sems.at[k - 1], device_id=peer, device_id_type=MESH).wait_recv()
        for cp in sends:
            cp.wait_send()
        mine.wait()

    return pl.pallas_call(
        body, name=name,
        out_shape=jax.ShapeDtypeStruct(slots.shape, slots.dtype),
        in_specs=[pl.BlockSpec(memory_space=pl.ANY)], out_specs=pl.BlockSpec(memory_space=pl.ANY),
        scratch_shapes=[pltpu.SemaphoreType.DMA((N_DEV - 1,)), pltpu.SemaphoreType.DMA((N_DEV - 1,)),
                        pltpu.SemaphoreType.DMA],
    )(slots)


def _sum_slots(slots, name):
    shape = slots.shape[1:]
    w = shape[-1]
    slots = slots.reshape(N_DEV, -1, w)
    r = slots.shape[1]
    tr = _tile(r, 256, 8)

    def body(x_ref, o_ref):
        acc = x_ref[0].astype(F32)
        for p in range(1, N_DEV):
            acc = acc + x_ref[p].astype(F32)
        o_ref[...] = acc

    return pl.pallas_call(
        body, name=name, grid=(r // tr,),
        in_specs=[pl.BlockSpec((N_DEV, tr, w), lambda i: (0, i, 0))],
        out_specs=pl.BlockSpec((tr, w), lambda i: (i, 0)),
        out_shape=jax.ShapeDtypeStruct((r, w), F32),
        compiler_params=pltpu.CompilerParams(dimension_semantics=("parallel",), vmem_limit_bytes=V7X_VMEM_LIMIT_BYTES),
    )(slots).reshape(shape)


def _small_all_gather(v, reduce, name):
    rows, w = v.shape

    def body(x_ref, out_ref, land_ref, send_sems, recv_sems):
        me = _my_index()
        sends = []
        for k in range(1, N_DEV):
            peer, _ = _peer(k)
            cp = pltpu.make_async_remote_copy(src_ref=x_ref, dst_ref=land_ref.at[me], send_sem=send_sems.at[k - 1],
                                              recv_sem=recv_sems.at[k - 1], device_id=peer, device_id_type=MESH)
            cp.start()
            sends.append(cp)
        land_ref[me] = x_ref[...]
        for k in range(1, N_DEV):
            peer, p = _peer(k)
            pltpu.make_async_remote_copy(src_ref=x_ref, dst_ref=land_ref.at[p], send_sem=send_sems.at[k - 1],
                                         recv_sem=recv_sems.at[k - 1], device_id=peer, device_id_type=MESH).wait_recv()
        for cp in sends:
            cp.wait_send()
        if reduce:
            acc = land_ref[0]
            for p in range(1, N_DEV):
                acc = acc + land_ref[p]
            out_ref[...] = acc
        else:
            out_ref[...] = land_ref[...]

    out_shape = (rows, w) if reduce else (N_DEV, rows, w)
    return pl.pallas_call(
        body, name=name,
        out_shape=jax.ShapeDtypeStruct(out_shape, F32),
        in_specs=[pl.BlockSpec(memory_space=pltpu.VMEM)], out_specs=pl.BlockSpec(memory_space=pltpu.VMEM),
        scratch_shapes=[pltpu.VMEM((N_DEV, rows, w), F32), pltpu.SemaphoreType.DMA((N_DEV - 1,)),
                        pltpu.SemaphoreType.DMA((N_DEV - 1,))],
    )(v)


def _adamw(w, g, m, v, name):
    shape = w.shape
    cols = shape[-1] if len(shape) > 1 else shape[0]
    w2, g2, m2, v2 = (t.reshape(-1, cols) for t in (w, g, m, v))
    rows = w2.shape[0]
    tr = _tile(rows, 256, 8)

    def body(w_ref, g_ref, m_ref, v_ref, d_ref, mo_ref, vo_ref):
        gv = g_ref[...]
        mn = ADAM_B1 * m_ref[...] + (1.0 - ADAM_B1) * gv
        vn = ADAM_B2 * v_ref[...] + (1.0 - ADAM_B2) * (gv * gv)
        m_hat = mn / (1.0 - ADAM_B1 ** ADAM_STEP)
        v_hat = vn / (1.0 - ADAM_B2 ** ADAM_STEP)
        d_ref[...] = -ADAM_LR * (m_hat / (jnp.sqrt(v_hat) + ADAM_EPS) + ADAM_WD * w_ref[...])
        mo_ref[...] = mn
        vo_ref[...] = vn

    spec = pl.BlockSpec((tr, cols), lambda i: (i, 0))
    outs = pl.pallas_call(
        body, name=name, grid=(rows // tr,), in_specs=[spec] * 4, out_specs=[spec] * 3,
        out_shape=[jax.ShapeDtypeStruct((rows, cols), F32)] * 3,
        compiler_params=pltpu.CompilerParams(dimension_semantics=("parallel",), vmem_limit_bytes=V7X_VMEM_LIMIT_BYTES),
    )(w2, g2, m2, v2)
    return tuple(o.reshape(shape) for o in outs)


_DOT_DIMS = {2: dict(nn=NN, nt=NT, tn=TN),
             3: dict(nn=(((2,), (1,)), ((0,), (0,))), nt=(((2,), (2,)), ((0,), (0,))), tn=(((1,), (1,)), ((0,), (0,))))}


def _bf16_dot(a, b, kind):
    if a.ndim == 3:
        return jnp.stack([_bf16_dot(a[i], b[i] if b.ndim == 3 else b, kind) for i in range(a.shape[0])])
    return lax.dot_general(a.astype(BF16), b.astype(BF16), _DOT_DIMS[2][kind], preferred_element_type=F32)


def _f32_dot(a, b, kind):
    if a.ndim == 3:
        return jnp.stack([_f32_dot(a[i], b[i] if b.ndim == 3 else b, kind) for i in range(a.shape[0])])
    return lax.dot_general(a, b, _DOT_DIMS[2][kind], precision=HI, preferred_element_type=F32)


def _dot_family(raw):
    @jax.custom_vjp
    def nn(a, b):
        return raw(a, b, "nn")

    @jax.custom_vjp
    def nt(a, b):
        return raw(a, b, "nt")

    @jax.custom_vjp
    def tn(a, b):
        return raw(a, b, "tn")

    nn.defvjp(lambda a, b: (raw(a, b, "nn"), (a, b)), lambda r, ct: (nt(ct, r[1]), tn(r[0], ct)))
    nt.defvjp(lambda a, b: (raw(a, b, "nt"), (a, b)), lambda r, ct: (nn(ct, r[1]), tn(ct, r[0])))
    tn.defvjp(lambda a, b: (raw(a, b, "tn"), (a, b)), lambda r, ct: (nt(r[1], ct), nn(r[0], ct)))
    return nn, nt, tn


_bdot_nn, _bdot_nt, _bdot_tn = _dot_family(_bf16_dot)
_hdot_nn, _hdot_nt, _hdot_tn = _dot_family(_f32_dot)


def _rows(width, t):
    return (t, width), lambda i: (i, 0)


def _whole(shape):
    return tuple(shape), lambda i: (0,) * len(shape)


def _tiled_op(name, fn, steps, in_blocks, kinds, outs, grad_blocks=None, grad_fix=None):
    n_in, n_out = len(in_blocks), len(outs)
    diff = [j for j, kd in enumerate(kinds) if kd != "const"]
    has_param = any(kd == "param" for kd in kinds)
    in_specs = [pl.BlockSpec(b, m) for b, m in in_blocks]
    out_specs = [pl.BlockSpec(b, m) for _, b, m in outs]

    def forward(*arrays):
        def body(*refs):
            res = fn(pl.program_id(0), *[r[...] for r in refs[:n_in]])
            for o_ref, val in zip(refs[n_in:], res):
                o_ref[...] = val

        return pl.pallas_call(
            body, name=name + "_fwd", grid=(steps,), in_specs=in_specs, out_specs=out_specs,
            out_shape=[jax.ShapeDtypeStruct(s, F32) for s, _, _ in outs],
            compiler_params=pltpu.CompilerParams(dimension_semantics=("parallel",),
                                                 vmem_limit_bytes=V7X_VMEM_LIMIT_BYTES),
        )(*arrays)

    def backward(arrays, douts):
        g_specs = []
        for j in diff:
            if kinds[j] == "param":
                g_specs.append((arrays[j].shape,) + _whole(arrays[j].shape))
            else:
                g_specs.append((grad_blocks or {}).get(j, (arrays[j].shape,) + in_blocks[j]))

        def body(*refs):
            i = pl.program_id(0)
            vals = [r[...] for r in refs[:n_in]]
            dos = tuple(r[...] for r in refs[n_in:n_in + n_out])
            g_refs = refs[n_in + n_out:]

            def of_diff(*dv):
                full = list(vals)
                for j, val in zip(diff, dv):
                    full[j] = val
                return tuple(fn(i, *full))

            _, vjp = jax.vjp(of_diff, *[vals[j] for j in diff])
            grads = vjp(dos)
            for j, g_ref, g in zip(diff, g_refs, grads):
                if kinds[j] == "param":
                    @pl.when(i == 0)
                    def _(g_ref=g_ref, g=g):
                        g_ref[...] = g

                    @pl.when(i != 0)
                    def _(g_ref=g_ref, g=g):
                        g_ref[...] += g
                else:
                    g_ref[...] = g

        return pl.pallas_call(
            body, name=name + "_bwd", grid=(steps,), in_specs=in_specs + out_specs,
            out_specs=[pl.BlockSpec(b, m) for _, b, m in g_specs],
            out_shape=[jax.ShapeDtypeStruct(shp, F32) for shp, _, _ in g_specs],
            compiler_params=pltpu.CompilerParams(dimension_semantics=("arbitrary" if has_param else "parallel",),
                                                 vmem_limit_bytes=V7X_VMEM_LIMIT_BYTES),
        )(*arrays, *douts)

    @jax.custom_vjp
    def op(*arrays):
        return tuple(forward(*arrays))

    def op_fwd(*arrays):
        return tuple(forward(*arrays)), arrays

    def op_bwd(arrays, douts):
        grads = backward(arrays, douts)
        full = [jnp.zeros_like(a) for a in arrays]
        for j, g in zip(diff, grads):
            full[j] = grad_fix[j](g) if grad_fix and j in grad_fix else g
        return tuple(full)

    op.defvjp(op_fwd, op_bwd)
    return op


def _sigmoid(x):
    return jax.nn.sigmoid(x)


def _silu(x):
    return x * jax.nn.sigmoid(x)


def _softplus(x):
    return jnp.maximum(x, 0.0) + jnp.log(1.0 + jnp.exp(-jnp.abs(x)))


def _heads(t, width):
    return [t[:, h * width:(h + 1) * width] for h in range(t.shape[1] // width)]


def _ffn_act(gu, name):
    s, two_f = gu.shape
    f = two_f // 2
    t = _tile(s, 128, 8)

    def fn(i, gu_t):
        return (_silu(gu_t[:, :f]) * gu_t[:, f:],)

    return _tiled_op(name, fn, s // t, [_rows(two_f, t)], ["row"], [((s, f),) + _rows(f, t)])(gu)[0]


def _res_ln(x, f, g, b, scale, name):
    s, d = x.shape
    t = _tile(s, 512, 8)

    def fn(i, x_t, f_t, g_t, b_t):
        r = DEEPNORM_ALPHA * x_t + scale * f_t
        mu = jnp.mean(r, axis=-1, keepdims=True)
        var = jnp.mean(jnp.square(r - mu), axis=-1, keepdims=True)
        return ((r - mu) * lax.rsqrt(var + LN_EPS) * g_t + b_t,)

    op = _tiled_op(name, fn, s // t, [_rows(d, t), _rows(d, t), _whole((1, d)), _whole((1, d))],
                   ["row", "row", "param", "param"], [((s, d),) + _rows(d, t)])
    return op(x, f, g.reshape(1, d), b.reshape(1, d))[0]


def _row_loss(y, target, name):
    s, d = y.shape
    t = _tile(s, 512, 8)

    def fn(i, y_t, t_t):
        e = y_t - t_t
        return (jnp.broadcast_to(0.5 * jnp.mean(e * e, axis=-1, keepdims=True), (t, LANES)),)

    return _tiled_op(name, fn, s // t, [_rows(d, t), _rows(d, t)], ["row", "const"],
                     [((s, LANES),) + _rows(LANES, t)])(y, target)[0]


def _gate_merge(gates, b0, b1, b2, name):
    s, d = b0.shape
    t = _tile(s, 256, 8)

    def fn(i, g_t, b0_t, b1_t, b2_t):
        return (_sigmoid(g_t[:, :d]) * b0_t + _sigmoid(g_t[:, d:2 * d]) * b1_t + _sigmoid(g_t[:, 2 * d:]) * b2_t,)

    op = _tiled_op(name, fn, s // t, [_rows(3 * d, t)] + [_rows(d, t)] * 3, ["row"] * 4, [((s, d),) + _rows(d, t)])
    return op(gates, b0, b1, b2)[0]


def _dn_pre(x, ba, conv_w, a_log, dt_bias, name):
    s, xw = x.shape
    w = DN_HEADS * DN_DK
    t = _tile(s, 256, 8)
    steps = s // t
    lane_pad = jnp.zeros((LANES - 2 * DN_HEADS,), F32)
    pa = jnp.concatenate([jnp.zeros((DN_HEADS,), F32), a_log, lane_pad]).reshape(1, LANES)
    pd = jnp.concatenate([jnp.zeros((DN_HEADS,), F32), dt_bias, lane_pad]).reshape(1, LANES)
    cw = jnp.concatenate([conv_w, jnp.zeros((8 - DN_CONV, xw), F32)], axis=0)
    taps = [jnp.pad(x, ((DN_CONV - 1 - j, 0), (0, 0)))[:s] for j in range(DN_CONV - 1)] + [x]

    def unit(x):
        return x * lax.rsqrt(jnp.sum(x * x, axis=-1, keepdims=True) + RMS_EPS)

    def fn(i, x0_t, x1_t, x2_t, x3_t, ba_t, cw_t, pa_t, pd_t):
        tap = lax.broadcasted_iota(jnp.int32, cw_t.shape, 0)
        c_t = sum(x_t * jnp.sum(jnp.where(tap == j, cw_t, 0.0), axis=0, keepdims=True)
                  for j, x_t in enumerate((x0_t, x1_t, x2_t, x3_t)))
        act = _silu(c_t)
        q = jnp.concatenate([unit(x) for x in _heads(act[:, :w], DN_DK)], axis=1)
        k = jnp.concatenate([unit(x) for x in _heads(act[:, w:2 * w], DN_DK)], axis=1)
        beta = _sigmoid(ba_t)
        g = -jnp.exp(pa_t) * _softplus(ba_t + pd_t)
        src = lax.broadcasted_iota(jnp.int32, (LANES, w), 0)
        head = lax.broadcasted_iota(jnp.int32, (LANES, w), 1) >> int(np.log2(DN_DK))
        be = _hdot_nn(beta, (src == head).astype(F32))
        ge = _hdot_nn(g, (src == head + DN_HEADS).astype(F32))
        return q, k, act[:, 2 * w:], be, ge

    op = _tiled_op(name, fn, steps,
                   [_rows(xw, t)] * DN_CONV + [_rows(LANES, t), _whole((8, xw)), _whole((1, LANES)), _whole((1, LANES))],
                   ["row"] * (DN_CONV + 1) + ["param"] * 3, [((s, w),) + _rows(w, t)] * 5)
    return op(*taps, ba, cw, pa, pd)


def _dn_post(o, z, norm_w, name):
    s, w = o.shape
    t = _tile(s, 512, 8)

    def fn(i, o_t, z_t, nw_t):
        outs = [x * lax.rsqrt(jnp.mean(x * x, axis=-1, keepdims=True) + RMS_EPS) * nw_t * _silu(zz)
                for x, zz in zip(_heads(o_t, DN_DV), _heads(z_t, DN_DV))]
        return (jnp.concatenate(outs, axis=1),)

    op = _tiled_op(name, fn, s // t, [_rows(w, t), _rows(w, t), _whole((1, DN_DV))], ["row", "row", "param"],
                   [((s, w),) + _rows(w, t)])
    return op(o, z, norm_w.reshape(1, DN_DV))[0]


def _xattn(q, k_mem, v_mem, name):
    s, w = q.shape
    t = _tile(s, 512, 8)

    def fn(i, q_t, k_t, v_t):
        outs = []
        for qh, kh, vh in zip(_heads(q_t, XA_DH), _heads(k_t, XA_DH), _heads(v_t, XA_DH)):
            sc = _bdot_nt(qh, kh) * (XA_DH ** -0.5)
            e = jnp.exp(sc - jnp.max(sc, axis=-1, keepdims=True))
            outs.append(_bdot_nn(e / jnp.sum(e, axis=-1, keepdims=True), vh))
        return (jnp.concatenate(outs, axis=1),)

    op = _tiled_op(name, fn, s // t, [_rows(w, t), _whole(k_mem.shape), _whole(v_mem.shape)],
                   ["row", "param", "param"], [((s, w),) + _rows(w, t)])
    return op(q, k_mem, v_mem)[0]


def _swa(q, kv, sinks, name):
    s, w = q.shape
    wd = WINDOW
    grp = SWA_HEADS // SWA_KV_HEADS
    dh = SWA_DH
    steps = s // wd
    sink_rows = jnp.broadcast_to(sinks[:, None], (SWA_HEADS, LANES))

    def fn(i, q_t, kc_t, kp_t, sk_t):
        qr = lax.broadcasted_iota(jnp.int32, (grp * wd, 2 * wd), 0) & (wd - 1)
        kc = lax.broadcasted_iota(jnp.int32, (grp * wd, 2 * wd), 1)
        dist = qr + wd - kc
        mask = (dist >= 0) & (dist < wd) & ((kc >= wd) | (i > 0))
        head_row = lax.broadcasted_iota(jnp.int32, (SWA_HEADS, LANES), 0)
        outs = [None] * SWA_HEADS
        for hk in range(SWA_KV_HEADS):
            kh = jnp.concatenate([kp_t[:, hk * dh:(hk + 1) * dh], kc_t[:, hk * dh:(hk + 1) * dh]], axis=0)
            vh = jnp.concatenate([kp_t[:, (SWA_KV_HEADS + hk) * dh:(SWA_KV_HEADS + hk + 1) * dh],
                                  kc_t[:, (SWA_KV_HEADS + hk) * dh:(SWA_KV_HEADS + hk + 1) * dh]], axis=0)
            qg = jnp.concatenate([q_t[:, (hk * grp + g) * dh:(hk * grp + g + 1) * dh] for g in range(grp)], axis=0)
            sink = jnp.concatenate(
                [jnp.broadcast_to(jnp.sum(jnp.where(head_row == hk * grp + g, sk_t, 0.0), axis=0, keepdims=True),
                                  (wd, LANES)) for g in range(grp)], axis=0)
            sink = jnp.max(sink, axis=-1, keepdims=True)
            sc = jnp.where(mask, _bdot_nt(qg, kh) * (dh ** -0.5), NEG_INF)
            m = jnp.maximum(jnp.max(sc, axis=-1, keepdims=True), sink)
            p = jnp.exp(sc - m)
            p = p / (jnp.sum(p, axis=-1, keepdims=True) + jnp.exp(sink - m))
            o = _bdot_nn(p, vh)
            for g in range(grp):
                outs[hk * grp + g] = o[g * wd:(g + 1) * wd]
        return (jnp.concatenate(outs, axis=1),)

    kvw = kv.shape[1]
    prev_block = ((wd, kvw), lambda i: (jnp.maximum(i - 1, 0), 0))
    shift = lambda g: jnp.concatenate([g[wd:], jnp.zeros((wd, kvw), F32)], axis=0)
    op = _tiled_op(name, fn, steps, [_rows(w, wd), _rows(kvw, wd), prev_block, _whole((SWA_HEADS, LANES))],
                   ["row", "row", "row", "param"], [((s, w),) + _rows(w, wd)],
                   grad_blocks={2: (kv.shape,) + _rows(kvw, wd)}, grad_fix={2: shift})
    return op(q, kv, kv, sink_rows)[0]


def _dn_prepare(q, k, v, be, ge):
    b, c, dk = q.shape
    row = lax.broadcasted_iota(jnp.int32, (c, c), 0)
    col = lax.broadcasted_iota(jnp.int32, (c, c), 1)
    tril = row >= col
    gc = _hdot_nn(jnp.broadcast_to(tril.astype(F32), (b, c, c)), ge)
    g_col = _hdot_nn(gc, jnp.full((b, LANES, c), 1.0 / LANES, F32))
    g_row = _hdot_nt(jnp.full((b, c, LANES), 1.0 / LANES, F32), gc)
    decay = jnp.where(tril, jnp.exp(jnp.where(tril, g_col - g_row, 0.0)), 0.0)
    kb = k * be
    a = jnp.where(row > col, _bdot_nt(kb, k) * decay, 0.0)
    x = jnp.broadcast_to((row == col).astype(F32), (b, c, c))
    for lvl in range(int(np.log2(c))):
        off = ((row >> lvl) == (col >> lvl) + 1) & ((row >> (lvl + 1)) == (col >> (lvl + 1)))
        x = x - _hdot_nn(x, _hdot_nn(jnp.where(off, a, 0.0), x))
    eg = jnp.exp(gc)
    qs = q * (dk ** -0.5)
    qk = jnp.where(tril, _bdot_nt(qs, k) * decay, 0.0)
    gl_c = _hdot_nn(jnp.ones((b, c, c), F32), ge)
    gl_s = _hdot_nn(jnp.ones((b, dk, c), F32), ge)
    return _hdot_nn(x, v * be), _hdot_nn(x, kb * eg), qs * eg, k * jnp.exp(gl_c - gc), qk, jnp.exp(gl_s)


def _dn_stack(ref, chunks, heads):
    return jnp.stack([ref[ci * DN_CHUNK:(ci + 1) * DN_CHUNK, h * DN_DK:(h + 1) * DN_DK]
                      for ci in range(chunks) for h in range(heads)])


def _dn_step(u, w, q_dec, k_tail, qk, decay, s):
    v_new = u - _bdot_nn(w, s)
    o = _bdot_nn(q_dec, s) + _bdot_nn(qk, v_new)
    return o, s * decay + _bdot_tn(k_tail, v_new)


def _hdot(a, b, dims=NN):
    return lax.dot_general(a, b, dims, precision=HI, preferred_element_type=F32)


def _dn_chunk_heads(q, k, v, be, ge, s):
    c, dk = q[0].shape
    row = lax.broadcasted_iota(jnp.int32, (c, c), 0)
    col = lax.broadcasted_iota(jnp.int32, (c, c), 1)
    tril = row >= col
    trilf = tril.astype(F32)
    mean_rows = jnp.full((c, LANES), 1.0 / LANES, F32)
    gc = [_hdot(trilf, g) for g in ge]
    g_row = [_hdot(mean_rows, g, NT) for g in gc]
    decay = [jnp.where(tril, jnp.exp(jnp.where(tril, g[:, :c] - gr, 0.0)), 0.0) for g, gr in zip(gc, g_row)]
    kb = [ki * bi for ki, bi in zip(k, be)]
    a = [jnp.where(row > col, _bdot_nt(x_, y_) * d, 0.0) for x_, y_, d in zip(kb, k, decay)]
    x = [(row == col).astype(F32)] * len(q)
    for lvl in range(int(np.log2(c))):
        off = ((row >> lvl) == (col >> lvl) + 1) & ((row >> (lvl + 1)) == (col >> (lvl + 1)))
        t = [_hdot(jnp.where(off, ai, 0.0), xi) for ai, xi in zip(a, x)]
        x = [xi - _hdot(xi, ti) for xi, ti in zip(x, t)]
    eg = [jnp.exp(g) for g in gc]
    u = [_hdot(xi, vi * bi) for xi, vi, bi in zip(x, v, be)]
    w = [_hdot(xi, ki * ei) for xi, ki, ei in zip(x, kb, eg)]
    qs = [qi * (dk ** -0.5) for qi in q]
    qk = [jnp.where(tril, _bdot_nt(x_, y_) * d, 0.0) for x_, y_, d in zip(qs, k, decay)]
    gl = [jnp.sum(g, axis=0, keepdims=True) for g in ge]
    v_new = [ui - _bdot_nn(wi, si) for ui, wi, si in zip(u, w, s)]
    o = [_bdot_nn(qi * ei, si) + _bdot_nn(ai, vi) for qi, ei, si, ai, vi in zip(qs, eg, s, qk, v_new)]
    s_next = [si * jnp.exp(g) + _bdot_tn(ki * jnp.exp(g - gi), vi)
              for si, g, ki, gi, vi in zip(s, gl, k, gc, v_new)]
    return o, s_next


def _dn_forward(q, k, v, be, ge, name, chunks_per_step=4):
    s_len, width = q.shape
    heads = width // DN_DK
    n = s_len // DN_CHUNK
    cb = min(chunks_per_step, n)
    rows = cb * DN_CHUNK

    def body(q_ref, k_ref, v_ref, be_ref, ge_ref, o_ref, st_ref, s_scr):
        @pl.when(pl.program_id(0) == 0)
        def _():
            s_scr[...] = jnp.zeros_like(s_scr)

        parts = _dn_prepare(*[_dn_stack(ref, cb, heads) for ref in (q_ref, k_ref, v_ref, be_ref, ge_ref)])
        for ci in range(cb):
            s = s_scr[...]
            o, s_next = _dn_step(*[t[ci * heads:(ci + 1) * heads] for t in parts], s)
            for h in range(heads):
                st_ref[ci, h] = s[h]
                o_ref[ci * DN_CHUNK:(ci + 1) * DN_CHUNK, h * DN_DK:(h + 1) * DN_DK] = o[h]
            s_scr[...] = s_next

    tok = pl.BlockSpec((rows, width), lambda i: (i, 0))
    return pl.pallas_call(
        body, name=name + "_fwd", grid=(n // cb,), in_specs=[tok] * 5,
        out_specs=[tok, pl.BlockSpec((cb, heads, DN_DK, DN_DV), lambda i: (i, 0, 0, 0))],
        out_shape=[jax.ShapeDtypeStruct((s_len, width), F32), jax.ShapeDtypeStruct((n, heads, DN_DK, DN_DV), F32)],
        scratch_shapes=[pltpu.VMEM((heads, DN_DK, DN_DV), F32)],
        compiler_params=pltpu.CompilerParams(dimension_semantics=("arbitrary",), vmem_limit_bytes=V7X_VMEM_LIMIT_BYTES),
    )(q, k, v, be, ge)


def _dn_backward(q, k, v, be, ge, states, do, name, chunks_per_step=4):
    s_len, width = q.shape
    heads = width // DN_DK
    n = s_len // DN_CHUNK
    cb = min(chunks_per_step, n)
    rows = cb * DN_CHUNK
    steps = n // cb

    def body(q_ref, k_ref, v_ref, be_ref, ge_ref, st_ref, do_ref, dq_ref, dk_ref, dv_ref, dbe_ref, dge_ref, ds_scr):
        @pl.when(pl.program_id(0) == 0)
        def _():
            ds_scr[...] = jnp.zeros_like(ds_scr)

        lanes = [slice(h * DN_DK, (h + 1) * DN_DK) for h in range(heads)]
        for ci in reversed(range(cb)):
            r = slice(ci * DN_CHUNK, (ci + 1) * DN_CHUNK)
            _, vjp = jax.vjp(_dn_chunk_heads, *[[ref[r, l] for l in lanes]
                                                for ref in (q_ref, k_ref, v_ref, be_ref, ge_ref)],
                             [st_ref[ci, h] for h in range(heads)])
            *grads, ds = vjp(([do_ref[r, l] for l in lanes], [ds_scr[h] for h in range(heads)]))
            for g_ref, g in zip((dq_ref, dk_ref, dv_ref, dbe_ref, dge_ref), grads):
                for l, g_head in zip(lanes, g):
                    g_ref[r, l] = g_head
            for h in range(heads):
                ds_scr[h] = ds[h]

    tok = pl.BlockSpec((rows, width), lambda i: (steps - 1 - i, 0))
    return pl.pallas_call(
        body, name=name + "_bwd", grid=(steps,),
        in_specs=[tok] * 5 + [pl.BlockSpec((cb, heads, DN_DK, DN_DV), lambda i: (steps - 1 - i, 0, 0, 0)), tok],
        out_specs=[tok] * 5,
        out_shape=[jax.ShapeDtypeStruct((s_len, width), F32)] * 5,
        scratch_shapes=[pltpu.VMEM((heads, DN_DK, DN_DV), F32)],
        compiler_params=pltpu.CompilerParams(dimension_semantics=("arbitrary",), vmem_limit_bytes=V7X_VMEM_LIMIT_BYTES),
    )(q, k, v, be, ge, states, do)


@functools.partial(jax.custom_vjp, nondiff_argnums=(5,))
def _dn_core(q, k, v, be, ge, name):
    return _dn_forward(q, k, v, be, ge, name)[0]


def _dn_core_fwd(q, k, v, be, ge, name):
    o, states = _dn_forward(q, k, v, be, ge, name)
    return o, (q, k, v, be, ge, states)


def _dn_core_bwd(name, res, do):
    return tuple(_dn_backward(*res, do, name))


_dn_core.defvjp(_dn_core_fwd, _dn_core_bwd)


def _layer_norm(x, g, b):
    mu = x.mean(-1, keepdims=True)
    var = jnp.square(x - mu).mean(-1, keepdims=True)
    return (x - mu) * lax.rsqrt(var + LN_EPS) * g + b


def _swiglu(x, w_gu, w_down, tag):
    return pmm(_ffn_act(pmm(x, w_gu, tag + "_gu"), tag + "_act"), w_down, tag + "_down")


def _hybrid_layer(x, mem_n, p, l):
    tag = f"l{l}"
    ln_g, ln_b = p["ln_g"][l], p["ln_b"][l]
    h = _res_ln(x, _swiglu(x, p["ffn1_w_gu"][l], p["ffn1_w_down"][l], tag + "_ffn1"), ln_g[0], ln_b[0], 0.5,
                tag + "_ln0")
    proj = pmm(h, p["w_in"][l], tag + "_in")
    seg = {name: proj[:, lo:hi] for name, (lo, hi) in SEG.items()}
    q, k, v, be, ge = _dn_pre(seg["dn_qkv"], seg["dn_ba"], p["dn_conv_w"][l], p["dn_a_log"][l], p["dn_dt_bias"][l],
                              tag + "_dn_pre")
    o_dn = _dn_post(_dn_core(q, k, v, be, ge, tag + "_dn"), seg["dn_z"], p["dn_norm_w"][l], tag + "_dn_post")
    o_sw = _swa(seg["sw_q"], seg["sw_kv"], p["swa_sinks"][l], tag + "_swa")
    kv_mem = pmm(mem_n, p["w_mem_kv"][l], tag + "_memkv")
    half = XA_HEADS * XA_DH
    o_xa = _xattn(seg["xa_q"], kv_mem[:, :half], kv_mem[:, half:], tag + "_xattn")
    branch = [pmm(o, p["w_branch"][l, i], tag + f"_branch{i}") for i, o in enumerate((o_dn, o_sw, o_xa))]
    merged = _gate_merge(seg["gates"], *branch, tag + "_merge")
    h = _res_ln(h, pmm(merged, p["w_out"][l], tag + "_out"), ln_g[1], ln_b[1], 1.0, tag + "_ln1")
    return _res_ln(h, _swiglu(h, p["ffn2_w_gu"][l], p["ffn2_w_down"][l], tag + "_ffn2"), ln_g[2], ln_b[2], 0.5,
                   tag + "_ln2")


def _local_loss(p, x, mem, target):
    mem_n = _layer_norm(mem, p["mem_ln_g"], p["mem_ln_b"])
    y = x
    for l in range(DEPTH):
        y = _hybrid_layer(y, mem_n, p, l)
    return jnp.sum(_row_loss(y, target, "row_loss")[:, 0])


def _w_in_permute(w):
    pad = jnp.zeros(w.shape[:-1] + (D_IN_PAD - D_IN,), w.dtype)
    return jnp.concatenate([w[..., :1536], w[..., 1544:], w[..., 1536:1544], pad], axis=-1)


def _w_in_unpermute(w):
    return jnp.concatenate([w[..., :1536], w[..., 6400:6408], w[..., 1536:6400]], axis=-1)


def _join_shards(gathered, ax):
    shp = gathered.shape[1:]
    t = jnp.moveaxis(gathered.astype(F32), 0, ax)
    return t.reshape(shp[:ax] + (N_DEV * shp[ax],) + shp[ax + 1:])


def _split_shards(whole, ax):
    shp = whole.shape
    t = whole.reshape(shp[:ax] + (N_DEV, shp[ax] // N_DEV) + shp[ax + 1:])
    return jnp.moveaxis(t, ax, 0)


def _pack_small(parts):
    flat = jnp.concatenate([t.reshape(-1).astype(F32) for t in parts])
    rows = -(-flat.shape[0] // LANES)
    rows = -(-rows // 8) * 8
    return jnp.concatenate([flat, jnp.zeros((rows * LANES - flat.shape[0],), F32)]).reshape(rows, LANES)


def _unpack_small(packed, shapes):
    flat = packed.reshape(packed.shape[:-2] + (-1,))
    out, off = [], 0
    for shp in shapes:
        size = int(np.prod(shp))
        out.append(flat[..., off:off + size].reshape(flat.shape[:-1] + tuple(shp)))
        off += size
    return out


def kernel(x, mem, mem_ln_g, mem_ln_b, ln_g, ln_b, ffn1_w_gu, ffn1_w_down, w_in, dn_conv_w, dn_a_log, dn_dt_bias, dn_norm_w, swa_sinks, w_mem_kv, w_branch, w_out, ffn2_w_gu, ffn2_w_down, loss_target, m_mem_ln_g, m_mem_ln_b, m_ln_g, m_ln_b, m_ffn1_w_gu, m_ffn1_w_down, m_w_in, m_dn_conv_w, m_dn_a_log, m_dn_dt_bias, m_dn_norm_w, m_swa_sinks, m_w_mem_kv, m_w_branch, m_w_out, m_ffn2_w_gu, m_ffn2_w_down, v_mem_ln_g, v_mem_ln_b, v_ln_g, v_ln_b, v_ffn1_w_gu, v_ffn1_w_down, v_w_in, v_dn_conv_w, v_dn_a_log, v_dn_dt_bias, v_dn_norm_w, v_swa_sinks, v_w_mem_kv, v_w_branch, v_w_out, v_ffn2_w_gu, v_ffn2_w_down):
    given = dict(locals())
    w_loc = {n: given[n] for n in WEIGHTS}
    m_loc = {n: given["m_" + n] for n in WEIGHTS}
    v_loc = {n: given["v_" + n] for n in WEIGHTS}
    me = _my_index()

    params = {n: _join_shards(_all_gather(w_loc[n].astype(BF16), "all_gather_" + n), BIG_AXIS[n]) for n in BIG}
    params["w_in"] = _w_in_permute(params["w_in"])
    small_shard_shapes = [w_loc[n].shape for n in SMALL_SHARDED]
    small_gathered = _small_all_gather(_pack_small([w_loc[n] for n in SMALL_SHARDED]), False, "all_gather_small")
    for n, t in zip(SMALL_SHARDED, _unpack_small(small_gathered, small_shard_shapes)):
        t = jnp.moveaxis(t, 0, -2)
        params[n] = t.reshape(t.shape[:-2] + (-1,))
    for n in SMALL_REPL:
        params[n] = w_loc[n]

    loss_local, (g_params, grad_x) = jax.value_and_grad(_local_loss, argnums=(0, 1))(
        params, x[0], mem[0], loss_target[0])
    grad_x = grad_x[None]
    g_params["w_in"] = _w_in_unpermute(g_params["w_in"])

    g_big = {n: _sum_slots(_exchange_slots(_split_shards(g_params[n], BIG_AXIS[n]), "exchange_grad_" + n),
                           "sum_grad_" + n) for n in BIG}
    small_names = SMALL_SHARDED + SMALL_REPL
    small_full_shapes = [g_params[n].shape for n in small_names] + [()]
    small_sum = _small_all_gather(_pack_small([g_params[n] for n in small_names] + [loss_local]), True, "all_sum_small")
    small_full = dict(zip(small_names + ("loss",), _unpack_small(small_sum, small_full_shapes)))
    loss = small_full["loss"]
    grads = dict(g_big)
    for n in SMALL_SHARDED:
        shard = w_loc[n].shape[-1]
        grads[n] = lax.dynamic_slice_in_dim(small_full[n], me * shard, shard, axis=small_full[n].ndim - 1)
    for n in SMALL_REPL:
        grads[n] = small_full[n]

    delta, new_m, new_v = {}, {}, {}
    for n in BIG:
        delta[n], new_m[n], new_v[n] = _adamw(w_loc[n], grads[n], m_loc[n], v_loc[n], "adamw_" + n)
    shapes = [w_loc[n].shape for n in small_names]
    packed = [_pack_small([src[n] for n in small_names]) for src in (w_loc, grads, m_loc, v_loc)]
    for dst, t in zip((delta, new_m, new_v), _adamw(*packed, "adamw_small")):
        dst.update(zip(small_names, _unpack_small(t, shapes)))

    return (loss, grad_x, *[grads[n] for n in WEIGHTS], *[delta[n] for n in WEIGHTS],
            *[new_m[n] for n in WEIGHTS], *[new_v[n] for n in WEIGHTS])
```

```python
import functools

import numpy as np
import jax
import jax.numpy as jnp
from jax import lax
from jax.experimental import pallas as pl
from jax.experimental.pallas import tpu as pltpu

F32 = jnp.float32
BF16 = jnp.bfloat16
MESH = pl.DeviceIdType.MESH
N_DEV = 8
V7X_VMEM_LIMIT_BYTES = 56 * 1024 * 1024
LANES = 128
PACK_W = 1024

D_MODEL = 1024
DEPTH = 2
DN_HEADS, DN_DK, DN_DV, DN_CONV, DN_CHUNK = 4, 128, 128, 4, 64
SWA_HEADS, SWA_KV_HEADS, SWA_DH, WINDOW = 8, 2, 64, 128
XA_HEADS, XA_DH = 4, 128
D_FF = 2816
N_BRANCH, BRANCH_W = 3, 512
DEEPNORM_ALPHA = (2 * DEPTH) ** 0.25
LN_EPS = 1e-5
RMS_EPS = 1e-6
NEG_INF = -1e30
D_IN = 6408
D_IN_PAD = 6528
SEG = dict(dn_qkv=(0, 1536), dn_z=(1536, 2048), sw_q=(2048, 2560), sw_kv=(2560, 2816), xa_q=(2816, 3328),
           gates=(3328, 6400), dn_ba=(6400, 6528))

ADAM_LR, ADAM_B1, ADAM_B2, ADAM_EPS, ADAM_WD, ADAM_STEP = 0.001, 0.9, 0.999, 1e-08, 0.01, 10

BIG = ("ffn1_w_gu", "ffn1_w_down", "w_in", "w_mem_kv", "w_branch", "w_out", "ffn2_w_gu", "ffn2_w_down")
BIG_AXIS = dict(ffn1_w_gu=2, ffn1_w_down=1, w_in=2, w_mem_kv=1, w_branch=3, w_out=1, ffn2_w_gu=2, ffn2_w_down=1)
SMALL_SHARDED = ("ln_g", "ln_b", "dn_conv_w")
SMALL_REPL = ("mem_ln_g", "mem_ln_b", "dn_a_log", "dn_dt_bias", "dn_norm_w", "swa_sinks")
WEIGHTS = ("mem_ln_g", "mem_ln_b", "ln_g", "ln_b", "ffn1_w_gu", "ffn1_w_down", "w_in", "dn_conv_w", "dn_a_log",
           "dn_dt_bias", "dn_norm_w", "swa_sinks", "w_mem_kv", "w_branch", "w_out", "ffn2_w_gu", "ffn2_w_down")

HI = lax.Precision.HIGHEST
NN = (((1,), (0,)), ((), ()))
NT = (((1,), (1,)), ((), ()))
TN = (((0,), (0,)), ((), ()))


def _tile(n, target, align):
    best = None
    for d in range(align, min(n, target) + 1, align):
        if n % d == 0:
            best = d
    return n if best is None else best


def _my_index():
    return 4 * lax.axis_index("x") + 2 * lax.axis_index("y") + lax.axis_index("c")


def _peer(k):
    x, y, c = lax.axis_index("x"), lax.axis_index("y"), lax.axis_index("c")
    px = 1 - x if (k >> 2) & 1 else x
    py = 1 - y if (k >> 1) & 1 else y
    pc = 1 - c if k & 1 else c
    return (px, py, pc), 4 * px + 2 * py + pc


def _matmul(a, b, *, ta=False, tb=False, name, tm=512, tn=2176, tk=2304):
    m, k = (a.shape[1], a.shape[0]) if ta else a.shape
    k2, n = (b.shape[1], b.shape[0]) if tb else b.shape
    assert k == k2, (a.shape, b.shape, ta, tb)
    tm = _tile(m, tm, LANES if ta else 8)
    tn = _tile(n, tn, LANES)
    tk = _tile(k, tk, LANES)
    nk = k // tk
    dims = (((0 if ta else 1,), (1 if tb else 0,)), ((), ()))

    def body(a_ref, b_ref, o_ref):
        kk = pl.program_id(2)
        p = lax.dot_general(a_ref[...].astype(BF16), b_ref[...].astype(BF16), dims, preferred_element_type=F32)

        @pl.when(kk == 0)
        def _():
            o_ref[...] = p

        @pl.when(kk != 0)
        def _():
            o_ref[...] += p

    a_spec = pl.BlockSpec((tk, tm), lambda j, i, kk: (kk, i)) if ta else pl.BlockSpec((tm, tk), lambda j, i, kk: (i, kk))
    b_spec = pl.BlockSpec((tn, tk), lambda j, i, kk: (j, kk)) if tb else pl.BlockSpec((tk, tn), lambda j, i, kk: (kk, j))
    return pl.pallas_call(
        body, name=name, grid=(n // tn, m // tm, nk), in_specs=[a_spec, b_spec],
        out_specs=pl.BlockSpec((tm, tn), lambda j, i, kk: (i, j)),
        out_shape=jax.ShapeDtypeStruct((m, n), F32),
        compiler_params=pltpu.CompilerParams(dimension_semantics=("parallel", "parallel", "arbitrary"),
                                             vmem_limit_bytes=V7X_VMEM_LIMIT_BYTES),
    )(a, b)


@functools.partial(jax.custom_vjp, nondiff_argnums=(3,))
def pmm(x, w, w_bf16, tag):
    return _matmul(x, w_bf16, name=tag + "_fwd", tm=1024)


def _pmm_fwd(x, w, w_bf16, tag):
    return _matmul(x, w_bf16, name=tag + "_fwd", tm=1024), (x, w_bf16)


def _pmm_bwd(tag, res, dy):
    x, w_bf16 = res
    dx = _matmul(dy, w_bf16, tb=True, name=tag + "_dx", tm=1024)
    dw = _matmul(x, dy, ta=True, name=tag + "_dw", tm=1408, tn=2176, tk=512)
    return dx, dw, jnp.zeros_like(w_bf16)


pmm.defvjp(_pmm_fwd, _pmm_bwd)


def _all_gather(block, name):
    def body(x_ref, out_ref, send_sems, recv_sems, local_sem):
        me = _my_index()
        mine = pltpu.make_async_copy(x_ref, out_ref.at[me], local_sem)
        mine.start()
        sends = []
        for k in range(1, N_DEV):
            peer, _ = _peer(k)
            cp = pltpu.make_async_remote_copy(src_ref=x_ref, dst_ref=out_ref.at[me], send_sem=send_sems.at[k - 1],
                                              recv_sem=recv_sems.at[k - 1], device_id=peer, device_id_type=MESH)
            cp.start()
            sends.append(cp)
        for k in range(1, N_DEV):
            peer, p = _peer(k)
            pltpu.make_async_remote_copy(src_ref=x_ref, dst_ref=out_ref.at[p], send_sem=send_sems.at[k - 1],
                                         recv_sem=recv_sems.at[k - 1], device_id=peer, device_id_type=MESH).wait_recv()
        for cp in sends:
            cp.wait_send()
        mine.wait()

    return pl.pallas_call(
        body, name=name,
        out_shape=jax.ShapeDtypeStruct((N_DEV,) + block.shape, block.dtype),
        in_specs=[pl.BlockSpec(memory_space=pl.ANY)], out_specs=pl.BlockSpec(memory_space=pl.ANY),
        scratch_shapes=[pltpu.SemaphoreType.DMA((N_DEV - 1,)), pltpu.SemaphoreType.DMA((N_DEV - 1,)),
                        pltpu.SemaphoreType.DMA],
    )(block)


def _exchange_slots(slots, name):
    def body(x_ref, out_ref, send_sems, recv_sems, local_sem):
        me = _my_index()
        mine = pltpu.make_async_copy(x_ref.at[me], out_ref.at[me], local_sem)
        mine.start()
        sends = []
        for k in range(1, N_DEV):
            peer, p = _peer(k)
            cp = pltpu.make_async_remote_copy(src_ref=x_ref.at[p], dst_ref=out_ref.at[me], send_sem=send_sems.at[k - 1],
                                              recv_sem=recv_sems.at[k - 1], device_id=peer, device_id_type=MESH)
            cp.start()
            sends.append(cp)
        for k in range(1, N_DEV):
            peer, p = _peer(k)
            pltpu.make_async_remote_copy(src_ref=x_ref.at[p], dst_ref=out_ref.at[p], send_sem=send_sems.at[k - 1],
                                         recv_sem=recv_sems.at[k - 1], device_id=peer, device_id_type=MESH).wait_recv()
        for cp in sends:
            cp.wait_send()
        mine.wait()

    return pl.pallas_call(
        body, name=name,
        out_shape=jax.ShapeDtypeStruct(slots.shape, slots.dtype),
        in_specs=[pl.BlockSpec(memory_space=pl.ANY)], out_specs=pl.BlockSpec(memory_space=pl.ANY),
        scratch_shapes=[pltpu.SemaphoreType.DMA((N_DEV - 1,)), pltpu.SemaphoreType.DMA((N_DEV - 1,)),
                        pltpu.SemaphoreType.DMA],
    )(slots)


def _sum_slots(slots, name):
    shape = slots.shape[1:]
    w = shape[-1]
    slots = slots.reshape(N_DEV, -1, w)
    r = slots.shape[1]
    tr = _tile(r, 256, 8)

    def body(x_ref, o_ref):
        acc = x_ref[0].astype(F32)
        for p in range(1, N_DEV):
            acc = acc + x_ref[p].astype(F32)
        o_ref[...] = acc

    return pl.pallas_call(
        body, name=name, grid=(r // tr,),
        in_specs=[pl.BlockSpec((N_DEV, tr, w), lambda i: (0, i, 0))],
        out_specs=pl.BlockSpec((tr, w), lambda i: (i, 0)),
        out_shape=jax.ShapeDtypeStruct((r, w), F32),
        compiler_params=pltpu.CompilerParams(dimension_semantics=("parallel",), vmem_limit_bytes=V7X_VMEM_LIMIT_BYTES),
    )(slots).reshape(shape)


def _small_all_gather(v, reduce, name):
    rows, w = v.shape

    def body(x_ref, out_ref, land_ref, send_sems, recv_sems):
        me = _my_index()
        sends = []
        for k in range(1, N_DEV):
            peer, _ = _peer(k)
            cp = pltpu.make_async_remote_copy(src_ref=x_ref, dst_ref=land_ref.at[me], send_sem=send_sems.at[k - 1],
                                              recv_sem=recv_sems.at[k - 1], device_id=peer, device_id_type=MESH)
            cp.start()
            sends.append(cp)
        land_ref[me] = x_ref[...]
        for k in range(1, N_DEV):
            peer, p = _peer(k)
            pltpu.make_async_remote_copy(src_ref=x_ref, dst_ref=land_ref.at[p], send_sem=send_sems.at[k - 1],
                                         recv_sem=recv_sems.at[k - 1], device_id=peer, device_id_type=MESH).wait_recv()
        for cp in sends:
            cp.wait_send()
        if reduce:
            acc = land_ref[0]
            for p in range(1, N_DEV):
                acc = acc + land_ref[p]
            out_ref[...] = acc
        else:
            out_ref[...] = land_ref[...]

    out_shape = (rows, w) if reduce else (N_DEV, rows, w)
    return pl.pallas_call(
        body, name=name,
        out_shape=jax.ShapeDtypeStruct(out_shape, F32),
        in_specs=[pl.BlockSpec(memory_space=pltpu.VMEM)], out_specs=pl.BlockSpec(memory_space=pltpu.VMEM),
        scratch_shapes=[pltpu.VMEM((N_DEV, rows, w), F32), pltpu.SemaphoreType.DMA((N_DEV - 1,)),
                        pltpu.SemaphoreType.DMA((N_DEV - 1,))],
    )(v)


def _adamw(w, g, m, v, name):
    shape = w.shape
    cols = shape[-1] if len(shape) > 1 else shape[0]
    w2, g2, m2, v2 = (t.reshape(-1, cols) for t in (w, g, m, v))
    rows = w2.shape[0]
    tr = _tile(rows, 256, 8)

    def body(w_ref, g_ref, m_ref, v_ref, d_ref, mo_ref, vo_ref):
        gv = g_ref[...]
        mn = ADAM_B1 * m_ref[...] + (1.0 - ADAM_B1) * gv
        vn = ADAM_B2 * v_ref[...] + (1.0 - ADAM_B2) * (gv * gv)
        m_hat = mn / (1.0 - ADAM_B1 ** ADAM_STEP)
        v_hat = vn / (1.0 - ADAM_B2 ** ADAM_STEP)
        d_ref[...] = -ADAM_LR * (m_hat / (jnp.sqrt(v_hat) + ADAM_EPS) + ADAM_WD * w_ref[...])
        mo_ref[...] = mn
        vo_ref[...] = vn

    spec = pl.BlockSpec((tr, cols), lambda i: (i, 0))
    outs = pl.pallas_call(
        body, name=name, grid=(rows // tr,), in_specs=[spec] * 4, out_specs=[spec] * 3,
        out_shape=[jax.ShapeDtypeStruct((rows, cols), F32)] * 3,
        compiler_params=pltpu.CompilerParams(dimension_semantics=("parallel",), vmem_limit_bytes=V7X_VMEM_LIMIT_BYTES),
    )(w2, g2, m2, v2)
    return tuple(o.reshape(shape) for o in outs)


_DOT_DIMS = {2: dict(nn=NN, nt=NT, tn=TN),
             3: dict(nn=(((2,), (1,)), ((0,), (0,))), nt=(((2,), (2,)), ((0,), (0,))), tn=(((1,), (1,)), ((0,), (0,))))}


def _bf16_dot(a, b, kind):
    if a.ndim == 3:
        return jnp.stack([_bf16_dot(a[i], b[i] if b.ndim == 3 else b, kind) for i in range(a.shape[0])])
    return lax.dot_general(a.astype(BF16), b.astype(BF16), _DOT_DIMS[2][kind], preferred_element_type=F32)


def _f32_dot(a, b, kind):
    if a.ndim == 3:
        return jnp.stack([_f32_dot(a[i], b[i] if b.ndim == 3 else b, kind) for i in range(a.shape[0])])
    return lax.dot_general(a, b, _DOT_DIMS[2][kind], precision=HI, preferred_element_type=F32)


def _dot_family(raw):
    @jax.custom_vjp
    def nn(a, b):
        return raw(a, b, "nn")

    @jax.custom_vjp
    def nt(a, b):
        return raw(a, b, "nt")

    @jax.custom_vjp
    def tn(a, b):
        return raw(a, b, "tn")

    nn.defvjp(lambda a, b: (raw(a, b, "nn"), (a, b)), lambda r, ct: (nt(ct, r[1]), tn(r[0], ct)))
    nt.defvjp(lambda a, b: (raw(a, b, "nt"), (a, b)), lambda r, ct: (nn(ct, r[1]), tn(ct, r[0])))
    tn.defvjp(lambda a, b: (raw(a, b, "tn"), (a, b)), lambda r, ct: (nt(r[1], ct), nn(r[0], ct)))
    return nn, nt, tn


_bdot_nn, _bdot_nt, _bdot_tn = _dot_family(_bf16_dot)
_hdot_nn, _hdot_nt, _hdot_tn = _dot_family(_f32_dot)


def _rows(width, t):
    return (t, width), lambda i: (i, 0)


def _whole(shape):
    return tuple(shape), lambda i: (0,) * len(shape)


def _tiled_op(name, fn, steps, in_blocks, kinds, outs, grad_blocks=None, grad_fix=None):
    n_in, n_out = len(in_blocks), len(outs)
    diff = [j for j, kd in enumerate(kinds) if kd != "const"]
    has_param = any(kd == "param" for kd in kinds)
    in_specs = [pl.BlockSpec(b, m) for b, m in in_blocks]
    out_specs = [pl.BlockSpec(b, m) for _, b, m in outs]

    def forward(*arrays):
        def body(*refs):
            res = fn(pl.program_id(0), *[r[...] for r in refs[:n_in]])
            for o_ref, val in zip(refs[n_in:], res):
                o_ref[...] = val

        return pl.pallas_call(
            body, name=name + "_fwd", grid=(steps,), in_specs=in_specs, out_specs=out_specs,
            out_shape=[jax.ShapeDtypeStruct(s, F32) for s, _, _ in outs],
            compiler_params=pltpu.CompilerParams(dimension_semantics=("parallel",),
                                                 vmem_limit_bytes=V7X_VMEM_LIMIT_BYTES),
        )(*arrays)

    def backward(arrays, douts):
        g_specs = []
        for j in diff:
            if kinds[j] == "param":
                g_specs.append((arrays[j].shape,) + _whole(arrays[j].shape))
            else:
                g_specs.append((grad_blocks or {}).get(j, (arrays[j].shape,) + in_blocks[j]))

        def body(*refs):
            i = pl.program_id(0)
            vals = [r[...] for r in refs[:n_in]]
            dos = tuple(r[...] for r in refs[n_in:n_in + n_out])
            g_refs = refs[n_in + n_out:]

            def of_diff(*dv):
                full = list(vals)
                for j, val in zip(diff, dv):
                    full[j] = val
                return tuple(fn(i, *full))

            _, vjp = jax.vjp(of_diff, *[vals[j] for j in diff])
            grads = vjp(dos)
            for j, g_ref, g in zip(diff, g_refs, grads):
                if kinds[j] == "param":
                    @pl.when(i == 0)
                    def _(g_ref=g_ref, g=g):
                        g_ref[...] = g

                    @pl.when(i != 0)
                    def _(g_ref=g_ref, g=g):
                        g_ref[...] += g
                else:
                    g_ref[...] = g

        return pl.pallas_call(
            body, name=name + "_bwd", grid=(steps,), in_specs=in_specs + out_specs,
            out_specs=[pl.BlockSpec(b, m) for _, b, m in g_specs],
            out_shape=[jax.ShapeDtypeStruct(shp, F32) for shp, _, _ in g_specs],
            compiler_params=pltpu.CompilerParams(dimension_semantics=("arbitrary" if has_param else "parallel",),
                                                 vmem_limit_bytes=V7X_VMEM_LIMIT_BYTES),
        )(*arrays, *douts)

    @jax.custom_vjp
    def op(*arrays):
        return tuple(forward(*arrays))

    def op_fwd(*arrays):
        return tuple(forward(*arrays)), arrays

    def op_bwd(arrays, douts):
        grads = backward(arrays, douts)
        full = [jnp.zeros_like(a) for a in arrays]
        for j, g in zip(diff, grads):
            full[j] = grad_fix[j](g) if grad_fix and j in grad_fix else g
        return tuple(full)

    op.defvjp(op_fwd, op_bwd)
    return op


def _sigmoid(x):
    return jax.nn.sigmoid(x)


def _silu(x):
    return x * jax.nn.sigmoid(x)


def _softplus(x):
    return jnp.maximum(x, 0.0) + jnp.log(1.0 + jnp.exp(-jnp.abs(x)))


def _heads(t, width):
    return [t[:, h * width:(h + 1) * width] for h in range(t.shape[1] // width)]


def _ffn_act(gu, name):
    s, two_f = gu.shape
    f = two_f // 2
    t = _tile(s, 128, 8)

    def fn(i, gu_t):
        return (_silu(gu_t[:, :f]) * gu_t[:, f:],)

    return _tiled_op(name, fn, s // t, [_rows(two_f, t)], ["row"], [((s, f),) + _rows(f, t)])(gu)[0]


def _res_ln(x, f, g, b, scale, name):
    s, d = x.shape
    t = _tile(s, 512, 8)

    def fn(i, x_t, f_t, g_t, b_t):
        r = DEEPNORM_ALPHA * x_t + scale * f_t
        mu = jnp.mean(r, axis=-1, keepdims=True)
        var = jnp.mean(jnp.square(r - mu), axis=-1, keepdims=True)
        return ((r - mu) * lax.rsqrt(var + LN_EPS) * g_t + b_t,)

    op = _tiled_op(name, fn, s // t, [_rows(d, t), _rows(d, t), _whole((1, d)), _whole((1, d))],
                   ["row", "row", "param", "param"], [((s, d),) + _rows(d, t)])
    return op(x, f, g.reshape(1, d), b.reshape(1, d))[0]


def _row_loss(y, target, name):
    s, d = y.shape
    t = _tile(s, 512, 8)

    def fn(i, y_t, t_t):
        e = y_t - t_t
        return (jnp.broadcast_to(0.5 * jnp.mean(e * e, axis=-1, keepdims=True), (t, LANES)),)

    return _tiled_op(name, fn, s // t, [_rows(d, t), _rows(d, t)], ["row", "const"],
                     [((s, LANES),) + _rows(LANES, t)])(y, target)[0]


def _gate_merge(gates, b0, b1, b2, name):
    s, d = b0.shape
    t = _tile(s, 256, 8)

    def fn(i, g_t, b0_t, b1_t, b2_t):
        return (_sigmoid(g_t[:, :d]) * b0_t + _sigmoid(g_t[:, d:2 * d]) * b1_t + _sigmoid(g_t[:, 2 * d:]) * b2_t,)

    op = _tiled_op(name, fn, s // t, [_rows(3 * d, t)] + [_rows(d, t)] * 3, ["row"] * 4, [((s, d),) + _rows(d, t)])
    return op(gates, b0, b1, b2)[0]


def _dn_pre(x, ba, conv_w, a_log, dt_bias, name):
    s, xw = x.shape
    w = DN_HEADS * DN_DK
    t = _tile(s, 256, 8)
    steps = s // t
    lane_pad = jnp.zeros((LANES - 2 * DN_HEADS,), F32)
    pa = jnp.concatenate([jnp.zeros((DN_HEADS,), F32), a_log, lane_pad]).reshape(1, LANES)
    pd = jnp.concatenate([jnp.zeros((DN_HEADS,), F32), dt_bias, lane_pad]).reshape(1, LANES)
    cw = jnp.concatenate([conv_w, jnp.zeros((8 - DN_CONV, xw), F32)], axis=0)
    taps = [jnp.pad(x, ((DN_CONV - 1 - j, 0), (0, 0)))[:s] for j in range(DN_CONV - 1)] + [x]

    def unit(x):
        return x * lax.rsqrt(jnp.sum(x * x, axis=-1, keepdims=True) + RMS_EPS)

    def fn(i, x0_t, x1_t, x2_t, x3_t, ba_t, cw_t, pa_t, pd_t):
        tap = lax.broadcasted_iota(jnp.int32, cw_t.shape, 0)
        c_t = sum(x_t * jnp.sum(jnp.where(tap == j, cw_t, 0.0), axis=0, keepdims=True)
                  for j, x_t in enumerate((x0_t, x1_t, x2_t, x3_t)))
        act = _silu(c_t)
        q = jnp.concatenate([unit(x) for x in _heads(act[:, :w], DN_DK)], axis=1)
        k = jnp.concatenate([unit(x) for x in _heads(act[:, w:2 * w], DN_DK)], axis=1)
        beta = _sigmoid(ba_t)
        g = -jnp.exp(pa_t) * _softplus(ba_t + pd_t)
        src = lax.broadcasted_iota(jnp.int32, (LANES, w), 0)
        head = lax.broadcasted_iota(jnp.int32, (LANES, w), 1) >> int(np.log2(DN_DK))
        be = _hdot_nn(beta, (src == head).astype(F32))
        ge = _hdot_nn(g, (src == head + DN_HEADS).astype(F32))
        return q, k, act[:, 2 * w:], be, ge

    op = _tiled_op(name, fn, steps,
                   [_rows(xw, t)] * DN_CONV + [_rows(LANES, t), _whole((8, xw)), _whole((1, LANES)), _whole((1, LANES))],
                   ["row"] * (DN_CONV + 1) + ["param"] * 3, [((s, w),) + _rows(w, t)] * 5)
    return op(*taps, ba, cw, pa, pd)


def _dn_post(o, z, norm_w, name):
    s, w = o.shape
    t = _tile(s, 512, 8)

    def fn(i, o_t, z_t, nw_t):
        outs = [x * lax.rsqrt(jnp.mean(x * x, axis=-1, keepdims=True) + RMS_EPS) * nw_t * _silu(zz)
                for x, zz in zip(_heads(o_t, DN_DV), _heads(z_t, DN_DV))]
        return (jnp.concatenate(outs, axis=1),)

    op = _tiled_op(name, fn, s // t, [_rows(w, t), _rows(w, t), _whole((1, DN_DV))], ["row", "row", "param"],
                   [((s, w),) + _rows(w, t)])
    return op(o, z, norm_w.reshape(1, DN_DV))[0]


def _xattn(q, k_mem, v_mem, name):
    s, w = q.shape
    t = _tile(s, 512, 8)

    def fn(i, q_t, k_t, v_t):
        outs = []
        for qh, kh, vh in zip(_heads(q_t, XA_DH), _heads(k_t, XA_DH), _heads(v_t, XA_DH)):
            sc = _bdot_nt(qh, kh) * (XA_DH ** -0.5)
            e = jnp.exp(sc - jnp.max(sc, axis=-1, keepdims=True))
            outs.append(_bdot_nn(e / jnp.sum(e, axis=-1, keepdims=True), vh))
        return (jnp.concatenate(outs, axis=1),)

    op = _tiled_op(name, fn, s // t, [_rows(w, t), _whole(k_mem.shape), _whole(v_mem.shape)],
                   ["row", "param", "param"], [((s, w),) + _rows(w, t)])
    return op(q, k_mem, v_mem)[0]


def _swa(q, kv, sinks, name):
    s, w = q.shape
    wd = WINDOW
    grp = SWA_HEADS // SWA_KV_HEADS
    dh = SWA_DH
    steps = s // wd
    sink_rows = jnp.broadcast_to(sinks[:, None], (SWA_HEADS, LANES))

    def fn(i, q_t, kc_t, kp_t, sk_t):
        qr = lax.broadcasted_iota(jnp.int32, (grp * wd, 2 * wd), 0) & (wd - 1)
        kc = lax.broadcasted_iota(jnp.int32, (grp * wd, 2 * wd), 1)
        dist = qr + wd - kc
        mask = (dist >= 0) & (dist < wd) & ((kc >= wd) | (i > 0))
        head_row = lax.broadcasted_iota(jnp.int32, (SWA_HEADS, LANES), 0)
        outs = [None] * SWA_HEADS
        for hk in range(SWA_KV_HEADS):
            kh = jnp.concatenate([kp_t[:, hk * dh:(hk + 1) * dh], kc_t[:, hk * dh:(hk + 1) * dh]], axis=0)
            vh = jnp.concatenate([kp_t[:, (SWA_KV_HEADS + hk) * dh:(SWA_KV_HEADS + hk + 1) * dh],
                                  kc_t[:, (SWA_KV_HEADS + hk) * dh:(SWA_KV_HEADS + hk + 1) * dh]], axis=0)
            qg = jnp.concatenate([q_t[:, (hk * grp + g) * dh:(hk * grp + g + 1) * dh] for g in range(grp)], axis=0)
            sink = jnp.concatenate(
                [jnp.broadcast_to(jnp.sum(jnp.where(head_row == hk * grp + g, sk_t, 0.0), axis=0, keepdims=True),
                                  (wd, LANES)) for g in range(grp)], axis=0)
            sink = jnp.max(sink, axis=-1, keepdims=True)
            sc = jnp.where(mask, _bdot_nt(qg, kh) * (dh ** -0.5), NEG_INF)
            m = jnp.maximum(jnp.max(sc, axis=-1, keepdims=True), sink)
            p = jnp.exp(sc - m)
            p = p / (jnp.sum(p, axis=-1, keepdims=True) + jnp.exp(sink - m))
            o = _bdot_nn(p, vh)
            for g in range(grp):
                outs[hk * grp + g] = o[g * wd:(g + 1) * wd]
        return (jnp.concatenate(outs, axis=1),)

    kvw = kv.shape[1]
    prev_block = ((wd, kvw), lambda i: (jnp.maximum(i - 1, 0), 0))
    shift = lambda g: jnp.concatenate([g[wd:], jnp.zeros((wd, kvw), F32)], axis=0)
    op = _tiled_op(name, fn, steps, [_rows(w, wd), _rows(kvw, wd), prev_block, _whole((SWA_HEADS, LANES))],
                   ["row", "row", "row", "param"], [((s, w),) + _rows(w, wd)],
                   grad_blocks={2: (kv.shape,) + _rows(kvw, wd)}, grad_fix={2: shift})
    return op(q, kv, kv, sink_rows)[0]


def _dn_prepare(q, k, v, be, ge):
    b, c, dk = q.shape
    row = lax.broadcasted_iota(jnp.int32, (c, c), 0)
    col = lax.broadcasted_iota(jnp.int32, (c, c), 1)
    tril = row >= col
    gc = _hdot_nn(jnp.broadcast_to(tril.astype(F32), (b, c, c)), ge)
    g_col = _hdot_nn(gc, jnp.full((b, LANES, c), 1.0 / LANES, F32))
    g_row = _hdot_nt(jnp.full((b, c, LANES), 1.0 / LANES, F32), gc)
    decay = jnp.where(tril, jnp.exp(jnp.where(tril, g_col - g_row, 0.0)), 0.0)
    kb = k * be
    a = jnp.where(row > col, _bdot_nt(kb, k) * decay, 0.0)
    x = jnp.broadcast_to((row == col).astype(F32), (b, c, c))
    for lvl in range(int(np.log2(c))):
        off = ((row >> lvl) == (col >> lvl) + 1) & ((row >> (lvl + 1)) == (col >> (lvl + 1)))
        x = x - _hdot_nn(x, _hdot_nn(jnp.where(off, a, 0.0), x))
    eg = jnp.exp(gc)
    qs = q * (dk ** -0.5)
    qk = jnp.where(tril, _bdot_nt(qs, k) * decay, 0.0)
    gl_c = _hdot_nn(jnp.ones((b, c, c), F32), ge)
    gl_s = _hdot_nn(jnp.ones((b, dk, c), F32), ge)
    return _hdot_nn(x, v * be), _hdot_nn(x, kb * eg), qs * eg, k * jnp.exp(gl_c - gc), qk, jnp.exp(gl_s)


def _dn_stack(ref, chunks, heads):
    return jnp.stack([ref[ci * DN_CHUNK:(ci + 1) * DN_CHUNK, h * DN_DK:(h + 1) * DN_DK]
                      for ci in range(chunks) for h in range(heads)])


def _dn_step(u, w, q_dec, k_tail, qk, decay, s):
    v_new = u - _bdot_nn(w, s)
    o = _bdot_nn(q_dec, s) + _bdot_nn(qk, v_new)
    return o, s * decay + _bdot_tn(k_tail, v_new)


def _hdot(a, b, dims=NN):
    return lax.dot_general(a, b, dims, precision=HI, preferred_element_type=F32)


def _dn_chunk_heads(q, k, v, be, ge, s):
    c, dk = q[0].shape
    row = lax.broadcasted_iota(jnp.int32, (c, c), 0)
    col = lax.broadcasted_iota(jnp.int32, (c, c), 1)
    tril = row >= col
    trilf = tril.astype(F32)
    mean_rows = jnp.full((c, LANES), 1.0 / LANES, F32)
    gc = [_hdot(trilf, g) for g in ge]
    g_row = [_hdot(mean_rows, g, NT) for g in gc]
    decay = [jnp.where(tril, jnp.exp(jnp.where(tril, g[:, :c] - gr, 0.0)), 0.0) for g, gr in zip(gc, g_row)]
    kb = [ki * bi for ki, bi in zip(k, be)]
    a = [jnp.where(row > col, _bdot_nt(x_, y_) * d, 0.0) for x_, y_, d in zip(kb, k, decay)]
    x = [(row == col).astype(F32)] * len(q)
    for lvl in range(int(np.log2(c))):
        off = ((row >> lvl) == (col >> lvl) + 1) & ((row >> (lvl + 1)) == (col >> (lvl + 1)))
        t = [_hdot(jnp.where(off, ai, 0.0), xi) for ai, xi in zip(a, x)]
        x = [xi - _hdot(xi, ti) for xi, ti in zip(x, t)]
    eg = [jnp.exp(g) for g in gc]
    u = [_hdot(xi, vi * bi) for xi, vi, bi in zip(x, v, be)]
    w = [_hdot(xi, ki * ei) for xi, ki, ei in zip(x, kb, eg)]
    qs = [qi * (dk ** -0.5) for qi in q]
    qk = [jnp.where(tril, _bdot_nt(x_, y_) * d, 0.0) for x_, y_, d in zip(qs, k, decay)]
    gl = [jnp.sum(g, axis=0, keepdims=True) for g in ge]
    v_new = [ui - _bdot_nn(wi, si) for ui, wi, si in zip(u, w, s)]
    o = [_bdot_nn(qi * ei, si) + _bdot_nn(ai, vi) for qi, ei, si, ai, vi in zip(qs, eg, s, qk, v_new)]
    s_next = [si * jnp.exp(g) + _bdot_tn(ki * jnp.exp(g - gi), vi)
              for si, g, ki, gi, vi in zip(s, gl, k, gc, v_new)]
    return o, s_next


def _dn_forward(q, k, v, be, ge, name, chunks_per_step=4):
    s_len, width = q.shape
    heads = width // DN_DK
    n = s_len // DN_CHUNK
    cb = min(chunks_per_step, n)
    rows = cb * DN_CHUNK

    def body(q_ref, k_ref, v_ref, be_ref, ge_ref, o_ref, st_ref, s_scr):
        @pl.when(pl.program_id(0) == 0)
        def _():
            s_scr[...] = jnp.zeros_like(s_scr)

        parts = _dn_prepare(*[_dn_stack(ref, cb, heads) for ref in (q_ref, k_ref, v_ref, be_ref, ge_ref)])
        for ci in range(cb):
            s = s_scr[...]
            o, s_next = _dn_step(*[t[ci * heads:(ci + 1) * heads] for t in parts], s)
            for h in range(heads):
                st_ref[ci, h] = s[h]
                o_ref[ci * DN_CHUNK:(ci + 1) * DN_CHUNK, h * DN_DK:(h + 1) * DN_DK] = o[h]
            s_scr[...] = s_next

    tok = pl.BlockSpec((rows, width), lambda i: (i, 0))
    return pl.pallas_call(
        body, name=name + "_fwd", grid=(n // cb,), in_specs=[tok] * 5,
        out_specs=[tok, pl.BlockSpec((cb, heads, DN_DK, DN_DV), lambda i: (i, 0, 0, 0))],
        out_shape=[jax.ShapeDtypeStruct((s_len, width), F32), jax.ShapeDtypeStruct((n, heads, DN_DK, DN_DV), F32)],
        scratch_shapes=[pltpu.VMEM((heads, DN_DK, DN_DV), F32)],
        compiler_params=pltpu.CompilerParams(dimension_semantics=("arbitrary",), vmem_limit_bytes=V7X_VMEM_LIMIT_BYTES),
    )(q, k, v, be, ge)


def _dn_backward(q, k, v, be, ge, states, do, name, chunks_per_step=4):
    s_len, width = q.shape
    heads = width // DN_DK
    n = s_len // DN_CHUNK
    cb = min(chunks_per_step, n)
    rows = cb * DN_CHUNK
    steps = n // cb

    def body(q_ref, k_ref, v_ref, be_ref, ge_ref, st_ref, do_ref, dq_ref, dk_ref, dv_ref, dbe_ref, dge_ref, ds_scr):
        @pl.when(pl.program_id(0) == 0)
        def _():
            ds_scr[...] = jnp.zeros_like(ds_scr)

        lanes = [slice(h * DN_DK, (h + 1) * DN_DK) for h in range(heads)]
        for ci in reversed(range(cb)):
            r = slice(ci * DN_CHUNK, (ci + 1) * DN_CHUNK)
            _, vjp = jax.vjp(_dn_chunk_heads, *[[ref[r, l] for l in lanes]
                                                for ref in (q_ref, k_ref, v_ref, be_ref, ge_ref)],
                             [st_ref[ci, h] for h in range(heads)])
            *grads, ds = vjp(([do_ref[r, l] for l in lanes], [ds_scr[h] for h in range(heads)]))
            for g_ref, g in zip((dq_ref, dk_ref, dv_ref, dbe_ref, dge_ref), grads):
                for l, g_head in zip(lanes, g):
                    g_ref[r, l] = g_head
            for h in range(heads):
                ds_scr[h] = ds[h]

    tok = pl.BlockSpec((rows, width), lambda i: (steps - 1 - i, 0))
    return pl.pallas_call(
        body, name=name + "_bwd", grid=(steps,),
        in_specs=[tok] * 5 + [pl.BlockSpec((cb, heads, DN_DK, DN_DV), lambda i: (steps - 1 - i, 0, 0, 0)), tok],
        out_specs=[tok] * 5,
        out_shape=[jax.ShapeDtypeStruct((s_len, width), F32)] * 5,
        scratch_shapes=[pltpu.VMEM((heads, DN_DK, DN_DV), F32)],
        compiler_params=pltpu.CompilerParams(dimension_semantics=("arbitrary",), vmem_limit_bytes=V7X_VMEM_LIMIT_BYTES),
    )(q, k, v, be, ge, states, do)


@functools.partial(jax.custom_vjp, nondiff_argnums=(5,))
def _dn_core(q, k, v, be, ge, name):
    return _dn_forward(q, k, v, be, ge, name)[0]


def _dn_core_fwd(q, k, v, be, ge, name):
    o, states = _dn_forward(q, k, v, be, ge, name)
    return o, (q, k, v, be, ge, states)


def _dn_core_bwd(name, res, do):
    return tuple(_dn_backward(*res, do, name))


_dn_core.defvjp(_dn_core_fwd, _dn_core_bwd)


def _layer_norm(x, g, b):
    mu = x.mean(-1, keepdims=True)
    var = jnp.square(x - mu).mean(-1, keepdims=True)
    return (x - mu) * lax.rsqrt(var + LN_EPS) * g + b


def _hybrid_layer(x, mem_n, p, p_bf16, l):
    tag = f"l{l}"
    ln_g, ln_b = p["ln_g"][l], p["ln_b"][l]

    def mm(a, name, tag_, *idx):
        return pmm(a, p[name][(l,) + idx], p_bf16[name][(l,) + idx], tag_)

    def swiglu(a, ffn):
        return mm(_ffn_act(mm(a, ffn + "_w_gu", f"{tag}_{ffn}_gu"), f"{tag}_{ffn}_act"), ffn + "_w_down",
                  f"{tag}_{ffn}_down")

    h = _res_ln(x, swiglu(x, "ffn1"), ln_g[0], ln_b[0], 0.5, tag + "_ln0")
    proj = mm(h, "w_in", tag + "_in")
    seg = {name: proj[:, lo:hi] for name, (lo, hi) in SEG.items()}
    q, k, v, be, ge = _dn_pre(seg["dn_qkv"], seg["dn_ba"], p["dn_conv_w"][l], p["dn_a_log"][l], p["dn_dt_bias"][l],
                              tag + "_dn_pre")
    o_dn = _dn_post(_dn_core(q, k, v, be, ge, tag + "_dn"), seg["dn_z"], p["dn_norm_w"][l], tag + "_dn_post")
    o_sw = _swa(seg["sw_q"], seg["sw_kv"], p["swa_sinks"][l], tag + "_swa")
    kv_mem = mm(mem_n, "w_mem_kv", tag + "_memkv")
    half = XA_HEADS * XA_DH
    o_xa = _xattn(seg["xa_q"], kv_mem[:, :half], kv_mem[:, half:], tag + "_xattn")
    branch = [mm(o, "w_branch", tag + f"_branch{i}", i) for i, o in enumerate((o_dn, o_sw, o_xa))]
    merged = _gate_merge(seg["gates"], *branch, tag + "_merge")
    h = _res_ln(h, mm(merged, "w_out", tag + "_out"), ln_g[1], ln_b[1], 1.0, tag + "_ln1")
    return _res_ln(h, swiglu(h, "ffn2"), ln_g[2], ln_b[2], 0.5, tag + "_ln2")


def _local_loss(p, x, mem, target, p_bf16):
    mem_n = _layer_norm(mem, p["mem_ln_g"], p["mem_ln_b"])
    y = x
    for l in range(DEPTH):
        y = _hybrid_layer(y, mem_n, p, p_bf16, l)
    return jnp.sum(_row_loss(y, target, "row_loss")[:, 0])


def _w_in_permute(w):
    pad = jnp.zeros(w.shape[:-1] + (D_IN_PAD - D_IN,), w.dtype)
    return jnp.concatenate([w[..., :1536], w[..., 1544:], w[..., 1536:1544], pad], axis=-1)


def _w_in_unpermute(w):
    return jnp.concatenate([w[..., :1536], w[..., 6400:6408], w[..., 1536:6400]], axis=-1)


def _join_shards(gathered, ax):
    shp = gathered.shape[1:]
    t = jnp.moveaxis(gathered, 0, ax)
    return t.reshape(shp[:ax] + (N_DEV * shp[ax],) + shp[ax + 1:])


def _split_shards(whole, ax):
    shp = whole.shape
    t = whole.reshape(shp[:ax] + (N_DEV, shp[ax] // N_DEV) + shp[ax + 1:])
    return jnp.moveaxis(t, ax, 0)


def _pack_small(parts):
    flat = jnp.concatenate([t.reshape(-1).astype(F32) for t in parts])
    rows = -(-flat.shape[0] // LANES)
    rows = -(-rows // 8) * 8
    return jnp.concatenate([flat, jnp.zeros((rows * LANES - flat.shape[0],), F32)]).reshape(rows, LANES)


def _unpack_small(packed, shapes):
    flat = packed.reshape(packed.shape[:-2] + (-1,))
    out, off = [], 0
    for shp in shapes:
        size = int(np.prod(shp))
        out.append(flat[..., off:off + size].reshape(flat.shape[:-1] + tuple(shp)))
        off += size
    return out


def kernel(x, mem, mem_ln_g, mem_ln_b, ln_g, ln_b, ffn1_w_gu, ffn1_w_down, w_in, dn_conv_w, dn_a_log, dn_dt_bias, dn_norm_w, swa_sinks, w_mem_kv, w_branch, w_out, ffn2_w_gu, ffn2_w_down, loss_target, m_mem_ln_g, m_mem_ln_b, m_ln_g, m_ln_b, m_ffn1_w_gu, m_ffn1_w_down, m_w_in, m_dn_conv_w, m_dn_a_log, m_dn_dt_bias, m_dn_norm_w, m_swa_sinks, m_w_mem_kv, m_w_branch, m_w_out, m_ffn2_w_gu, m_ffn2_w_down, v_mem_ln_g, v_mem_ln_b, v_ln_g, v_ln_b, v_ffn1_w_gu, v_ffn1_w_down, v_w_in, v_dn_conv_w, v_dn_a_log, v_dn_dt_bias, v_dn_norm_w, v_swa_sinks, v_w_mem_kv, v_w_branch, v_w_out, v_ffn2_w_gu, v_ffn2_w_down):
    given = dict(locals())
    w_loc = {n: given[n] for n in WEIGHTS}
    m_loc = {n: given["m_" + n] for n in WEIGHTS}
    v_loc = {n: given["v_" + n] for n in WEIGHTS}
    me = _my_index()

    params_bf16 = {n: _join_shards(_all_gather(w_loc[n].astype(BF16), "all_gather_" + n), BIG_AXIS[n]) for n in BIG}
    params_bf16["w_in"] = _w_in_permute(params_bf16["w_in"])
    params = {n: params_bf16[n].astype(F32) for n in BIG}
    small_shard_shapes = [w_loc[n].shape for n in SMALL_SHARDED]
    small_gathered = _small_all_gather(_pack_small([w_loc[n] for n in SMALL_SHARDED]), False, "all_gather_small")
    for n, t in zip(SMALL_SHARDED, _unpack_small(small_gathered, small_shard_shapes)):
        t = jnp.moveaxis(t, 0, -2)
        params[n] = t.reshape(t.shape[:-2] + (-1,))
    for n in SMALL_REPL:
        params[n] = w_loc[n]

    loss_local, (g_params, grad_x) = jax.value_and_grad(_local_loss, argnums=(0, 1))(
        params, x[0], mem[0], loss_target[0], params_bf16)
    grad_x = grad_x[None]
    g_params["w_in"] = _w_in_unpermute(g_params["w_in"])

    g_big = {n: _sum_slots(_exchange_slots(_split_shards(g_params[n], BIG_AXIS[n]).astype(BF16),
                                           "exchange_grad_" + n), "sum_grad_" + n) for n in BIG}
    small_names = SMALL_SHARDED + SMALL_REPL
    small_full_shapes = [g_params[n].shape for n in small_names] + [()]
    small_sum = _small_all_gather(_pack_small([g_params[n] for n in small_names] + [loss_local]), True, "all_sum_small")
    small_full = dict(zip(small_names + ("loss",), _unpack_small(small_sum, small_full_shapes)))
    loss = small_full["loss"]
    grads = dict(g_big)
    for n in SMALL_SHARDED:
        shard = w_loc[n].shape[-1]
        grads[n] = lax.dynamic_slice_in_dim(small_full[n], me * shard, shard, axis=small_full[n].ndim - 1)
    for n in SMALL_REPL:
        grads[n] = small_full[n]

    delta, new_m, new_v = {}, {}, {}
    for n in BIG:
        delta[n], new_m[n], new_v[n] = _adamw(w_loc[n], grads[n], m_loc[n], v_loc[n], "adamw_" + n)
    shapes = [w_loc[n].shape for n in small_names]
    packed = [_pack_small([src[n] for n in small_names]) for src in (w_loc, grads, m_loc, v_loc)]
    for dst, t in zip((delta, new_m, new_v), _adamw(*packed, "adamw_small")):
        dst.update(zip(small_names, _unpack_small(t, shapes)))

    return (loss, grad_x, *[grads[n] for n in WEIGHTS], *[delta[n] for n in WEIGHTS],
            *[new_m[n] for n in WEIGHTS], *[new_v[n] for n in WEIGHTS])
```

```python
import functools

import numpy as np
import jax
import jax.numpy as jnp
from jax import lax
from jax.experimental import pallas as pl
from jax.experimental.pallas import tpu as pltpu

F32 = jnp.float32
BF16 = jnp.bfloat16
MESH = pl.DeviceIdType.MESH
N_DEV = 8
V7X_VMEM_LIMIT_BYTES = 56 * 1024 * 1024
LANES = 128
PACK_W = 1024

D_MODEL = 1024
DEPTH = 2
DN_HEADS, DN_DK, DN_DV, DN_CONV, DN_CHUNK = 4, 128, 128, 4, 64
SWA_HEADS, SWA_KV_HEADS, SWA_DH, WINDOW = 8, 2, 64, 128
XA_HEADS, XA_DH = 4, 128
D_FF = 2816
N_BRANCH, BRANCH_W = 3, 512
DEEPNORM_ALPHA = (2 * DEPTH) ** 0.25
LN_EPS = 1e-5
RMS_EPS = 1e-6
NEG_INF = -1e30
D_IN = 6408
D_IN_PAD = 6528
SEG = dict(dn_qkv=(0, 1536), dn_z=(1536, 2048), sw_q=(2048, 2560), sw_kv=(2560, 2816), xa_q=(2816, 3328),
           gates=(3328, 6400), dn_ba=(6400, 6528))

ADAM_LR, ADAM_B1, ADAM_B2, ADAM_EPS, ADAM_WD, ADAM_STEP = 0.001, 0.9, 0.999, 1e-08, 0.01, 10

BIG = ("ffn1_w_gu", "ffn1_w_down", "w_in", "w_mem_kv", "w_branch", "w_out", "ffn2_w_gu", "ffn2_w_down")
BIG_AXIS = dict(ffn1_w_gu=2, ffn1_w_down=1, w_in=2, w_mem_kv=1, w_branch=3, w_out=1, ffn2_w_gu=2, ffn2_w_down=1)
SMALL_SHARDED = ("ln_g", "ln_b", "dn_conv_w")
SMALL_REPL = ("mem_ln_g", "mem_ln_b", "dn_a_log", "dn_dt_bias", "dn_norm_w", "swa_sinks")
WEIGHTS = ("mem_ln_g", "mem_ln_b", "ln_g", "ln_b", "ffn1_w_gu", "ffn1_w_down", "w_in", "dn_conv_w", "dn_a_log",
           "dn_dt_bias", "dn_norm_w", "swa_sinks", "w_mem_kv", "w_branch", "w_out", "ffn2_w_gu", "ffn2_w_down")

HI = lax.Precision.HIGHEST
NN = (((1,), (0,)), ((), ()))
NT = (((1,), (1,)), ((), ()))
TN = (((0,), (0,)), ((), ()))


def _tile(n, target, align):
    best = None
    for d in range(align, min(n, target) + 1, align):
        if n % d == 0:
            best = d
    return n if best is None else best


def _my_index():
    return 4 * lax.axis_index("x") + 2 * lax.axis_index("y") + lax.axis_index("c")


def _peer(k):
    x, y, c = lax.axis_index("x"), lax.axis_index("y"), lax.axis_index("c")
    px = 1 - x if (k >> 2) & 1 else x
    py = 1 - y if (k >> 1) & 1 else y
    pc = 1 - c if k & 1 else c
    return (px, py, pc), 4 * px + 2 * py + pc


def _matmul(a, b, *, ta=False, tb=False, name, tm=512, tn=2176, tk=2304):
    m, k = (a.shape[1], a.shape[0]) if ta else a.shape
    k2, n = (b.shape[1], b.shape[0]) if tb else b.shape
    assert k == k2, (a.shape, b.shape, ta, tb)
    tm = _tile(m, tm, LANES if ta else 8)
    tn = _tile(n, tn, LANES)
    tk = _tile(k, tk, LANES)
    nk = k // tk
    dims = (((0 if ta else 1,), (1 if tb else 0,)), ((), ()))

    def body(a_ref, b_ref, o_ref):
        kk = pl.program_id(2)
        p = lax.dot_general(a_ref[...].astype(BF16), b_ref[...].astype(BF16), dims, preferred_element_type=F32)

        @pl.when(kk == 0)
        def _():
            o_ref[...] = p

        @pl.when(kk != 0)
        def _():
            o_ref[...] += p

    a_spec = pl.BlockSpec((tk, tm), lambda j, i, kk: (kk, i)) if ta else pl.BlockSpec((tm, tk), lambda j, i, kk: (i, kk))
    b_spec = pl.BlockSpec((tn, tk), lambda j, i, kk: (j, kk)) if tb else pl.BlockSpec((tk, tn), lambda j, i, kk: (kk, j))
    return pl.pallas_call(
        body, name=name, grid=(n // tn, m // tm, nk), in_specs=[a_spec, b_spec],
        out_specs=pl.BlockSpec((tm, tn), lambda j, i, kk: (i, j)),
        out_shape=jax.ShapeDtypeStruct((m, n), F32),
        compiler_params=pltpu.CompilerParams(dimension_semantics=("parallel", "parallel", "arbitrary"),
                                             vmem_limit_bytes=V7X_VMEM_LIMIT_BYTES),
    )(a, b)


@functools.partial(jax.custom_vjp, nondiff_argnums=(3,))
def pmm(x, w, w_bf16, tag):
    return _matmul(x, w_bf16, name=tag + "_fwd", tm=1024)


def _pmm_fwd(x, w, w_bf16, tag):
    return _matmul(x, w_bf16, name=tag + "_fwd", tm=1024), (x, w_bf16)


def _pmm_bwd(tag, res, dy):
    x, w_bf16 = res
    dx = _matmul(dy, w_bf16, tb=True, name=tag + "_dx", tm=1024)
    dw = _matmul(x, dy, ta=True, name=tag + "_dw", tm=1408, tn=2176, tk=512)
    return dx, dw, jnp.zeros_like(w_bf16)


pmm.defvjp(_pmm_fwd, _pmm_bwd)


def _all_gather(block, name):
    def body(x_ref, out_ref, send_sems, recv_sems, local_sem):
        me = _my_index()
        mine = pltpu.make_async_copy(x_ref, out_ref.at[me], local_sem)
        mine.start()
        sends = []
        for k in range(1, N_DEV):
            peer, _ = _peer(k)
            cp = pltpu.make_async_remote_copy(src_ref=x_ref, dst_ref=out_ref.at[me], send_sem=send_sems.at[k - 1],
                                              recv_sem=recv_sems.at[k - 1], device_id=peer, device_id_type=MESH)
            cp.start()
            sends.append(cp)
        for k in range(1, N_DEV):
            peer, p = _peer(k)
            pltpu.make_async_remote_copy(src_ref=x_ref, dst_ref=out_ref.at[p], send_sem=send_sems.at[k - 1],
                                         recv_sem=recv_sems.at[k - 1], device_id=peer, device_id_type=MESH).wait_recv()
        for cp in sends:
            cp.wait_send()
        mine.wait()

    return pl.pallas_call(
        body, name=name,
        out_shape=jax.ShapeDtypeStruct((N_DEV,) + block.shape, block.dtype),
        in_specs=[pl.BlockSpec(memory_space=pl.ANY)], out_specs=pl.BlockSpec(memory_space=pl.ANY),
        scratch_shapes=[pltpu.SemaphoreType.DMA((N_DEV - 1,)), pltpu.SemaphoreType.DMA((N_DEV - 1,)),
                        pltpu.SemaphoreType.DMA],
    )(block)


def _exchange_slots(slots, name):
    def body(x_ref, out_ref, send_sems, recv_sems, local_sem):
        me = _my_index()
        mine = pltpu.make_async_copy(x_ref.at[me], out_ref.at[me], local_sem)
        mine.start()
        sends = []
        for k in range(1, N_DEV):
            peer, p = _peer(k)
            cp = pltpu.make_async_remote_copy(src_ref=x_ref.at[p], dst_ref=out_ref.at[me], send_sem=send_sems.at[k - 1],
                                              recv_sem=recv_sems.at[k - 1], device_id=peer, device_id_type=MESH)
            cp.start()
            sends.append(cp)
        for k in range(1, N_DEV):
            peer, p = _peer(k)
            pltpu.make_async_remote_copy(src_ref=x_ref.at[p], dst_ref=out_ref.at[p], send_sem=send_sems.at[k - 1],
                                         recv_sem=recv_sems.at[k - 1], device_id=peer, device_id_type=MESH).wait_recv()
        for cp in sends:
            cp.wait_send()
        mine.wait()

    return pl.pallas_call(
        body, name=name,
        out_shape=jax.ShapeDtypeStruct(slots.shape, slots.dtype),
        in_specs=[pl.BlockSpec(memory_space=pl.ANY)], out_specs=pl.BlockSpec(memory_space=pl.ANY),
        scratch_shapes=[pltpu.SemaphoreType.DMA((N_DEV - 1,)), pltpu.SemaphoreType.DMA((N_DEV - 1,)),
                        pltpu.SemaphoreType.DMA],
    )(slots)


def _sum_slots(slots, name):
    shape = slots.shape[1:]
    w = shape[-1]
    slots = slots.reshape(N_DEV, -1, w)
    r = slots.shape[1]
    tr = _tile(r, 256, 8)

    def body(x_ref, o_ref):
        acc = x_ref[0].astype(F32)
        for p in range(1, N_DEV):
            acc = acc + x_ref[p].astype(F32)
        o_ref[...] = acc

    return pl.pallas_call(
        body, name=name, grid=(r // tr,),
        in_specs=[pl.BlockSpec((N_DEV, tr, w), lambda i: (0, i, 0))],
        out_specs=pl.BlockSpec((tr, w), lambda i: (i, 0)),
        out_shape=jax.ShapeDtypeStruct((r, w), F32),
        compiler_params=pltpu.CompilerParams(dimension_semantics=("parallel",), vmem_limit_bytes=V7X_VMEM_LIMIT_BYTES),
    )(slots).reshape(shape)


def _small_all_gather(v, reduce, name):
    rows, w = v.shape

    def body(x_ref, out_ref, land_ref, send_sems, recv_sems):
        me = _my_index()
        sends = []
        for k in range(1, N_DEV):
            peer, _ = _peer(k)
            cp = pltpu.make_async_remote_copy(src_ref=x_ref, dst_ref=land_ref.at[me], send_sem=send_sems.at[k - 1],
                                              recv_sem=recv_sems.at[k - 1], device_id=peer, device_id_type=MESH)
            cp.start()
            sends.append(cp)
        land_ref[me] = x_ref[...]
        for k in range(1, N_DEV):
            peer, p = _peer(k)
            pltpu.make_async_remote_copy(src_ref=x_ref, dst_ref=land_ref.at[p], send_sem=send_sems.at[k - 1],
                                         recv_sem=recv_sems.at[k - 1], device_id=peer, device_id_type=MESH).wait_recv()
        for cp in sends:
            cp.wait_send()
        if reduce:
            acc = land_ref[0]
            for p in range(1, N_DEV):
                acc = acc + land_ref[p]
            out_ref[...] = acc
        else:
            out_ref[...] = land_ref[...]

    out_shape = (rows, w) if reduce else (N_DEV, rows, w)
    return pl.pallas_call(
        body, name=name,
        out_shape=jax.ShapeDtypeStruct(out_shape, F32),
        in_specs=[pl.BlockSpec(memory_space=pltpu.VMEM)], out_specs=pl.BlockSpec(memory_space=pltpu.VMEM),
        scratch_shapes=[pltpu.VMEM((N_DEV, rows, w), F32), pltpu.SemaphoreType.DMA((N_DEV - 1,)),
                        pltpu.SemaphoreType.DMA((N_DEV - 1,))],
    )(v)


def _adamw(w, g, m, v, name):
    shape = w.shape
    cols = shape[-1] if len(shape) > 1 else shape[0]
    w2, g2, m2, v2 = (t.reshape(-1, cols) for t in (w, g, m, v))
    rows = w2.shape[0]
    tr = _tile(rows, 256, 8)

    def body(w_ref, g_ref, m_ref, v_ref, d_ref, mo_ref, vo_ref):
        gv = g_ref[...]
        mn = ADAM_B1 * m_ref[...] + (1.0 - ADAM_B1) * gv
        vn = ADAM_B2 * v_ref[...] + (1.0 - ADAM_B2) * (gv * gv)
        m_hat = mn / (1.0 - ADAM_B1 ** ADAM_STEP)
        v_hat = vn / (1.0 - ADAM_B2 ** ADAM_STEP)
        d_ref[...] = -ADAM_LR * (m_hat / (jnp.sqrt(v_hat) + ADAM_EPS) + ADAM_WD * w_ref[...])
        mo_ref[...] = mn
        vo_ref[...] = vn

    spec = pl.BlockSpec((tr, cols), lambda i: (i, 0))
    outs = pl.pallas_call(
        body, name=name, grid=(rows // tr,), in_specs=[spec] * 4, out_specs=[spec] * 3,
        out_shape=[jax.ShapeDtypeStruct((rows, cols), F32)] * 3,
        compiler_params=pltpu.CompilerParams(dimension_semantics=("parallel",), vmem_limit_bytes=V7X_VMEM_LIMIT_BYTES),
    )(w2, g2, m2, v2)
    return tuple(o.reshape(shape) for o in outs)


_DOT_DIMS = {2: dict(nn=NN, nt=NT, tn=TN),
             3: dict(nn=(((2,), (1,)), ((0,), (0,))), nt=(((2,), (2,)), ((0,), (0,))), tn=(((1,), (1,)), ((0,), (0,))))}


def _bf16_dot(a, b, kind):
    if a.ndim == 3:
        return jnp.stack([_bf16_dot(a[i], b[i] if b.ndim == 3 else b, kind) for i in range(a.shape[0])])
    return lax.dot_general(a.astype(BF16), b.astype(BF16), _DOT_DIMS[2][kind], preferred_element_type=F32)


def _f32_dot(a, b, kind):
    if a.ndim == 3:
        return jnp.stack([_f32_dot(a[i], b[i] if b.ndim == 3 else b, kind) for i in range(a.shape[0])])
    return lax.dot_general(a, b, _DOT_DIMS[2][kind], precision=HI, preferred_element_type=F32)


def _dot_family(raw):
    @jax.custom_vjp
    def nn(a, b):
        return raw(a, b, "nn")

    @jax.custom_vjp
    def nt(a, b):
        return raw(a, b, "nt")

    @jax.custom_vjp
    def tn(a, b):
        return raw(a, b, "tn")

    nn.defvjp(lambda a, b: (raw(a, b, "nn"), (a, b)), lambda r, ct: (nt(ct, r[1]), tn(r[0], ct)))
    nt.defvjp(lambda a, b: (raw(a, b, "nt"), (a, b)), lambda r, ct: (nn(ct, r[1]), tn(ct, r[0])))
    tn.defvjp(lambda a, b: (raw(a, b, "tn"), (a, b)), lambda r, ct: (nt(r[1], ct), nn(r[0], ct)))
    return nn, nt, tn


_bdot_nn, _bdot_nt, _bdot_tn = _dot_family(_bf16_dot)
_hdot_nn, _hdot_nt, _hdot_tn = _dot_family(_f32_dot)


def _f32x3_dot(a, b, kind):
    if a.ndim == 3:
        return jnp.stack([_f32x3_dot(a[i], b[i], kind) for i in range(a.shape[0])])
    return lax.dot_general(a, b, _DOT_DIMS[2][kind], precision=lax.Precision.HIGH, preferred_element_type=F32)


_mdot_nn, _mdot_nt, _mdot_tn = _dot_family(_f32x3_dot)


def _rows(width, t):
    return (t, width), lambda i: (i, 0)


def _whole(shape):
    return tuple(shape), lambda i: (0,) * len(shape)


def _tiled_op(name, fn, steps, in_blocks, kinds, outs, grad_blocks=None, grad_fix=None):
    n_in, n_out = len(in_blocks), len(outs)
    diff = [j for j, kd in enumerate(kinds) if kd != "const"]
    has_param = any(kd == "param" for kd in kinds)
    in_specs = [pl.BlockSpec(b, m) for b, m in in_blocks]
    out_specs = [pl.BlockSpec(b, m) for _, b, m in outs]

    def forward(*arrays):
        def body(*refs):
            res = fn(pl.program_id(0), *[r[...] for r in refs[:n_in]])
            for o_ref, val in zip(refs[n_in:], res):
                o_ref[...] = val

        return pl.pallas_call(
            body, name=name + "_fwd", grid=(steps,), in_specs=in_specs, out_specs=out_specs,
            out_shape=[jax.ShapeDtypeStruct(s, F32) for s, _, _ in outs],
            compiler_params=pltpu.CompilerParams(dimension_semantics=("parallel",),
                                                 vmem_limit_bytes=V7X_VMEM_LIMIT_BYTES),
        )(*arrays)

    def backward(arrays, douts):
        g_specs = []
        for j in diff:
            if kinds[j] == "param":
                g_specs.append((arrays[j].shape,) + _whole(arrays[j].shape))
            else:
                g_specs.append((grad_blocks or {}).get(j, (arrays[j].shape,) + in_blocks[j]))

        def body(*refs):
            i = pl.program_id(0)
            vals = [r[...] for r in refs[:n_in]]
            dos = tuple(r[...] for r in refs[n_in:n_in + n_out])
            g_refs = refs[n_in + n_out:]

            def of_diff(*dv):
                full = list(vals)
                for j, val in zip(diff, dv):
                    full[j] = val
                return tuple(fn(i, *full))

            _, vjp = jax.vjp(of_diff, *[vals[j] for j in diff])
            grads = vjp(dos)
            for j, g_ref, g in zip(diff, g_refs, grads):
                if kinds[j] == "param":
                    @pl.when(i == 0)
                    def _(g_ref=g_ref, g=g):
                        g_ref[...] = g

                    @pl.when(i != 0)
                    def _(g_ref=g_ref, g=g):
                        g_ref[...] += g
                else:
                    g_ref[...] = g

        return pl.pallas_call(
            body, name=name + "_bwd", grid=(steps,), in_specs=in_specs + out_specs,
            out_specs=[pl.BlockSpec(b, m) for _, b, m in g_specs],
            out_shape=[jax.ShapeDtypeStruct(shp, F32) for shp, _, _ in g_specs],
            compiler_params=pltpu.CompilerParams(dimension_semantics=("arbitrary" if has_param else "parallel",),
                                                 vmem_limit_bytes=V7X_VMEM_LIMIT_BYTES),
        )(*arrays, *douts)

    @jax.custom_vjp
    def op(*arrays):
        return tuple(forward(*arrays))

    def op_fwd(*arrays):
        return tuple(forward(*arrays)), arrays

    def op_bwd(arrays, douts):
        grads = backward(arrays, douts)
        full = [jnp.zeros_like(a) for a in arrays]
        for j, g in zip(diff, grads):
            full[j] = grad_fix[j](g) if grad_fix and j in grad_fix else g
        return tuple(full)

    op.defvjp(op_fwd, op_bwd)
    return op


def _sigmoid(x):
    return jax.nn.sigmoid(x)


def _silu(x):
    return x * jax.nn.sigmoid(x)


def _softplus(x):
    return jnp.maximum(x, 0.0) + jnp.log(1.0 + jnp.exp(-jnp.abs(x)))


def _heads(t, width):
    return [t[:, h * width:(h + 1) * width] for h in range(t.shape[1] // width)]


def _ffn_act(gu, name):
    s, two_f = gu.shape
    f = two_f // 2
    t = _tile(s, 128, 8)

    def fn(i, gu_t):
        return (_silu(gu_t[:, :f]) * gu_t[:, f:],)

    return _tiled_op(name, fn, s // t, [_rows(two_f, t)], ["row"], [((s, f),) + _rows(f, t)])(gu)[0]


def _res_ln(x, f, g, b, scale, name):
    s, d = x.shape
    t = _tile(s, 512, 8)

    def fn(i, x_t, f_t, g_t, b_t):
        r = DEEPNORM_ALPHA * x_t + scale * f_t
        mu = jnp.mean(r, axis=-1, keepdims=True)
        var = jnp.mean(jnp.square(r - mu), axis=-1, keepdims=True)
        return ((r - mu) * lax.rsqrt(var + LN_EPS) * g_t + b_t,)

    op = _tiled_op(name, fn, s // t, [_rows(d, t), _rows(d, t), _whole((1, d)), _whole((1, d))],
                   ["row", "row", "param", "param"], [((s, d),) + _rows(d, t)])
    return op(x, f, g.reshape(1, d), b.reshape(1, d))[0]


def _row_loss(y, target, name):
    s, d = y.shape
    t = _tile(s, 512, 8)

    def fn(i, y_t, t_t):
        e = y_t - t_t
        return (jnp.broadcast_to(0.5 * jnp.mean(e * e, axis=-1, keepdims=True), (t, LANES)),)

    return _tiled_op(name, fn, s // t, [_rows(d, t), _rows(d, t)], ["row", "const"],
                     [((s, LANES),) + _rows(LANES, t)])(y, target)[0]


def _gate_merge(gates, b0, b1, b2, name):
    s, d = b0.shape
    t = _tile(s, 256, 8)

    def fn(i, g_t, b0_t, b1_t, b2_t):
        return (_sigmoid(g_t[:, :d]) * b0_t + _sigmoid(g_t[:, d:2 * d]) * b1_t + _sigmoid(g_t[:, 2 * d:]) * b2_t,)

    op = _tiled_op(name, fn, s // t, [_rows(3 * d, t)] + [_rows(d, t)] * 3, ["row"] * 4, [((s, d),) + _rows(d, t)])
    return op(gates, b0, b1, b2)[0]


def _dn_pre(x, ba, conv_w, a_log, dt_bias, name):
    s, xw = x.shape
    w = DN_HEADS * DN_DK
    t = _tile(s, 256, 8)
    steps = s // t
    lane_pad = jnp.zeros((LANES - 2 * DN_HEADS,), F32)
    pa = jnp.concatenate([jnp.zeros((DN_HEADS,), F32), a_log, lane_pad]).reshape(1, LANES)
    pd = jnp.concatenate([jnp.zeros((DN_HEADS,), F32), dt_bias, lane_pad]).reshape(1, LANES)
    cw = jnp.concatenate([conv_w, jnp.zeros((8 - DN_CONV, xw), F32)], axis=0)
    taps = [jnp.pad(x, ((DN_CONV - 1 - j, 0), (0, 0)))[:s] for j in range(DN_CONV - 1)] + [x]

    def unit(x):
        return x * lax.rsqrt(jnp.sum(x * x, axis=-1, keepdims=True) + RMS_EPS)

    def fn(i, x0_t, x1_t, x2_t, x3_t, ba_t, cw_t, pa_t, pd_t):
        tap = lax.broadcasted_iota(jnp.int32, cw_t.shape, 0)
        c_t = sum(x_t * jnp.sum(jnp.where(tap == j, cw_t, 0.0), axis=0, keepdims=True)
                  for j, x_t in enumerate((x0_t, x1_t, x2_t, x3_t)))
        act = _silu(c_t)
        q = jnp.concatenate([unit(x) for x in _heads(act[:, :w], DN_DK)], axis=1)
        k = jnp.concatenate([unit(x) for x in _heads(act[:, w:2 * w], DN_DK)], axis=1)
        beta = _sigmoid(ba_t)
        g = -jnp.exp(pa_t) * _softplus(ba_t + pd_t)
        src = lax.broadcasted_iota(jnp.int32, (LANES, w), 0)
        head = lax.broadcasted_iota(jnp.int32, (LANES, w), 1) >> int(np.log2(DN_DK))
        be = _hdot_nn(beta, (src == head).astype(F32))
        ge = _hdot_nn(g, (src == head + DN_HEADS).astype(F32))
        return q, k, act[:, 2 * w:], be, ge

    op = _tiled_op(name, fn, steps,
                   [_rows(xw, t)] * DN_CONV + [_rows(LANES, t), _whole((8, xw)), _whole((1, LANES)), _whole((1, LANES))],
                   ["row"] * (DN_CONV + 1) + ["param"] * 3, [((s, w),) + _rows(w, t)] * 5)
    return op(*taps, ba, cw, pa, pd)


def _dn_post(o, z, norm_w, name):
    s, w = o.shape
    t = _tile(s, 512, 8)

    def fn(i, o_t, z_t, nw_t):
        outs = [x * lax.rsqrt(jnp.mean(x * x, axis=-1, keepdims=True) + RMS_EPS) * nw_t * _silu(zz)
                for x, zz in zip(_heads(o_t, DN_DV), _heads(z_t, DN_DV))]
        return (jnp.concatenate(outs, axis=1),)

    op = _tiled_op(name, fn, s // t, [_rows(w, t), _rows(w, t), _whole((1, DN_DV))], ["row", "row", "param"],
                   [((s, w),) + _rows(w, t)])
    return op(o, z, norm_w.reshape(1, DN_DV))[0]


def _xattn(q, k_mem, v_mem, name):
    s, w = q.shape
    t = _tile(s, 512, 8)

    def fn(i, q_t, k_t, v_t):
        outs = []
        for qh, kh, vh in zip(_heads(q_t, XA_DH), _heads(k_t, XA_DH), _heads(v_t, XA_DH)):
            sc = _bdot_nt(qh, kh) * (XA_DH ** -0.5)
            e = jnp.exp(sc - jnp.max(sc, axis=-1, keepdims=True))
            outs.append(_bdot_nn(e / jnp.sum(e, axis=-1, keepdims=True), vh))
        return (jnp.concatenate(outs, axis=1),)

    op = _tiled_op(name, fn, s // t, [_rows(w, t), _whole(k_mem.shape), _whole(v_mem.shape)],
                   ["row", "param", "param"], [((s, w),) + _rows(w, t)])
    return op(q, k_mem, v_mem)[0]


def _swa(q, kv, sinks, name):
    s, w = q.shape
    wd = WINDOW
    grp = SWA_HEADS // SWA_KV_HEADS
    dh = SWA_DH
    steps = s // wd
    sink_rows = jnp.broadcast_to(sinks[:, None], (SWA_HEADS, LANES))

    def fn(i, q_t, kc_t, kp_t, sk_t):
        qr = lax.broadcasted_iota(jnp.int32, (grp * wd, 2 * wd), 0) & (wd - 1)
        kc = lax.broadcasted_iota(jnp.int32, (grp * wd, 2 * wd), 1)
        dist = qr + wd - kc
        mask = (dist >= 0) & (dist < wd) & ((kc >= wd) | (i > 0))
        head_row = lax.broadcasted_iota(jnp.int32, (SWA_HEADS, LANES), 0)
        outs = [None] * SWA_HEADS
        for hk in range(SWA_KV_HEADS):
            kh = jnp.concatenate([kp_t[:, hk * dh:(hk + 1) * dh], kc_t[:, hk * dh:(hk + 1) * dh]], axis=0)
            vh = jnp.concatenate([kp_t[:, (SWA_KV_HEADS + hk) * dh:(SWA_KV_HEADS + hk + 1) * dh],
                                  kc_t[:, (SWA_KV_HEADS + hk) * dh:(SWA_KV_HEADS + hk + 1) * dh]], axis=0)
            qg = jnp.concatenate([q_t[:, (hk * grp + g) * dh:(hk * grp + g + 1) * dh] for g in range(grp)], axis=0)
            sink = jnp.concatenate(
                [jnp.broadcast_to(jnp.sum(jnp.where(head_row == hk * grp + g, sk_t, 0.0), axis=0, keepdims=True),
                                  (wd, LANES)) for g in range(grp)], axis=0)
            sink = jnp.max(sink, axis=-1, keepdims=True)
            sc = jnp.where(mask, _bdot_nt(qg, kh) * (dh ** -0.5), NEG_INF)
            m = jnp.maximum(jnp.max(sc, axis=-1, keepdims=True), sink)
            p = jnp.exp(sc - m)
            p = p / (jnp.sum(p, axis=-1, keepdims=True) + jnp.exp(sink - m))
            o = _bdot_nn(p, vh)
            for g in range(grp):
                outs[hk * grp + g] = o[g * wd:(g + 1) * wd]
        return (jnp.concatenate(outs, axis=1),)

    kvw = kv.shape[1]
    prev_block = ((wd, kvw), lambda i: (jnp.maximum(i - 1, 0), 0))
    shift = lambda g: jnp.concatenate([g[wd:], jnp.zeros((wd, kvw), F32)], axis=0)
    op = _tiled_op(name, fn, steps, [_rows(w, wd), _rows(kvw, wd), prev_block, _whole((SWA_HEADS, LANES))],
                   ["row", "row", "row", "param"], [((s, w),) + _rows(w, wd)],
                   grad_blocks={2: (kv.shape,) + _rows(kvw, wd)}, grad_fix={2: shift})
    return op(q, kv, kv, sink_rows)[0]


def _dn_prepare(q, k, v, be, ge):
    b, c, dk = q.shape
    row = lax.broadcasted_iota(jnp.int32, (c, c), 0)
    col = lax.broadcasted_iota(jnp.int32, (c, c), 1)
    tril = row >= col
    gc = _hdot_nn(jnp.broadcast_to(tril.astype(F32), (b, c, c)), ge)
    g_col = _hdot_nn(gc, jnp.full((b, LANES, c), 1.0 / LANES, F32))
    g_row = _hdot_nt(jnp.full((b, c, LANES), 1.0 / LANES, F32), gc)
    decay = jnp.where(tril, jnp.exp(jnp.where(tril, g_col - g_row, 0.0)), 0.0)
    kb = k * be
    a = jnp.where(row > col, _bdot_nt(kb, k) * decay, 0.0)
    x = jnp.broadcast_to((row == col).astype(F32), (b, c, c))
    for lvl in range(int(np.log2(c))):
        off = ((row >> lvl) == (col >> lvl) + 1) & ((row >> (lvl + 1)) == (col >> (lvl + 1)))
        x = x - _mdot_nn(x, _mdot_nn(jnp.where(off, a, 0.0), x))
    eg = jnp.exp(gc)
    qs = q * (dk ** -0.5)
    qk = jnp.where(tril, _bdot_nt(qs, k) * decay, 0.0)
    gl_c = _hdot_nn(jnp.ones((b, c, c), F32), ge)
    gl_s = _hdot_nn(jnp.ones((b, dk, c), F32), ge)
    return _mdot_nn(x, v * be), _mdot_nn(x, kb * eg), qs * eg, k * jnp.exp(gl_c - gc), qk, jnp.exp(gl_s)


def _dn_stack(ref, chunks, heads):
    return jnp.stack([ref[ci * DN_CHUNK:(ci + 1) * DN_CHUNK, h * DN_DK:(h + 1) * DN_DK]
                      for ci in range(chunks) for h in range(heads)])


def _dn_step(u, w, q_dec, k_tail, qk, decay, s):
    v_new = u - _bdot_nn(w, s)
    o = _bdot_nn(q_dec, s) + _bdot_nn(qk, v_new)
    return o, s * decay + _bdot_tn(k_tail, v_new)


def _hdot(a, b, dims=NN):
    return lax.dot_general(a, b, dims, precision=HI, preferred_element_type=F32)


def _mdot(a, b, dims=NN):
    return lax.dot_general(a, b, dims, precision=lax.Precision.HIGH, preferred_element_type=F32)


def _dn_chunk_heads(q, k, v, be, ge, s):
    c, dk = q[0].shape
    row = lax.broadcasted_iota(jnp.int32, (c, c), 0)
    col = lax.broadcasted_iota(jnp.int32, (c, c), 1)
    tril = row >= col
    trilf = tril.astype(F32)
    mean_rows = jnp.full((c, LANES), 1.0 / LANES, F32)
    gc = [_hdot(trilf, g) for g in ge]
    g_row = [_hdot(mean_rows, g, NT) for g in gc]
    decay = [jnp.where(tril, jnp.exp(jnp.where(tril, g[:, :c] - gr, 0.0)), 0.0) for g, gr in zip(gc, g_row)]
    kb = [ki * bi for ki, bi in zip(k, be)]
    a = [jnp.where(row > col, _bdot_nt(x_, y_) * d, 0.0) for x_, y_, d in zip(kb, k, decay)]
    x = [(row == col).astype(F32)] * len(q)
    for lvl in range(int(np.log2(c))):
        off = ((row >> lvl) == (col >> lvl) + 1) & ((row >> (lvl + 1)) == (col >> (lvl + 1)))
        t = [_mdot(jnp.where(off, ai, 0.0), xi) for ai, xi in zip(a, x)]
        x = [xi - _mdot(xi, ti) for xi, ti in zip(x, t)]
    eg = [jnp.exp(g) for g in gc]
    u = [_mdot(xi, vi * bi) for xi, vi, bi in zip(x, v, be)]
    w = [_mdot(xi, ki * ei) for xi, ki, ei in zip(x, kb, eg)]
    qs = [qi * (dk ** -0.5) for qi in q]
    qk = [jnp.where(tril, _bdot_nt(x_, y_) * d, 0.0) for x_, y_, d in zip(qs, k, decay)]
    gl = [jnp.sum(g, axis=0, keepdims=True) for g in ge]
    v_new = [ui - _bdot_nn(wi, si) for ui, wi, si in zip(u, w, s)]
    o = [_bdot_nn(qi * ei, si) + _bdot_nn(ai, vi) for qi, ei, si, ai, vi in zip(qs, eg, s, qk, v_new)]
    s_next = [si * jnp.exp(g) + _bdot_tn(ki * jnp.exp(g - gi), vi)
              for si, g, ki, gi, vi in zip(s, gl, k, gc, v_new)]
    return o, s_next


def _dn_forward(q, k, v, be, ge, name, chunks_per_step=4):
    s_len, width = q.shape
    heads = width // DN_DK
    n = s_len // DN_CHUNK
    cb = min(chunks_per_step, n)
    rows = cb * DN_CHUNK

    def body(q_ref, k_ref, v_ref, be_ref, ge_ref, o_ref, st_ref, s_scr):
        @pl.when(pl.program_id(0) == 0)
        def _():
            s_scr[...] = jnp.zeros_like(s_scr)

        parts = _dn_prepare(*[_dn_stack(ref, cb, heads) for ref in (q_ref, k_ref, v_ref, be_ref, ge_ref)])
        for ci in range(cb):
            s = s_scr[...]
            o, s_next = _dn_step(*[t[ci * heads:(ci + 1) * heads] for t in parts], s)
            for h in range(heads):
                st_ref[ci, h] = s[h]
                o_ref[ci * DN_CHUNK:(ci + 1) * DN_CHUNK, h * DN_DK:(h + 1) * DN_DK] = o[h]
            s_scr[...] = s_next

    tok = pl.BlockSpec((rows, width), lambda i: (i, 0))
    return pl.pallas_call(
        body, name=name + "_fwd", grid=(n // cb,), in_specs=[tok] * 5,
        out_specs=[tok, pl.BlockSpec((cb, heads, DN_DK, DN_DV), lambda i: (i, 0, 0, 0))],
        out_shape=[jax.ShapeDtypeStruct((s_len, width), F32), jax.ShapeDtypeStruct((n, heads, DN_DK, DN_DV), F32)],
        scratch_shapes=[pltpu.VMEM((heads, DN_DK, DN_DV), F32)],
        compiler_params=pltpu.CompilerParams(dimension_semantics=("arbitrary",), vmem_limit_bytes=V7X_VMEM_LIMIT_BYTES),
    )(q, k, v, be, ge)


def _dn_backward(q, k, v, be, ge, states, do, name, chunks_per_step=4):
    s_len, width = q.shape
    heads = width // DN_DK
    n = s_len // DN_CHUNK
    cb = min(chunks_per_step, n)
    rows = cb * DN_CHUNK
    steps = n // cb

    def body(q_ref, k_ref, v_ref, be_ref, ge_ref, st_ref, do_ref, dq_ref, dk_ref, dv_ref, dbe_ref, dge_ref, ds_scr):
        @pl.when(pl.program_id(0) == 0)
        def _():
            ds_scr[...] = jnp.zeros_like(ds_scr)

        lanes = [slice(h * DN_DK, (h + 1) * DN_DK) for h in range(heads)]
        for ci in reversed(range(cb)):
            r = slice(ci * DN_CHUNK, (ci + 1) * DN_CHUNK)
            _, vjp = jax.vjp(_dn_chunk_heads, *[[ref[r, l] for l in lanes]
                                                for ref in (q_ref, k_ref, v_ref, be_ref, ge_ref)],
                             [st_ref[ci, h] for h in range(heads)])
            *grads, ds = vjp(([do_ref[r, l] for l in lanes], [ds_scr[h] for h in range(heads)]))
            for g_ref, g in zip((dq_ref, dk_ref, dv_ref, dbe_ref, dge_ref), grads):
                for l, g_head in zip(lanes, g):
                    g_ref[r, l] = g_head
            for h in range(heads):
                ds_scr[h] = ds[h]

    tok = pl.BlockSpec((rows, width), lambda i: (steps - 1 - i, 0))
    return pl.pallas_call(
        body, name=name + "_bwd", grid=(steps,),
        in_specs=[tok] * 5 + [pl.BlockSpec((cb, heads, DN_DK, DN_DV), lambda i: (steps - 1 - i, 0, 0, 0)), tok],
        out_specs=[tok] * 5,
        out_shape=[jax.ShapeDtypeStruct((s_len, width), F32)] * 5,
        scratch_shapes=[pltpu.VMEM((heads, DN_DK, DN_DV), F32)],
        compiler_params=pltpu.CompilerParams(dimension_semantics=("arbitrary",), vmem_limit_bytes=V7X_VMEM_LIMIT_BYTES),
    )(q, k, v, be, ge, states, do)


@functools.partial(jax.custom_vjp, nondiff_argnums=(5,))
def _dn_core(q, k, v, be, ge, name):
    return _dn_forward(q, k, v, be, ge, name)[0]


def _dn_core_fwd(q, k, v, be, ge, name):
    o, states = _dn_forward(q, k, v, be, ge, name)
    return o, (q, k, v, be, ge, states)


def _dn_core_bwd(name, res, do):
    return tuple(_dn_backward(*res, do, name))


_dn_core.defvjp(_dn_core_fwd, _dn_core_bwd)


def _layer_norm(x, g, b):
    mu = x.mean(-1, keepdims=True)
    var = jnp.square(x - mu).mean(-1, keepdims=True)
    return (x - mu) * lax.rsqrt(var + LN_EPS) * g + b


def _hybrid_layer(x, mem_n, p, p_bf16, l):
    tag = f"l{l}"
    ln_g, ln_b = p["ln_g"][l], p["ln_b"][l]

    def mm(a, name, tag_, *idx):
        return pmm(a, p[name][(l,) + idx], p_bf16[name][(l,) + idx], tag_)

    def swiglu(a, ffn):
        return mm(_ffn_act(mm(a, ffn + "_w_gu", f"{tag}_{ffn}_gu"), f"{tag}_{ffn}_act"), ffn + "_w_down",
                  f"{tag}_{ffn}_down")

    h = _res_ln(x, swiglu(x, "ffn1"), ln_g[0], ln_b[0], 0.5, tag + "_ln0")
    proj = mm(h, "w_in", tag + "_in")
    seg = {name: proj[:, lo:hi] for name, (lo, hi) in SEG.items()}
    q, k, v, be, ge = _dn_pre(seg["dn_qkv"], seg["dn_ba"], p["dn_conv_w"][l], p["dn_a_log"][l], p["dn_dt_bias"][l],
                              tag + "_dn_pre")
    o_dn = _dn_post(_dn_core(q, k, v, be, ge, tag + "_dn"), seg["dn_z"], p["dn_norm_w"][l], tag + "_dn_post")
    o_sw = _swa(seg["sw_q"], seg["sw_kv"], p["swa_sinks"][l], tag + "_swa")
    kv_mem = mm(mem_n, "w_mem_kv", tag + "_memkv")
    half = XA_HEADS * XA_DH
    o_xa = _xattn(seg["xa_q"], kv_mem[:, :half], kv_mem[:, half:], tag + "_xattn")
    branch = [mm(o, "w_branch", tag + f"_branch{i}", i) for i, o in enumerate((o_dn, o_sw, o_xa))]
    merged = _gate_merge(seg["gates"], *branch, tag + "_merge")
    h = _res_ln(h, mm(merged, "w_out", tag + "_out"), ln_g[1], ln_b[1], 1.0, tag + "_ln1")
    return _res_ln(h, swiglu(h, "ffn2"), ln_g[2], ln_b[2], 0.5, tag + "_ln2")


def _local_loss(p, x, mem, target, p_bf16):
    mem_n = _layer_norm(mem, p["mem_ln_g"], p["mem_ln_b"])
    y = x
    for l in range(DEPTH):
        y = _hybrid_layer(y, mem_n, p, p_bf16, l)
    return jnp.sum(_row_loss(y, target, "row_loss")[:, 0])


def _w_in_permute(w):
    pad = jnp.zeros(w.shape[:-1] + (D_IN_PAD - D_IN,), w.dtype)
    return jnp.concatenate([w[..., :1536], w[..., 1544:], w[..., 1536:1544], pad], axis=-1)


def _w_in_unpermute(w):
    return jnp.concatenate([w[..., :1536], w[..., 6400:6408], w[..., 1536:6400]], axis=-1)


def _join_shards(gathered, ax):
    shp = gathered.shape[1:]
    t = jnp.moveaxis(gathered, 0, ax)
    return t.reshape(shp[:ax] + (N_DEV * shp[ax],) + shp[ax + 1:])


def _split_shards(whole, ax):
    shp = whole.shape
    t = whole.reshape(shp[:ax] + (N_DEV, shp[ax] // N_DEV) + shp[ax + 1:])
    return jnp.moveaxis(t, ax, 0)


def _pack_small(parts):
    flat = jnp.concatenate([t.reshape(-1).astype(F32) for t in parts])
    rows = -(-flat.shape[0] // LANES)
    rows = -(-rows // 8) * 8
    return jnp.concatenate([flat, jnp.zeros((rows * LANES - flat.shape[0],), F32)]).reshape(rows, LANES)


def _unpack_small(packed, shapes):
    flat = packed.reshape(packed.shape[:-2] + (-1,))
    out, off = [], 0
    for shp in shapes:
        size = int(np.prod(shp))
        out.append(flat[..., off:off + size].reshape(flat.shape[:-1] + tuple(shp)))
        off += size
    return out


def kernel(x, mem, mem_ln_g, mem_ln_b, ln_g, ln_b, ffn1_w_gu, ffn1_w_down, w_in, dn_conv_w, dn_a_log, dn_dt_bias, dn_norm_w, swa_sinks, w_mem_kv, w_branch, w_out, ffn2_w_gu, ffn2_w_down, loss_target, m_mem_ln_g, m_mem_ln_b, m_ln_g, m_ln_b, m_ffn1_w_gu, m_ffn1_w_down, m_w_in, m_dn_conv_w, m_dn_a_log, m_dn_dt_bias, m_dn_norm_w, m_swa_sinks, m_w_mem_kv, m_w_branch, m_w_out, m_ffn2_w_gu, m_ffn2_w_down, v_mem_ln_g, v_mem_ln_b, v_ln_g, v_ln_b, v_ffn1_w_gu, v_ffn1_w_down, v_w_in, v_dn_conv_w, v_dn_a_log, v_dn_dt_bias, v_dn_norm_w, v_swa_sinks, v_w_mem_kv, v_w_branch, v_w_out, v_ffn2_w_gu, v_ffn2_w_down):
    given = dict(locals())
    w_loc = {n: given[n] for n in WEIGHTS}
    m_loc = {n: given["m_" + n] for n in WEIGHTS}
    v_loc = {n: given["v_" + n] for n in WEIGHTS}
    me = _my_index()

    params_bf16 = {n: _join_shards(_all_gather(w_loc[n].astype(BF16), "all_gather_" + n), BIG_AXIS[n]) for n in BIG}
    params_bf16["w_in"] = _w_in_permute(params_bf16["w_in"])
    params = {n: params_bf16[n].astype(F32) for n in BIG}
    small_shard_shapes = [w_loc[n].shape for n in SMALL_SHARDED]
    small_gathered = _small_all_gather(_pack_small([w_loc[n] for n in SMALL_SHARDED]), False, "all_gather_small")
    for n, t in zip(SMALL_SHARDED, _unpack_small(small_gathered, small_shard_shapes)):
        t = jnp.moveaxis(t, 0, -2)
        params[n] = t.reshape(t.shape[:-2] + (-1,))
    for n in SMALL_REPL:
        params[n] = w_loc[n]

    loss_local, (g_params, grad_x) = jax.value_and_grad(_local_loss, argnums=(0, 1))(
        params, x[0], mem[0], loss_target[0], params_bf16)
    grad_x = grad_x[None]
    g_params["w_in"] = _w_in_unpermute(g_params["w_in"])

    g_big = {n: _sum_slots(_exchange_slots(_split_shards(g_params[n], BIG_AXIS[n]).astype(BF16),
                                           "exchange_grad_" + n), "sum_grad_" + n) for n in BIG}
    small_names = SMALL_SHARDED + SMALL_REPL
    small_full_shapes = [g_params[n].shape for n in small_names] + [()]
    small_sum = _small_all_gather(_pack_small([g_params[n] for n in small_names] + [loss_local]), True, "all_sum_small")
    small_full = dict(zip(small_names + ("loss",), _unpack_small(small_sum, small_full_shapes)))
    loss = small_full["loss"]
    grads = dict(g_big)
    for n in SMALL_SHARDED:
        shard = w_loc[n].shape[-1]
        grads[n] = lax.dynamic_slice_in_dim(small_full[n], me * shard, shard, axis=small_full[n].ndim - 1)
    for n in SMALL_REPL:
        grads[n] = small_full[n]

    delta, new_m, new_v = {}, {}, {}
    for n in BIG:
        delta[n], new_m[n], new_v[n] = _adamw(w_loc[n], grads[n], m_loc[n], v_loc[n], "adamw_" + n)
    shapes = [w_loc[n].shape for n in small_names]
    packed = [_pack_small([src[n] for n in small_names]) for src in (w_loc, grads, m_loc, v_loc)]
    for dst, t in zip((delta, new_m, new_v), _adamw(*packed, "adamw_small")):
        dst.update(zip(small_names, _unpack_small(t, shapes)))

    return (loss, grad_x, *[grads[n] for n in WEIGHTS], *[delta[n] for n in WEIGHTS],
            *[new_m[n] for n in WEIGHTS], *[new_v[n] for n in WEIGHTS])
```

```python
import functools

import numpy as np
import jax
import jax.numpy as jnp
from jax import lax
from jax.experimental import pallas as pl
from jax.experimental.pallas import tpu as pltpu

F32 = jnp.float32
BF16 = jnp.bfloat16
MESH = pl.DeviceIdType.MESH
N_DEV = 8
V7X_VMEM_LIMIT_BYTES = 56 * 1024 * 1024
LANES = 128
PACK_W = 1024

D_MODEL = 1024
DEPTH = 2
DN_HEADS, DN_DK, DN_DV, DN_CONV, DN_CHUNK = 4, 128, 128, 4, 64
SWA_HEADS, SWA_KV_HEADS, SWA_DH, WINDOW = 8, 2, 64, 128
XA_HEADS, XA_DH = 4, 128
D_FF = 2816
N_BRANCH, BRANCH_W = 3, 512
DEEPNORM_ALPHA = (2 * DEPTH) ** 0.25
LN_EPS = 1e-5
RMS_EPS = 1e-6
NEG_INF = -1e30
D_IN = 6408
D_IN_PAD = 6528
SEG = dict(dn_qkv=(0, 1536), dn_z=(1536, 2048), sw_q=(2048, 2560), sw_kv=(2560, 2816), xa_q=(2816, 3328),
           gates=(3328, 6400), dn_ba=(6400, 6528))

ADAM_LR, ADAM_B1, ADAM_B2, ADAM_EPS, ADAM_WD, ADAM_STEP = 0.001, 0.9, 0.999, 1e-08, 0.01, 10

BIG = ("ffn1_w_gu", "ffn1_w_down", "w_in", "w_mem_kv", "w_branch", "w_out", "ffn2_w_gu", "ffn2_w_down")
BIG_AXIS = dict(ffn1_w_gu=2, ffn1_w_down=1, w_in=2, w_mem_kv=1, w_branch=3, w_out=1, ffn2_w_gu=2, ffn2_w_down=1)
SMALL_SHARDED = ("ln_g", "ln_b", "dn_conv_w")
SMALL_REPL = ("mem_ln_g", "mem_ln_b", "dn_a_log", "dn_dt_bias", "dn_norm_w", "swa_sinks")
WEIGHTS = ("mem_ln_g", "mem_ln_b", "ln_g", "ln_b", "ffn1_w_gu", "ffn1_w_down", "w_in", "dn_conv_w", "dn_a_log",
           "dn_dt_bias", "dn_norm_w", "swa_sinks", "w_mem_kv", "w_branch", "w_out", "ffn2_w_gu", "ffn2_w_down")

HI = lax.Precision.HIGHEST
NN = (((1,), (0,)), ((), ()))
NT = (((1,), (1,)), ((), ()))
TN = (((0,), (0,)), ((), ()))


def _tile(n, target, align):
    best = None
    for d in range(align, min(n, target) + 1, align):
        if n % d == 0:
            best = d
    return n if best is None else best


def _my_index():
    return 4 * lax.axis_index("x") + 2 * lax.axis_index("y") + lax.axis_index("c")


def _peer(k):
    x, y, c = lax.axis_index("x"), lax.axis_index("y"), lax.axis_index("c")
    px = 1 - x if (k >> 2) & 1 else x
    py = 1 - y if (k >> 1) & 1 else y
    pc = 1 - c if k & 1 else c
    return (px, py, pc), 4 * px + 2 * py + pc


def _matmul(a, b, *, ta=False, tb=False, name, tm=512, tn=2176, tk=2304):
    m, k = (a.shape[1], a.shape[0]) if ta else a.shape
    k2, n = (b.shape[1], b.shape[0]) if tb else b.shape
    assert k == k2, (a.shape, b.shape, ta, tb)
    tm = _tile(m, tm, LANES if ta else 8)
    tn = _tile(n, tn, LANES)
    tk = _tile(k, tk, LANES)
    nk = k // tk
    dims = (((0 if ta else 1,), (1 if tb else 0,)), ((), ()))

    def body(a_ref, b_ref, o_ref):
        kk = pl.program_id(2)
        p = lax.dot_general(a_ref[...].astype(BF16), b_ref[...].astype(BF16), dims, preferred_element_type=F32)

        @pl.when(kk == 0)
        def _():
            o_ref[...] = p

        @pl.when(kk != 0)
        def _():
            o_ref[...] += p

    a_spec = pl.BlockSpec((tk, tm), lambda j, i, kk: (kk, i)) if ta else pl.BlockSpec((tm, tk), lambda j, i, kk: (i, kk))
    b_spec = pl.BlockSpec((tn, tk), lambda j, i, kk: (j, kk)) if tb else pl.BlockSpec((tk, tn), lambda j, i, kk: (kk, j))
    return pl.pallas_call(
        body, name=name, grid=(n // tn, m // tm, nk), in_specs=[a_spec, b_spec],
        out_specs=pl.BlockSpec((tm, tn), lambda j, i, kk: (i, j)),
        out_shape=jax.ShapeDtypeStruct((m, n), F32),
        compiler_params=pltpu.CompilerParams(dimension_semantics=("parallel", "parallel", "arbitrary"),
                                             vmem_limit_bytes=V7X_VMEM_LIMIT_BYTES),
    )(a, b)


@functools.partial(jax.custom_vjp, nondiff_argnums=(3,))
def pmm(x, w, w_bf16, tag):
    return _pmm_forward(x, w_bf16, tag)


def _pmm_forward(x, w_bf16, tag):
    wide = w_bf16.shape[1] == 2 * D_FF
    return _matmul(x, w_bf16, name=tag + "_fwd", tm=512 if wide else 1024, tn=D_FF if wide else 2176)


def _pmm_fwd(x, w, w_bf16, tag):
    return _pmm_forward(x, w_bf16, tag), (x, w_bf16)


def _pmm_bwd(tag, res, dy):
    x, w_bf16 = res
    dx = _matmul(dy, w_bf16, tb=True, name=tag + "_dx", tm=1024)
    dw = _matmul(x, dy, ta=True, name=tag + "_dw", tm=1408, tn=2176, tk=512)
    return dx, dw, jnp.zeros_like(w_bf16)


pmm.defvjp(_pmm_fwd, _pmm_bwd)


def _all_gather(block, name):
    def body(x_ref, out_ref, send_sems, recv_sems, local_sem):
        me = _my_index()
        mine = pltpu.make_async_copy(x_ref, out_ref.at[me], local_sem)
        mine.start()
        sends = []
        for k in range(1, N_DEV):
            peer, _ = _peer(k)
            cp = pltpu.make_async_remote_copy(src_ref=x_ref, dst_ref=out_ref.at[me], send_sem=send_sems.at[k - 1],
                                              recv_sem=recv_sems.at[k - 1], device_id=peer, device_id_type=MESH)
            cp.start()
            sends.append(cp)
        for k in range(1, N_DEV):
            peer, p = _peer(k)
            pltpu.make_async_remote_copy(src_ref=x_ref, dst_ref=out_ref.at[p], send_sem=send_sems.at[k - 1],
                                         recv_sem=recv_sems.at[k - 1], device_id=peer, device_id_type=MESH).wait_recv()
        for cp in sends:
            cp.wait_send()
        mine.wait()

    return pl.pallas_call(
        body, name=name,
        out_shape=jax.ShapeDtypeStruct((N_DEV,) + block.shape, block.dtype),
        in_specs=[pl.BlockSpec(memory_space=pl.ANY)], out_specs=pl.BlockSpec(memory_space=pl.ANY),
        scratch_shapes=[pltpu.SemaphoreType.DMA((N_DEV - 1,)), pltpu.SemaphoreType.DMA((N_DEV - 1,)),
                        pltpu.SemaphoreType.DMA],
    )(block)


def _exchange_slots(slots, name):
    def body(x_ref, out_ref, send_sems, recv_sems, local_sem):
        me = _my_index()
        mine = pltpu.make_async_copy(x_ref.at[me], out_ref.at[me], local_sem)
        mine.start()
        sends = []
        for k in range(1, N_DEV):
            peer, p = _peer(k)
            cp = pltpu.make_async_remote_copy(src_ref=x_ref.at[p], dst_ref=out_ref.at[me], send_sem=send_sems.at[k - 1],
                                              recv_sem=recv_sems.at[k - 1], device_id=peer, device_id_type=MESH)
            cp.start()
            sends.append(cp)
        for k in range(1, N_DEV):
            peer, p = _peer(k)
            pltpu.make_async_remote_copy(src_ref=x_ref.at[p], dst_ref=out_ref.at[p], send_sem=send_sems.at[k - 1],
                                         recv_sem=recv_sems.at[k - 1], device_id=peer, device_id_type=MESH).wait_recv()
        for cp in sends:
            cp.wait_send()
        mine.wait()

    return pl.pallas_call(
        body, name=name,
        out_shape=jax.ShapeDtypeStruct(slots.shape, slots.dtype),
        in_specs=[pl.BlockSpec(memory_space=pl.ANY)], out_specs=pl.BlockSpec(memory_space=pl.ANY),
        scratch_shapes=[pltpu.SemaphoreType.DMA((N_DEV - 1,)), pltpu.SemaphoreType.DMA((N_DEV - 1,)),
                        pltpu.SemaphoreType.DMA],
    )(slots)


def _sum_slots(slots, name):
    shape = slots.shape[1:]
    w = shape[-1]
    slots = slots.reshape(N_DEV, -1, w)
    r = slots.shape[1]
    tr = _tile(r, 256, 8)

    def body(x_ref, o_ref):
        acc = x_ref[0].astype(F32)
        for p in range(1, N_DEV):
            acc = acc + x_ref[p].astype(F32)
        o_ref[...] = acc

    return pl.pallas_call(
        body, name=name, grid=(r // tr,),
        in_specs=[pl.BlockSpec((N_DEV, tr, w), lambda i: (0, i, 0))],
        out_specs=pl.BlockSpec((tr, w), lambda i: (i, 0)),
        out_shape=jax.ShapeDtypeStruct((r, w), F32),
        compiler_params=pltpu.CompilerParams(dimension_semantics=("parallel",), vmem_limit_bytes=V7X_VMEM_LIMIT_BYTES),
    )(slots).reshape(shape)


def _small_all_gather(v, reduce, name):
    rows, w = v.shape

    def body(x_ref, out_ref, land_ref, send_sems, recv_sems):
        me = _my_index()
        sends = []
        for k in range(1, N_DEV):
            peer, _ = _peer(k)
            cp = pltpu.make_async_remote_copy(src_ref=x_ref, dst_ref=land_ref.at[me], send_sem=send_sems.at[k - 1],
                                              recv_sem=recv_sems.at[k - 1], device_id=peer, device_id_type=MESH)
            cp.start()
            sends.append(cp)
        land_ref[me] = x_ref[...]
        for k in range(1, N_DEV):
            peer, p = _peer(k)
            pltpu.make_async_remote_copy(src_ref=x_ref, dst_ref=land_ref.at[p], send_sem=send_sems.at[k - 1],
                                         recv_sem=recv_sems.at[k - 1], device_id=peer, device_id_type=MESH).wait_recv()
        for cp in sends:
            cp.wait_send()
        if reduce:
            acc = land_ref[0]
            for p in range(1, N_DEV):
                acc = acc + land_ref[p]
            out_ref[...] = acc
        else:
            out_ref[...] = land_ref[...]

    out_shape = (rows, w) if reduce else (N_DEV, rows, w)
    return pl.pallas_call(
        body, name=name,
        out_shape=jax.ShapeDtypeStruct(out_shape, F32),
        in_specs=[pl.BlockSpec(memory_space=pltpu.VMEM)], out_specs=pl.BlockSpec(memory_space=pltpu.VMEM),
        scratch_shapes=[pltpu.VMEM((N_DEV, rows, w), F32), pltpu.SemaphoreType.DMA((N_DEV - 1,)),
                        pltpu.SemaphoreType.DMA((N_DEV - 1,))],
    )(v)


def _adamw(w, g, m, v, name):
    shape = w.shape
    cols = shape[-1] if len(shape) > 1 else shape[0]
    w2, g2, m2, v2 = (t.reshape(-1, cols) for t in (w, g, m, v))
    rows = w2.shape[0]
    tr = _tile(rows, 256, 8)

    def body(w_ref, g_ref, m_ref, v_ref, d_ref, mo_ref, vo_ref):
        gv = g_ref[...]
        mn = ADAM_B1 * m_ref[...] + (1.0 - ADAM_B1) * gv
        vn = ADAM_B2 * v_ref[...] + (1.0 - ADAM_B2) * (gv * gv)
        m_hat = mn / (1.0 - ADAM_B1 ** ADAM_STEP)
        v_hat = vn / (1.0 - ADAM_B2 ** ADAM_STEP)
        d_ref[...] = -ADAM_LR * (m_hat / (jnp.sqrt(v_hat) + ADAM_EPS) + ADAM_WD * w_ref[...])
        mo_ref[...] = mn
        vo_ref[...] = vn

    spec = pl.BlockSpec((tr, cols), lambda i: (i, 0))
    outs = pl.pallas_call(
        body, name=name, grid=(rows // tr,), in_specs=[spec] * 4, out_specs=[spec] * 3,
        out_shape=[jax.ShapeDtypeStruct((rows, cols), F32)] * 3,
        compiler_params=pltpu.CompilerParams(dimension_semantics=("parallel",), vmem_limit_bytes=V7X_VMEM_LIMIT_BYTES),
    )(w2, g2, m2, v2)
    return tuple(o.reshape(shape) for o in outs)


_DOT_DIMS = {2: dict(nn=NN, nt=NT, tn=TN),
             3: dict(nn=(((2,), (1,)), ((0,), (0,))), nt=(((2,), (2,)), ((0,), (0,))), tn=(((1,), (1,)), ((0,), (0,))))}


def _bf16_dot(a, b, kind):
    if a.ndim == 3:
        return jnp.stack([_bf16_dot(a[i], b[i] if b.ndim == 3 else b, kind) for i in range(a.shape[0])])
    return lax.dot_general(a.astype(BF16), b.astype(BF16), _DOT_DIMS[2][kind], preferred_element_type=F32)


def _f32_dot(a, b, kind):
    if a.ndim == 3:
        return jnp.stack([_f32_dot(a[i], b[i] if b.ndim == 3 else b, kind) for i in range(a.shape[0])])
    return lax.dot_general(a, b, _DOT_DIMS[2][kind], precision=HI, preferred_element_type=F32)


def _dot_family(raw):
    @jax.custom_vjp
    def nn(a, b):
        return raw(a, b, "nn")

    @jax.custom_vjp
    def nt(a, b):
        return raw(a, b, "nt")

    @jax.custom_vjp
    def tn(a, b):
        return raw(a, b, "tn")

    nn.defvjp(lambda a, b: (raw(a, b, "nn"), (a, b)), lambda r, ct: (nt(ct, r[1]), tn(r[0], ct)))
    nt.defvjp(lambda a, b: (raw(a, b, "nt"), (a, b)), lambda r, ct: (nn(ct, r[1]), tn(ct, r[0])))
    tn.defvjp(lambda a, b: (raw(a, b, "tn"), (a, b)), lambda r, ct: (nt(r[1], ct), nn(r[0], ct)))
    return nn, nt, tn


_bdot_nn, _bdot_nt, _bdot_tn = _dot_family(_bf16_dot)
_hdot_nn, _hdot_nt, _hdot_tn = _dot_family(_f32_dot)


def _f32x3_dot(a, b, kind):
    if a.ndim == 3:
        return jnp.stack([_f32x3_dot(a[i], b[i], kind) for i in range(a.shape[0])])
    return lax.dot_general(a, b, _DOT_DIMS[2][kind], precision=lax.Precision.HIGH, preferred_element_type=F32)


_mdot_nn, _mdot_nt, _mdot_tn = _dot_family(_f32x3_dot)


def _rows(width, t):
    return (t, width), lambda i: (i, 0)


def _whole(shape):
    return tuple(shape), lambda i: (0,) * len(shape)


def _tiled_op(name, fn, steps, in_blocks, kinds, outs, grad_blocks=None, grad_fix=None):
    n_in, n_out = len(in_blocks), len(outs)
    diff = [j for j, kd in enumerate(kinds) if kd != "const"]
    has_param = any(kd == "param" for kd in kinds)
    in_specs = [pl.BlockSpec(b, m) for b, m in in_blocks]
    out_specs = [pl.BlockSpec(b, m) for _, b, m in outs]

    def forward(*arrays):
        def body(*refs):
            res = fn(pl.program_id(0), *[r[...] for r in refs[:n_in]])
            for o_ref, val in zip(refs[n_in:], res):
                o_ref[...] = val

        return pl.pallas_call(
            body, name=name + "_fwd", grid=(steps,), in_specs=in_specs, out_specs=out_specs,
            out_shape=[jax.ShapeDtypeStruct(s, F32) for s, _, _ in outs],
            compiler_params=pltpu.CompilerParams(dimension_semantics=("parallel",),
                                                 vmem_limit_bytes=V7X_VMEM_LIMIT_BYTES),
        )(*arrays)

    def backward(arrays, douts):
        g_specs = []
        for j in diff:
            if kinds[j] == "param":
                g_specs.append((arrays[j].shape,) + _whole(arrays[j].shape))
            else:
                g_specs.append((grad_blocks or {}).get(j, (arrays[j].shape,) + in_blocks[j]))

        def body(*refs):
            i = pl.program_id(0)
            vals = [r[...] for r in refs[:n_in]]
            dos = tuple(r[...] for r in refs[n_in:n_in + n_out])
            g_refs = refs[n_in + n_out:]

            def of_diff(*dv):
                full = list(vals)
                for j, val in zip(diff, dv):
                    full[j] = val
                return tuple(fn(i, *full))

            _, vjp = jax.vjp(of_diff, *[vals[j] for j in diff])
            grads = vjp(dos)
            for j, g_ref, g in zip(diff, g_refs, grads):
                if kinds[j] == "param":
                    @pl.when(i == 0)
                    def _(g_ref=g_ref, g=g):
                        g_ref[...] = g

                    @pl.when(i != 0)
                    def _(g_ref=g_ref, g=g):
                        g_ref[...] += g
                else:
                    g_ref[...] = g

        return pl.pallas_call(
            body, name=name + "_bwd", grid=(steps,), in_specs=in_specs + out_specs,
            out_specs=[pl.BlockSpec(b, m) for _, b, m in g_specs],
            out_shape=[jax.ShapeDtypeStruct(shp, F32) for shp, _, _ in g_specs],
            compiler_params=pltpu.CompilerParams(dimension_semantics=("arbitrary" if has_param else "parallel",),
                                                 vmem_limit_bytes=V7X_VMEM_LIMIT_BYTES),
        )(*arrays, *douts)

    @jax.custom_vjp
    def op(*arrays):
        return tuple(forward(*arrays))

    def op_fwd(*arrays):
        return tuple(forward(*arrays)), arrays

    def op_bwd(arrays, douts):
        grads = backward(arrays, douts)
        full = [jnp.zeros_like(a) for a in arrays]
        for j, g in zip(diff, grads):
            full[j] = grad_fix[j](g) if grad_fix and j in grad_fix else g
        return tuple(full)

    op.defvjp(op_fwd, op_bwd)
    return op


def _sigmoid(x):
    return jax.nn.sigmoid(x)


def _silu(x):
    return x * jax.nn.sigmoid(x)


def _softplus(x):
    return jnp.maximum(x, 0.0) + jnp.log(1.0 + jnp.exp(-jnp.abs(x)))


def _heads(t, width):
    return [t[:, h * width:(h + 1) * width] for h in range(t.shape[1] // width)]


def _ffn_act(gu, name):
    s, two_f = gu.shape
    f = two_f // 2
    t = _tile(s, 256, 8)

    def fn(i, gu_t):
        return (_silu(gu_t[:, :f]) * gu_t[:, f:],)

    return _tiled_op(name, fn, s // t, [_rows(two_f, t)], ["row"], [((s, f),) + _rows(f, t)])(gu)[0]


def _res_ln(x, f, g, b, scale, name):
    s, d = x.shape
    t = _tile(s, 512, 8)

    def fn(i, x_t, f_t, g_t, b_t):
        r = DEEPNORM_ALPHA * x_t + scale * f_t
        mu = jnp.mean(r, axis=-1, keepdims=True)
        var = jnp.mean(jnp.square(r - mu), axis=-1, keepdims=True)
        return ((r - mu) * lax.rsqrt(var + LN_EPS) * g_t + b_t,)

    op = _tiled_op(name, fn, s // t, [_rows(d, t), _rows(d, t), _whole((1, d)), _whole((1, d))],
                   ["row", "row", "param", "param"], [((s, d),) + _rows(d, t)])
    return op(x, f, g.reshape(1, d), b.reshape(1, d))[0]


def _row_loss(y, target, name):
    s, d = y.shape
    t = _tile(s, 512, 8)

    def fn(i, y_t, t_t):
        e = y_t - t_t
        return (jnp.broadcast_to(0.5 * jnp.mean(e * e, axis=-1, keepdims=True), (t, LANES)),)

    return _tiled_op(name, fn, s // t, [_rows(d, t), _rows(d, t)], ["row", "const"],
                     [((s, LANES),) + _rows(LANES, t)])(y, target)[0]


def _gate_merge(gates, b0, b1, b2, name):
    s, d = b0.shape
    t = _tile(s, 256, 8)

    def fn(i, g_t, b0_t, b1_t, b2_t):
        return (_sigmoid(g_t[:, :d]) * b0_t + _sigmoid(g_t[:, d:2 * d]) * b1_t + _sigmoid(g_t[:, 2 * d:]) * b2_t,)

    op = _tiled_op(name, fn, s // t, [_rows(3 * d, t)] + [_rows(d, t)] * 3, ["row"] * 4, [((s, d),) + _rows(d, t)])
    return op(gates, b0, b1, b2)[0]


def _dn_pre(x, ba, conv_w, a_log, dt_bias, name):
    s, xw = x.shape
    w = DN_HEADS * DN_DK
    t = _tile(s, 256, 8)
    steps = s // t
    lane_pad = jnp.zeros((LANES - 2 * DN_HEADS,), F32)
    pa = jnp.concatenate([jnp.zeros((DN_HEADS,), F32), a_log, lane_pad]).reshape(1, LANES)
    pd = jnp.concatenate([jnp.zeros((DN_HEADS,), F32), dt_bias, lane_pad]).reshape(1, LANES)
    cw = jnp.concatenate([conv_w, jnp.zeros((8 - DN_CONV, xw), F32)], axis=0)
    taps = [jnp.pad(x, ((DN_CONV - 1 - j, 0), (0, 0)))[:s] for j in range(DN_CONV - 1)] + [x]

    def unit(x):
        return x * lax.rsqrt(jnp.sum(x * x, axis=-1, keepdims=True) + RMS_EPS)

    def fn(i, x0_t, x1_t, x2_t, x3_t, ba_t, cw_t, pa_t, pd_t):
        tap = lax.broadcasted_iota(jnp.int32, cw_t.shape, 0)
        c_t = sum(x_t * jnp.sum(jnp.where(tap == j, cw_t, 0.0), axis=0, keepdims=True)
                  for j, x_t in enumerate((x0_t, x1_t, x2_t, x3_t)))
        act = _silu(c_t)
        q = jnp.concatenate([unit(x) for x in _heads(act[:, :w], DN_DK)], axis=1)
        k = jnp.concatenate([unit(x) for x in _heads(act[:, w:2 * w], DN_DK)], axis=1)
        beta = _sigmoid(ba_t)
        g = -jnp.exp(pa_t) * _softplus(ba_t + pd_t)
        src = lax.broadcasted_iota(jnp.int32, (LANES, w), 0)
        head = lax.broadcasted_iota(jnp.int32, (LANES, w), 1) >> int(np.log2(DN_DK))
        be = _hdot_nn(beta, (src == head).astype(F32))
        ge = _hdot_nn(g, (src == head + DN_HEADS).astype(F32))
        return q, k, act[:, 2 * w:], be, ge

    op = _tiled_op(name, fn, steps,
                   [_rows(xw, t)] * DN_CONV + [_rows(LANES, t), _whole((8, xw)), _whole((1, LANES)), _whole((1, LANES))],
                   ["row"] * (DN_CONV + 1) + ["param"] * 3, [((s, w),) + _rows(w, t)] * 5)
    return op(*taps, ba, cw, pa, pd)


def _dn_post(o, z, norm_w, name):
    s, w = o.shape
    t = _tile(s, 512, 8)

    def fn(i, o_t, z_t, nw_t):
        outs = [x * lax.rsqrt(jnp.mean(x * x, axis=-1, keepdims=True) + RMS_EPS) * nw_t * _silu(zz)
                for x, zz in zip(_heads(o_t, DN_DV), _heads(z_t, DN_DV))]
        return (jnp.concatenate(outs, axis=1),)

    op = _tiled_op(name, fn, s // t, [_rows(w, t), _rows(w, t), _whole((1, DN_DV))], ["row", "row", "param"],
                   [((s, w),) + _rows(w, t)])
    return op(o, z, norm_w.reshape(1, DN_DV))[0]


def _xattn(q, k_mem, v_mem, name):
    s, w = q.shape
    t = _tile(s, 512, 8)

    def fn(i, q_t, k_t, v_t):
        outs = []
        for qh, kh, vh in zip(_heads(q_t, XA_DH), _heads(k_t, XA_DH), _heads(v_t, XA_DH)):
            sc = _bdot_nt(qh, kh) * (XA_DH ** -0.5)
            e = jnp.exp(sc - jnp.max(sc, axis=-1, keepdims=True))
            outs.append(_bdot_nn(e / jnp.sum(e, axis=-1, keepdims=True), vh))
        return (jnp.concatenate(outs, axis=1),)

    op = _tiled_op(name, fn, s // t, [_rows(w, t), _whole(k_mem.shape), _whole(v_mem.shape)],
                   ["row", "param", "param"], [((s, w),) + _rows(w, t)])
    return op(q, k_mem, v_mem)[0]


def _swa(q, kv, sinks, name):
    s, w = q.shape
    wd = WINDOW
    grp = SWA_HEADS // SWA_KV_HEADS
    dh = SWA_DH
    steps = s // wd
    sink_rows = jnp.broadcast_to(sinks[:, None], (SWA_HEADS, LANES))

    def fn(i, q_t, kc_t, kp_t, sk_t):
        qr = lax.broadcasted_iota(jnp.int32, (grp * wd, 2 * wd), 0) & (wd - 1)
        kc = lax.broadcasted_iota(jnp.int32, (grp * wd, 2 * wd), 1)
        dist = qr + wd - kc
        mask = (dist >= 0) & (dist < wd) & ((kc >= wd) | (i > 0))
        head_row = lax.broadcasted_iota(jnp.int32, (SWA_HEADS, LANES), 0)
        outs = [None] * SWA_HEADS
        for hk in range(SWA_KV_HEADS):
            kh = jnp.concatenate([kp_t[:, hk * dh:(hk + 1) * dh], kc_t[:, hk * dh:(hk + 1) * dh]], axis=0)
            vh = jnp.concatenate([kp_t[:, (SWA_KV_HEADS + hk) * dh:(SWA_KV_HEADS + hk + 1) * dh],
                                  kc_t[:, (SWA_KV_HEADS + hk) * dh:(SWA_KV_HEADS + hk + 1) * dh]], axis=0)
            qg = jnp.concatenate([q_t[:, (hk * grp + g) * dh:(hk * grp + g + 1) * dh] for g in range(grp)], axis=0)
            sink = jnp.concatenate(
                [jnp.broadcast_to(jnp.sum(jnp.where(head_row == hk * grp + g, sk_t, 0.0), axis=0, keepdims=True),
                                  (wd, LANES)) for g in range(grp)], axis=0)
            sink = jnp.max(sink, axis=-1, keepdims=True)
            sc = jnp.where(mask, _bdot_nt(qg, kh) * (dh ** -0.5), NEG_INF)
            m = jnp.maximum(jnp.max(sc, axis=-1, keepdims=True), sink)
            p = jnp.exp(sc - m)
            p = p / (jnp.sum(p, axis=-1, keepdims=True) + jnp.exp(sink - m))
            o = _bdot_nn(p, vh)
            for g in range(grp):
                outs[hk * grp + g] = o[g * wd:(g + 1) * wd]
        return (jnp.concatenate(outs, axis=1),)

    kvw = kv.shape[1]
    prev_block = ((wd, kvw), lambda i: (jnp.maximum(i - 1, 0), 0))
    shift = lambda g: jnp.concatenate([g[wd:], jnp.zeros((wd, kvw), F32)], axis=0)
    op = _tiled_op(name, fn, steps, [_rows(w, wd), _rows(kvw, wd), prev_block, _whole((SWA_HEADS, LANES))],
                   ["row", "row", "row", "param"], [((s, w),) + _rows(w, wd)],
                   grad_blocks={2: (kv.shape,) + _rows(kvw, wd)}, grad_fix={2: shift})
    return op(q, kv, kv, sink_rows)[0]


def _dn_prepare(q, k, v, be, ge):
    b, c, dk = q.shape
    row = lax.broadcasted_iota(jnp.int32, (c, c), 0)
    col = lax.broadcasted_iota(jnp.int32, (c, c), 1)
    tril = row >= col
    gc = _hdot_nn(jnp.broadcast_to(tril.astype(F32), (b, c, c)), ge)
    g_col = _hdot_nn(gc, jnp.full((b, LANES, c), 1.0 / LANES, F32))
    g_row = _hdot_nt(jnp.full((b, c, LANES), 1.0 / LANES, F32), gc)
    decay = jnp.where(tril, jnp.exp(jnp.where(tril, g_col - g_row, 0.0)), 0.0)
    kb = k * be
    a = jnp.where(row > col, _bdot_nt(kb, k) * decay, 0.0)
    x = jnp.broadcast_to((row == col).astype(F32), (b, c, c))
    for lvl in range(int(np.log2(c))):
        off = ((row >> lvl) == (col >> lvl) + 1) & ((row >> (lvl + 1)) == (col >> (lvl + 1)))
        x = x - _mdot_nn(x, _mdot_nn(jnp.where(off, a, 0.0), x))
    eg = jnp.exp(gc)
    qs = q * (dk ** -0.5)
    qk = jnp.where(tril, _bdot_nt(qs, k) * decay, 0.0)
    gl_c = _hdot_nn(jnp.ones((b, c, c), F32), ge)
    gl_s = _hdot_nn(jnp.ones((b, dk, c), F32), ge)
    return _mdot_nn(x, v * be), _mdot_nn(x, kb * eg), qs * eg, k * jnp.exp(gl_c - gc), qk, jnp.exp(gl_s)


def _dn_stack(ref, chunks, heads):
    return jnp.stack([ref[ci * DN_CHUNK:(ci + 1) * DN_CHUNK, h * DN_DK:(h + 1) * DN_DK]
                      for ci in range(chunks) for h in range(heads)])


def _dn_step(u, w, q_dec, k_tail, qk, decay, s):
    v_new = u - _bdot_nn(w, s)
    o = _bdot_nn(q_dec, s) + _bdot_nn(qk, v_new)
    return o, s * decay + _bdot_tn(k_tail, v_new)


def _hdot(a, b, dims=NN):
    return lax.dot_general(a, b, dims, precision=HI, preferred_element_type=F32)


def _mdot(a, b, dims=NN):
    return lax.dot_general(a, b, dims, precision=lax.Precision.HIGH, preferred_element_type=F32)


def _dn_chunk_heads(q, k, v, be, ge, s):
    c, dk = q[0].shape
    row = lax.broadcasted_iota(jnp.int32, (c, c), 0)
    col = lax.broadcasted_iota(jnp.int32, (c, c), 1)
    tril = row >= col
    trilf = tril.astype(F32)
    mean_rows = jnp.full((c, LANES), 1.0 / LANES, F32)
    gc = [_hdot(trilf, g) for g in ge]
    g_row = [_hdot(mean_rows, g, NT) for g in gc]
    decay = [jnp.where(tril, jnp.exp(jnp.where(tril, g[:, :c] - gr, 0.0)), 0.0) for g, gr in zip(gc, g_row)]
    kb = [ki * bi for ki, bi in zip(k, be)]
    a = [jnp.where(row > col, _bdot_nt(x_, y_) * d, 0.0) for x_, y_, d in zip(kb, k, decay)]
    x = [(row == col).astype(F32)] * len(q)
    for lvl in range(int(np.log2(c))):
        off = ((row >> lvl) == (col >> lvl) + 1) & ((row >> (lvl + 1)) == (col >> (lvl + 1)))
        t = [_mdot(jnp.where(off, ai, 0.0), xi) for ai, xi in zip(a, x)]
        x = [xi - _mdot(xi, ti) for xi, ti in zip(x, t)]
    eg = [jnp.exp(g) for g in gc]
    u = [_mdot(xi, vi * bi) for xi, vi, bi in zip(x, v, be)]
    w = [_mdot(xi, ki * ei) for xi, ki, ei in zip(x, kb, eg)]
    qs = [qi * (dk ** -0.5) for qi in q]
    qk = [jnp.where(tril, _bdot_nt(x_, y_) * d, 0.0) for x_, y_, d in zip(qs, k, decay)]
    gl = [jnp.sum(g, axis=0, keepdims=True) for g in ge]
    v_new = [ui - _bdot_nn(wi, si) for ui, wi, si in zip(u, w, s)]
    o = [_bdot_nn(qi * ei, si) + _bdot_nn(ai, vi) for qi, ei, si, ai, vi in zip(qs, eg, s, qk, v_new)]
    s_next = [si * jnp.exp(g) + _bdot_tn(ki * jnp.exp(g - gi), vi)
              for si, g, ki, gi, vi in zip(s, gl, k, gc, v_new)]
    return o, s_next


def _dn_forward(q, k, v, be, ge, name, chunks_per_step=4):
    s_len, width = q.shape
    heads = width // DN_DK
    n = s_len // DN_CHUNK
    cb = min(chunks_per_step, n)
    rows = cb * DN_CHUNK

    def body(q_ref, k_ref, v_ref, be_ref, ge_ref, o_ref, st_ref, s_scr):
        @pl.when(pl.program_id(0) == 0)
        def _():
            s_scr[...] = jnp.zeros_like(s_scr)

        parts = _dn_prepare(*[_dn_stack(ref, cb, heads) for ref in (q_ref, k_ref, v_ref, be_ref, ge_ref)])
        for ci in range(cb):
            s = s_scr[...]
            o, s_next = _dn_step(*[t[ci * heads:(ci + 1) * heads] for t in parts], s)
            for h in range(heads):
                st_ref[ci, h] = s[h]
                o_ref[ci * DN_CHUNK:(ci + 1) * DN_CHUNK, h * DN_DK:(h + 1) * DN_DK] = o[h]
            s_scr[...] = s_next

    tok = pl.BlockSpec((rows, width), lambda i: (i, 0))
    return pl.pallas_call(
        body, name=name + "_fwd", grid=(n // cb,), in_specs=[tok] * 5,
        out_specs=[tok, pl.BlockSpec((cb, heads, DN_DK, DN_DV), lambda i: (i, 0, 0, 0))],
        out_shape=[jax.ShapeDtypeStruct((s_len, width), F32), jax.ShapeDtypeStruct((n, heads, DN_DK, DN_DV), F32)],
        scratch_shapes=[pltpu.VMEM((heads, DN_DK, DN_DV), F32)],
        compiler_params=pltpu.CompilerParams(dimension_semantics=("arbitrary",), vmem_limit_bytes=V7X_VMEM_LIMIT_BYTES),
    )(q, k, v, be, ge)


def _dn_backward(q, k, v, be, ge, states, do, name, chunks_per_step=4):
    s_len, width = q.shape
    heads = width // DN_DK
    n = s_len // DN_CHUNK
    cb = min(chunks_per_step, n)
    rows = cb * DN_CHUNK
    steps = n // cb

    def body(q_ref, k_ref, v_ref, be_ref, ge_ref, st_ref, do_ref, dq_ref, dk_ref, dv_ref, dbe_ref, dge_ref, ds_scr):
        @pl.when(pl.program_id(0) == 0)
        def _():
            ds_scr[...] = jnp.zeros_like(ds_scr)

        lanes = [slice(h * DN_DK, (h + 1) * DN_DK) for h in range(heads)]
        for ci in reversed(range(cb)):
            r = slice(ci * DN_CHUNK, (ci + 1) * DN_CHUNK)
            _, vjp = jax.vjp(_dn_chunk_heads, *[[ref[r, l] for l in lanes]
                                                for ref in (q_ref, k_ref, v_ref, be_ref, ge_ref)],
                             [st_ref[ci, h] for h in range(heads)])
            *grads, ds = vjp(([do_ref[r, l] for l in lanes], [ds_scr[h] for h in range(heads)]))
            for g_ref, g in zip((dq_ref, dk_ref, dv_ref, dbe_ref, dge_ref), grads):
                for l, g_head in zip(lanes, g):
                    g_ref[r, l] = g_head
            for h in range(heads):
                ds_scr[h] = ds[h]

    tok = pl.BlockSpec((rows, width), lambda i: (steps - 1 - i, 0))
    return pl.pallas_call(
        body, name=name + "_bwd", grid=(steps,),
        in_specs=[tok] * 5 + [pl.BlockSpec((cb, heads, DN_DK, DN_DV), lambda i: (steps - 1 - i, 0, 0, 0)), tok],
        out_specs=[tok] * 5,
        out_shape=[jax.ShapeDtypeStruct((s_len, width), F32)] * 5,
        scratch_shapes=[pltpu.VMEM((heads, DN_DK, DN_DV), F32)],
        compiler_params=pltpu.CompilerParams(dimension_semantics=("arbitrary",), vmem_limit_bytes=V7X_VMEM_LIMIT_BYTES),
    )(q, k, v, be, ge, states, do)


@functools.partial(jax.custom_vjp, nondiff_argnums=(5,))
def _dn_core(q, k, v, be, ge, name):
    return _dn_forward(q, k, v, be, ge, name)[0]


def _dn_core_fwd(q, k, v, be, ge, name):
    o, states = _dn_forward(q, k, v, be, ge, name)
    return o, (q, k, v, be, ge, states)


def _dn_core_bwd(name, res, do):
    return tuple(_dn_backward(*res, do, name))


_dn_core.defvjp(_dn_core_fwd, _dn_core_bwd)


def _layer_norm(x, g, b):
    mu = x.mean(-1, keepdims=True)
    var = jnp.square(x - mu).mean(-1, keepdims=True)
    return (x - mu) * lax.rsqrt(var + LN_EPS) * g + b


def _hybrid_layer(x, mem_n, p, p_bf16, l):
    tag = f"l{l}"
    ln_g, ln_b = p["ln_g"][l], p["ln_b"][l]

    def mm(a, name, tag_, *idx):
        return pmm(a, p[name][(l,) + idx], p_bf16[name][(l,) + idx], tag_)

    def swiglu(a, ffn):
        return mm(_ffn_act(mm(a, ffn + "_w_gu", f"{tag}_{ffn}_gu"), f"{tag}_{ffn}_act"), ffn + "_w_down",
                  f"{tag}_{ffn}_down")

    h = _res_ln(x, swiglu(x, "ffn1"), ln_g[0], ln_b[0], 0.5, tag + "_ln0")
    proj = mm(h, "w_in", tag + "_in")
    seg = {name: proj[:, lo:hi] for name, (lo, hi) in SEG.items()}
    q, k, v, be, ge = _dn_pre(seg["dn_qkv"], seg["dn_ba"], p["dn_conv_w"][l], p["dn_a_log"][l], p["dn_dt_bias"][l],
                              tag + "_dn_pre")
    o_dn = _dn_post(_dn_core(q, k, v, be, ge, tag + "_dn"), seg["dn_z"], p["dn_norm_w"][l], tag + "_dn_post")
    o_sw = _swa(seg["sw_q"], seg["sw_kv"], p["swa_sinks"][l], tag + "_swa")
    kv_mem = mm(mem_n, "w_mem_kv", tag + "_memkv")
    half = XA_HEADS * XA_DH
    o_xa = _xattn(seg["xa_q"], kv_mem[:, :half], kv_mem[:, half:], tag + "_xattn")
    branch = [mm(o, "w_branch", tag + f"_branch{i}", i) for i, o in enumerate((o_dn, o_sw, o_xa))]
    merged = _gate_merge(seg["gates"], *branch, tag + "_merge")
    h = _res_ln(h, mm(merged, "w_out", tag + "_out"), ln_g[1], ln_b[1], 1.0, tag + "_ln1")
    return _res_ln(h, swiglu(h, "ffn2"), ln_g[2], ln_b[2], 0.5, tag + "_ln2")


def _local_loss(p, x, mem, target, p_bf16):
    mem_n = _layer_norm(mem, p["mem_ln_g"], p["mem_ln_b"])
    y = x
    for l in range(DEPTH):
        y = _hybrid_layer(y, mem_n, p, p_bf16, l)
    return jnp.sum(_row_loss(y, target, "row_loss")[:, 0])


def _w_in_permute(w):
    pad = jnp.zeros(w.shape[:-1] + (D_IN_PAD - D_IN,), w.dtype)
    return jnp.concatenate([w[..., :1536], w[..., 1544:], w[..., 1536:1544], pad], axis=-1)


def _w_in_unpermute(w):
    return jnp.concatenate([w[..., :1536], w[..., 6400:6408], w[..., 1536:6400]], axis=-1)


def _join_shards(gathered, ax):
    shp = gathered.shape[1:]
    t = jnp.moveaxis(gathered, 0, ax)
    return t.reshape(shp[:ax] + (N_DEV * shp[ax],) + shp[ax + 1:])


def _split_shards(whole, ax):
    shp = whole.shape
    t = whole.reshape(shp[:ax] + (N_DEV, shp[ax] // N_DEV) + shp[ax + 1:])
    return jnp.moveaxis(t, ax, 0)


def _pack_small(parts):
    flat = jnp.concatenate([t.reshape(-1).astype(F32) for t in parts])
    rows = -(-flat.shape[0] // LANES)
    rows = -(-rows // 8) * 8
    return jnp.concatenate([flat, jnp.zeros((rows * LANES - flat.shape[0],), F32)]).reshape(rows, LANES)


def _unpack_small(packed, shapes):
    flat = packed.reshape(packed.shape[:-2] + (-1,))
    out, off = [], 0
    for shp in shapes:
        size = int(np.prod(shp))
        out.append(flat[..., off:off + size].reshape(flat.shape[:-1] + tuple(shp)))
        off += size
    return out


def kernel(x, mem, mem_ln_g, mem_ln_b, ln_g, ln_b, ffn1_w_gu, ffn1_w_down, w_in, dn_conv_w, dn_a_log, dn_dt_bias, dn_norm_w, swa_sinks, w_mem_kv, w_branch, w_out, ffn2_w_gu, ffn2_w_down, loss_target, m_mem_ln_g, m_mem_ln_b, m_ln_g, m_ln_b, m_ffn1_w_gu, m_ffn1_w_down, m_w_in, m_dn_conv_w, m_dn_a_log, m_dn_dt_bias, m_dn_norm_w, m_swa_sinks, m_w_mem_kv, m_w_branch, m_w_out, m_ffn2_w_gu, m_ffn2_w_down, v_mem_ln_g, v_mem_ln_b, v_ln_g, v_ln_b, v_ffn1_w_gu, v_ffn1_w_down, v_w_in, v_dn_conv_w, v_dn_a_log, v_dn_dt_bias, v_dn_norm_w, v_swa_sinks, v_w_mem_kv, v_w_branch, v_w_out, v_ffn2_w_gu, v_ffn2_w_down):
    given = dict(locals())
    w_loc = {n: given[n] for n in WEIGHTS}
    m_loc = {n: given["m_" + n] for n in WEIGHTS}
    v_loc = {n: given["v_" + n] for n in WEIGHTS}
    me = _my_index()

    params_bf16 = {n: _join_shards(_all_gather(w_loc[n].astype(BF16), "all_gather_" + n), BIG_AXIS[n]) for n in BIG}
    params_bf16["w_in"] = _w_in_permute(params_bf16["w_in"])
    params = {n: params_bf16[n].astype(F32) for n in BIG}
    small_shard_shapes = [w_loc[n].shape for n in SMALL_SHARDED]
    small_gathered = _small_all_gather(_pack_small([w_loc[n] for n in SMALL_SHARDED]), False, "all_gather_small")
    for n, t in zip(SMALL_SHARDED, _unpack_small(small_gathered, small_shard_shapes)):
        t = jnp.moveaxis(t, 0, -2)
        params[n] = t.reshape(t.shape[:-2] + (-1,))
    for n in SMALL_REPL:
        params[n] = w_loc[n]

    loss_local, (g_params, grad_x) = jax.value_and_grad(_local_loss, argnums=(0, 1))(
        params, x[0], mem[0], loss_target[0], params_bf16)
    grad_x = grad_x[None]
    g_params["w_in"] = _w_in_unpermute(g_params["w_in"])

    g_big = {n: _sum_slots(_exchange_slots(_split_shards(g_params[n], BIG_AXIS[n]).astype(BF16),
                                           "exchange_grad_" + n), "sum_grad_" + n) for n in BIG}
    small_names = SMALL_SHARDED + SMALL_REPL
    small_full_shapes = [g_params[n].shape for n in small_names] + [()]
    small_sum = _small_all_gather(_pack_small([g_params[n] for n in small_names] + [loss_local]), True, "all_sum_small")
    small_full = dict(zip(small_names + ("loss",), _unpack_small(small_sum, small_full_shapes)))
    loss = small_full["loss"]
    grads = dict(g_big)
    for n in SMALL_SHARDED:
        shard = w_loc[n].shape[-1]
        grads[n] = lax.dynamic_slice_in_dim(small_full[n], me * shard, shard, axis=small_full[n].ndim - 1)
    for n in SMALL_REPL:
        grads[n] = small_full[n]

    delta, new_m, new_v = {}, {}, {}
    for n in BIG:
        delta[n], new_m[n], new_v[n] = _adamw(w_loc[n], grads[n], m_loc[n], v_loc[n], "adamw_" + n)
    shapes = [w_loc[n].shape for n in small_names]
    packed = [_pack_small([src[n] for n in small_names]) for src in (w_loc, grads, m_loc, v_loc)]
    for dst, t in zip((delta, new_m, new_v), _adamw(*packed, "adamw_small")):
        dst.update(zip(small_names, _unpack_small(t, shapes)))

    return (loss, grad_x, *[grads[n] for n in WEIGHTS], *[delta[n] for n in WEIGHTS],
            *[new_m[n] for n in WEIGHTS], *[new_v[n] for n in WEIGHTS])
```

```python
import functools

import numpy as np
import jax
import jax.numpy as jnp
from jax import lax
from jax.experimental import pallas as pl
from jax.experimental.pallas import tpu as pltpu

F32 = jnp.float32
BF16 = jnp.bfloat16
MESH = pl.DeviceIdType.MESH
N_DEV = 8
V7X_VMEM_LIMIT_BYTES = 56 * 1024 * 1024
LANES = 128
PACK_W = 1024

D_MODEL = 1024
DEPTH = 2
DN_HEADS, DN_DK, DN_DV, DN_CONV, DN_CHUNK = 4, 128, 128, 4, 64
SWA_HEADS, SWA_KV_HEADS, SWA_DH, WINDOW = 8, 2, 64, 128
XA_HEADS, XA_DH = 4, 128
D_FF = 2816
N_BRANCH, BRANCH_W = 3, 512
DEEPNORM_ALPHA = (2 * DEPTH) ** 0.25
LN_EPS = 1e-5
RMS_EPS = 1e-6
NEG_INF = -1e30
D_IN = 6408
D_IN_PAD = 6528
SEG = dict(dn_qkv=(0, 1536), dn_z=(1536, 2048), sw_q=(2048, 2560), sw_kv=(2560, 2816), xa_q=(2816, 3328),
           gates=(3328, 6400), dn_ba=(6400, 6528))

ADAM_LR, ADAM_B1, ADAM_B2, ADAM_EPS, ADAM_WD, ADAM_STEP = 0.001, 0.9, 0.999, 1e-08, 0.01, 10

BIG = ("ffn1_w_gu", "ffn1_w_down", "w_in", "w_mem_kv", "w_branch", "w_out", "ffn2_w_gu", "ffn2_w_down")
BIG_AXIS = dict(ffn1_w_gu=2, ffn1_w_down=1, w_in=2, w_mem_kv=1, w_branch=3, w_out=1, ffn2_w_gu=2, ffn2_w_down=1)
SMALL_SHARDED = ("ln_g", "ln_b", "dn_conv_w")
SMALL_REPL = ("mem_ln_g", "mem_ln_b", "dn_a_log", "dn_dt_bias", "dn_norm_w", "swa_sinks")
WEIGHTS = ("mem_ln_g", "mem_ln_b", "ln_g", "ln_b", "ffn1_w_gu", "ffn1_w_down", "w_in", "dn_conv_w", "dn_a_log",
           "dn_dt_bias", "dn_norm_w", "swa_sinks", "w_mem_kv", "w_branch", "w_out", "ffn2_w_gu", "ffn2_w_down")

HI = lax.Precision.HIGHEST
NN = (((1,), (0,)), ((), ()))
NT = (((1,), (1,)), ((), ()))
TN = (((0,), (0,)), ((), ()))


def _tile(n, target, align):
    best = None
    for d in range(align, min(n, target) + 1, align):
        if n % d == 0:
            best = d
    return n if best is None else best


def _my_index():
    return 4 * lax.axis_index("x") + 2 * lax.axis_index("y") + lax.axis_index("c")


def _peer(k):
    x, y, c = lax.axis_index("x"), lax.axis_index("y"), lax.axis_index("c")
    px = 1 - x if (k >> 2) & 1 else x
    py = 1 - y if (k >> 1) & 1 else y
    pc = 1 - c if k & 1 else c
    return (px, py, pc), 4 * px + 2 * py + pc


def _matmul(a, b, *, ta=False, tb=False, name, tm=512, tn=2176, tk=2304):
    m, k = (a.shape[1], a.shape[0]) if ta else a.shape
    k2, n = (b.shape[1], b.shape[0]) if tb else b.shape
    assert k == k2, (a.shape, b.shape, ta, tb)
    tm = _tile(m, tm, LANES if ta else 8)
    tn = _tile(n, tn, LANES)
    tk = _tile(k, tk, LANES)
    nk = k // tk
    dims = (((0 if ta else 1,), (1 if tb else 0,)), ((), ()))

    def body(a_ref, b_ref, o_ref):
        kk = pl.program_id(2)
        p = lax.dot_general(a_ref[...].astype(BF16), b_ref[...].astype(BF16), dims, preferred_element_type=F32)

        @pl.when(kk == 0)
        def _():
            o_ref[...] = p

        @pl.when(kk != 0)
        def _():
            o_ref[...] += p

    a_spec = pl.BlockSpec((tk, tm), lambda j, i, kk: (kk, i)) if ta else pl.BlockSpec((tm, tk), lambda j, i, kk: (i, kk))
    b_spec = pl.BlockSpec((tn, tk), lambda j, i, kk: (j, kk)) if tb else pl.BlockSpec((tk, tn), lambda j, i, kk: (kk, j))
    return pl.pallas_call(
        body, name=name, grid=(n // tn, m // tm, nk), in_specs=[a_spec, b_spec],
        out_specs=pl.BlockSpec((tm, tn), lambda j, i, kk: (i, j)),
        out_shape=jax.ShapeDtypeStruct((m, n), F32),
        compiler_params=pltpu.CompilerParams(dimension_semantics=("parallel", "parallel", "arbitrary"),
                                             vmem_limit_bytes=V7X_VMEM_LIMIT_BYTES),
    )(a, b)


@functools.partial(jax.custom_vjp, nondiff_argnums=(3,))
def pmm(x, w, w_bf16, tag):
    return _pmm_forward(x, w_bf16, tag)


def _pmm_forward(x, w_bf16, tag):
    wide = w_bf16.shape[1] == 2 * D_FF
    return _matmul(x, w_bf16, name=tag + "_fwd", tm=512 if wide else 1024, tn=D_FF if wide else 2176)


def _pmm_fwd(x, w, w_bf16, tag):
    return _pmm_forward(x, w_bf16, tag), (x, w_bf16)


def _pmm_bwd(tag, res, dy):
    x, w_bf16 = res
    dx = _matmul(dy, w_bf16, tb=True, name=tag + "_dx", tm=1024)
    dw = _matmul(x, dy, ta=True, name=tag + "_dw", tm=1408, tn=2176, tk=1024 if dy.shape[1] == 2 * D_FF else 512)
    return dx, dw, jnp.zeros_like(w_bf16)


pmm.defvjp(_pmm_fwd, _pmm_bwd)


def _all_gather(block, name):
    def body(x_ref, out_ref, send_sems, recv_sems, local_sem):
        me = _my_index()
        mine = pltpu.make_async_copy(x_ref, out_ref.at[me], local_sem)
        mine.start()
        sends = []
        for k in range(1, N_DEV):
            peer, _ = _peer(k)
            cp = pltpu.make_async_remote_copy(src_ref=x_ref, dst_ref=out_ref.at[me], send_sem=send_sems.at[k - 1],
                                              recv_sem=recv_sems.at[k - 1], device_id=peer, device_id_type=MESH)
            cp.start()
            sends.append(cp)
        for k in range(1, N_DEV):
            peer, p = _peer(k)
            pltpu.make_async_remote_copy(src_ref=x_ref, dst_ref=out_ref.at[p], send_sem=send_sems.at[k - 1],
                                         recv_sem=recv_sems.at[k - 1], device_id=peer, device_id_type=MESH).wait_recv()
        for cp in sends:
            cp.wait_send()
        mine.wait()

    return pl.pallas_call(
        body, name=name,
        out_shape=jax.ShapeDtypeStruct((N_DEV,) + block.shape, block.dtype),
        in_specs=[pl.BlockSpec(memory_space=pl.ANY)], out_specs=pl.BlockSpec(memory_space=pl.ANY),
        scratch_shapes=[pltpu.SemaphoreType.DMA((N_DEV - 1,)), pltpu.SemaphoreType.DMA((N_DEV - 1,)),
                        pltpu.SemaphoreType.DMA],
    )(block)


def _exchange_slots(slots, name):
    def body(x_ref, out_ref, send_sems, recv_sems, local_sem):
        me = _my_index()
        mine = pltpu.make_async_copy(x_ref.at[me], out_ref.at[me], local_sem)
        mine.start()
        sends = []
        for k in range(1, N_DEV):
            peer, p = _peer(k)
            cp = pltpu.make_async_remote_copy(src_ref=x_ref.at[p], dst_ref=out_ref.at[me], send_sem=send_sems.at[k - 1],
                                              recv_sem=recv_sems.at[k - 1], device_id=peer, device_id_type=MESH)
            cp.start()
            sends.append(cp)
        for k in range(1, N_DEV):
            peer, p = _peer(k)
            pltpu.make_async_remote_copy(src_ref=x_ref.at[p], dst_ref=out_ref.at[p], send_sem=send_sems.at[k - 1],
                                         recv_sem=recv_sems.at[k - 1], device_id=peer, device_id_type=MESH).wait_recv()
        for cp in sends:
            cp.wait_send()
        mine.wait()

    return pl.pallas_call(
        body, name=name,
        out_shape=jax.ShapeDtypeStruct(slots.shape, slots.dtype),
        in_specs=[pl.BlockSpec(memory_space=pl.ANY)], out_specs=pl.BlockSpec(memory_space=pl.ANY),
        scratch_shapes=[pltpu.SemaphoreType.DMA((N_DEV - 1,)), pltpu.SemaphoreType.DMA((N_DEV - 1,)),
                        pltpu.SemaphoreType.DMA],
    )(slots)


def _sum_slots(slots, name):
    shape = slots.shape[1:]
    w = shape[-1]
    slots = slots.reshape(N_DEV, -1, w)
    r = slots.shape[1]
    tr = _tile(r, 256, 8)

    def body(x_ref, o_ref):
        acc = x_ref[0].astype(F32)
        for p in range(1, N_DEV):
            acc = acc + x_ref[p].astype(F32)
        o_ref[...] = acc

    return pl.pallas_call(
        body, name=name, grid=(r // tr,),
        in_specs=[pl.BlockSpec((N_DEV, tr, w), lambda i: (0, i, 0))],
        out_specs=pl.BlockSpec((tr, w), lambda i: (i, 0)),
        out_shape=jax.ShapeDtypeStruct((r, w), F32),
        compiler_params=pltpu.CompilerParams(dimension_semantics=("parallel",), vmem_limit_bytes=V7X_VMEM_LIMIT_BYTES),
    )(slots).reshape(shape)


def _small_all_gather(v, reduce, name):
    rows, w = v.shape

    def body(x_ref, out_ref, land_ref, send_sems, recv_sems):
        me = _my_index()
        sends = []
        for k in range(1, N_DEV):
            peer, _ = _peer(k)
            cp = pltpu.make_async_remote_copy(src_ref=x_ref, dst_ref=land_ref.at[me], send_sem=send_sems.at[k - 1],
                                              recv_sem=recv_sems.at[k - 1], device_id=peer, device_id_type=MESH)
            cp.start()
            sends.append(cp)
        land_ref[me] = x_ref[...]
        for k in range(1, N_DEV):
            peer, p = _peer(k)
            pltpu.make_async_remote_copy(src_ref=x_ref, dst_ref=land_ref.at[p], send_sem=send_sems.at[k - 1],
                                         recv_sem=recv_sems.at[k - 1], device_id=peer, device_id_type=MESH).wait_recv()
        for cp in sends:
            cp.wait_send()
        if reduce:
            acc = land_ref[0]
            for p in range(1, N_DEV):
                acc = acc + land_ref[p]
            out_ref[...] = acc
        else:
            out_ref[...] = land_ref[...]

    out_shape = (rows, w) if reduce else (N_DEV, rows, w)
    return pl.pallas_call(
        body, name=name,
        out_shape=jax.ShapeDtypeStruct(out_shape, F32),
        in_specs=[pl.BlockSpec(memory_space=pltpu.VMEM)], out_specs=pl.BlockSpec(memory_space=pltpu.VMEM),
        scratch_shapes=[pltpu.VMEM((N_DEV, rows, w), F32), pltpu.SemaphoreType.DMA((N_DEV - 1,)),
                        pltpu.SemaphoreType.DMA((N_DEV - 1,))],
    )(v)


def _adamw(w, g, m, v, name):
    shape = w.shape
    cols = shape[-1] if len(shape) > 1 else shape[0]
    w2, g2, m2, v2 = (t.reshape(-1, cols) for t in (w, g, m, v))
    rows = w2.shape[0]
    tr = _tile(rows, 256, 8)

    def body(w_ref, g_ref, m_ref, v_ref, d_ref, mo_ref, vo_ref):
        gv = g_ref[...]
        mn = ADAM_B1 * m_ref[...] + (1.0 - ADAM_B1) * gv
        vn = ADAM_B2 * v_ref[...] + (1.0 - ADAM_B2) * (gv * gv)
        m_hat = mn / (1.0 - ADAM_B1 ** ADAM_STEP)
        v_hat = vn / (1.0 - ADAM_B2 ** ADAM_STEP)
        d_ref[...] = -ADAM_LR * (m_hat / (jnp.sqrt(v_hat) + ADAM_EPS) + ADAM_WD * w_ref[...])
        mo_ref[...] = mn
        vo_ref[...] = vn

    spec = pl.BlockSpec((tr, cols), lambda i: (i, 0))
    outs = pl.pallas_call(
        body, name=name, grid=(rows // tr,), in_specs=[spec] * 4, out_specs=[spec] * 3,
        out_shape=[jax.ShapeDtypeStruct((rows, cols), F32)] * 3,
        compiler_params=pltpu.CompilerParams(dimension_semantics=("parallel",), vmem_limit_bytes=V7X_VMEM_LIMIT_BYTES),
    )(w2, g2, m2, v2)
    return tuple(o.reshape(shape) for o in outs)


_DOT_DIMS = {2: dict(nn=NN, nt=NT, tn=TN),
             3: dict(nn=(((2,), (1,)), ((0,), (0,))), nt=(((2,), (2,)), ((0,), (0,))), tn=(((1,), (1,)), ((0,), (0,))))}


def _bf16_dot(a, b, kind):
    if a.ndim == 3:
        return jnp.stack([_bf16_dot(a[i], b[i] if b.ndim == 3 else b, kind) for i in range(a.shape[0])])
    return lax.dot_general(a.astype(BF16), b.astype(BF16), _DOT_DIMS[2][kind], preferred_element_type=F32)


def _f32_dot(a, b, kind):
    if a.ndim == 3:
        return jnp.stack([_f32_dot(a[i], b[i] if b.ndim == 3 else b, kind) for i in range(a.shape[0])])
    return lax.dot_general(a, b, _DOT_DIMS[2][kind], precision=HI, preferred_element_type=F32)


def _dot_family(raw):
    @jax.custom_vjp
    def nn(a, b):
        return raw(a, b, "nn")

    @jax.custom_vjp
    def nt(a, b):
        return raw(a, b, "nt")

    @jax.custom_vjp
    def tn(a, b):
        return raw(a, b, "tn")

    nn.defvjp(lambda a, b: (raw(a, b, "nn"), (a, b)), lambda r, ct: (nt(ct, r[1]), tn(r[0], ct)))
    nt.defvjp(lambda a, b: (raw(a, b, "nt"), (a, b)), lambda r, ct: (nn(ct, r[1]), tn(ct, r[0])))
    tn.defvjp(lambda a, b: (raw(a, b, "tn"), (a, b)), lambda r, ct: (nt(r[1], ct), nn(r[0], ct)))
    return nn, nt, tn


_bdot_nn, _bdot_nt, _bdot_tn = _dot_family(_bf16_dot)
_hdot_nn, _hdot_nt, _hdot_tn = _dot_family(_f32_dot)


def _f32x3_dot(a, b, kind):
    if a.ndim == 3:
        return jnp.stack([_f32x3_dot(a[i], b[i], kind) for i in range(a.shape[0])])
    return lax.dot_general(a, b, _DOT_DIMS[2][kind], precision=lax.Precision.HIGH, preferred_element_type=F32)


_mdot_nn, _mdot_nt, _mdot_tn = _dot_family(_f32x3_dot)


def _rows(width, t):
    return (t, width), lambda i: (i, 0)


def _whole(shape):
    return tuple(shape), lambda i: (0,) * len(shape)


def _tiled_op(name, fn, steps, in_blocks, kinds, outs, grad_blocks=None, grad_fix=None):
    n_in, n_out = len(in_blocks), len(outs)
    diff = [j for j, kd in enumerate(kinds) if kd != "const"]
    has_param = any(kd == "param" for kd in kinds)
    in_specs = [pl.BlockSpec(b, m) for b, m in in_blocks]
    out_specs = [pl.BlockSpec(b, m) for _, b, m in outs]

    def forward(*arrays):
        def body(*refs):
            res = fn(pl.program_id(0), *[r[...] for r in refs[:n_in]])
            for o_ref, val in zip(refs[n_in:], res):
                o_ref[...] = val

        return pl.pallas_call(
            body, name=name + "_fwd", grid=(steps,), in_specs=in_specs, out_specs=out_specs,
            out_shape=[jax.ShapeDtypeStruct(s, F32) for s, _, _ in outs],
            compiler_params=pltpu.CompilerParams(dimension_semantics=("parallel",),
                                                 vmem_limit_bytes=V7X_VMEM_LIMIT_BYTES),
        )(*arrays)

    def backward(arrays, douts):
        g_specs = []
        for j in diff:
            if kinds[j] == "param":
                g_specs.append((arrays[j].shape,) + _whole(arrays[j].shape))
            else:
                g_specs.append((grad_blocks or {}).get(j, (arrays[j].shape,) + in_blocks[j]))

        def body(*refs):
            i = pl.program_id(0)
            vals = [r[...] for r in refs[:n_in]]
            dos = tuple(r[...] for r in refs[n_in:n_in + n_out])
            g_refs = refs[n_in + n_out:]

            def of_diff(*dv):
                full = list(vals)
                for j, val in zip(diff, dv):
                    full[j] = val
                return tuple(fn(i, *full))

            _, vjp = jax.vjp(of_diff, *[vals[j] for j in diff])
            grads = vjp(dos)
            for j, g_ref, g in zip(diff, g_refs, grads):
                if kinds[j] == "param":
                    @pl.when(i == 0)
                    def _(g_ref=g_ref, g=g):
                        g_ref[...] = g

                    @pl.when(i != 0)
                    def _(g_ref=g_ref, g=g):
                        g_ref[...] += g
                else:
                    g_ref[...] = g

        return pl.pallas_call(
            body, name=name + "_bwd", grid=(steps,), in_specs=in_specs + out_specs,
            out_specs=[pl.BlockSpec(b, m) for _, b, m in g_specs],
            out_shape=[jax.ShapeDtypeStruct(shp, F32) for shp, _, _ in g_specs],
            compiler_params=pltpu.CompilerParams(dimension_semantics=("arbitrary" if has_param else "parallel",),
                                                 vmem_limit_bytes=V7X_VMEM_LIMIT_BYTES),
        )(*arrays, *douts)

    @jax.custom_vjp
    def op(*arrays):
        return tuple(forward(*arrays))

    def op_fwd(*arrays):
        return tuple(forward(*arrays)), arrays

    def op_bwd(arrays, douts):
        grads = backward(arrays, douts)
        full = [jnp.zeros_like(a) for a in arrays]
        for j, g in zip(diff, grads):
            full[j] = grad_fix[j](g) if grad_fix and j in grad_fix else g
        return tuple(full)

    op.defvjp(op_fwd, op_bwd)
    return op


def _sigmoid(x):
    return jax.nn.sigmoid(x)


def _silu(x):
    return x * jax.nn.sigmoid(x)


def _softplus(x):
    return jnp.maximum(x, 0.0) + jnp.log(1.0 + jnp.exp(-jnp.abs(x)))


def _heads(t, width):
    return [t[:, h * width:(h + 1) * width] for h in range(t.shape[1] // width)]


def _ffn_act(gu, name):
    s, two_f = gu.shape
    f = two_f // 2
    t = _tile(s, 256, 8)

    def fn(i, gu_t):
        return (_silu(gu_t[:, :f]) * gu_t[:, f:],)

    return _tiled_op(name, fn, s // t, [_rows(two_f, t)], ["row"], [((s, f),) + _rows(f, t)])(gu)[0]


def _res_ln(x, f, g, b, scale, name):
    s, d = x.shape
    t = _tile(s, 512, 8)

    def fn(i, x_t, f_t, g_t, b_t):
        r = DEEPNORM_ALPHA * x_t + scale * f_t
        mu = jnp.mean(r, axis=-1, keepdims=True)
        var = jnp.mean(jnp.square(r - mu), axis=-1, keepdims=True)
        return ((r - mu) * lax.rsqrt(var + LN_EPS) * g_t + b_t,)

    op = _tiled_op(name, fn, s // t, [_rows(d, t), _rows(d, t), _whole((1, d)), _whole((1, d))],
                   ["row", "row", "param", "param"], [((s, d),) + _rows(d, t)])
    return op(x, f, g.reshape(1, d), b.reshape(1, d))[0]


def _row_loss(y, target, name):
    s, d = y.shape
    t = _tile(s, 512, 8)

    def fn(i, y_t, t_t):
        e = y_t - t_t
        return (jnp.broadcast_to(0.5 * jnp.mean(e * e, axis=-1, keepdims=True), (t, LANES)),)

    return _tiled_op(name, fn, s // t, [_rows(d, t), _rows(d, t)], ["row", "const"],
                     [((s, LANES),) + _rows(LANES, t)])(y, target)[0]


def _gate_merge(gates, b0, b1, b2, name):
    s, d = b0.shape
    t = _tile(s, 256, 8)

    def fn(i, g_t, b0_t, b1_t, b2_t):
        return (_sigmoid(g_t[:, :d]) * b0_t + _sigmoid(g_t[:, d:2 * d]) * b1_t + _sigmoid(g_t[:, 2 * d:]) * b2_t,)

    op = _tiled_op(name, fn, s // t, [_rows(3 * d, t)] + [_rows(d, t)] * 3, ["row"] * 4, [((s, d),) + _rows(d, t)])
    return op(gates, b0, b1, b2)[0]


def _dn_pre(x, ba, conv_w, a_log, dt_bias, name):
    s, xw = x.shape
    w = DN_HEADS * DN_DK
    t = _tile(s, 256, 8)
    steps = s // t
    lane_pad = jnp.zeros((LANES - 2 * DN_HEADS,), F32)
    pa = jnp.concatenate([jnp.zeros((DN_HEADS,), F32), a_log, lane_pad]).reshape(1, LANES)
    pd = jnp.concatenate([jnp.zeros((DN_HEADS,), F32), dt_bias, lane_pad]).reshape(1, LANES)
    cw = jnp.concatenate([conv_w, jnp.zeros((8 - DN_CONV, xw), F32)], axis=0)
    taps = [jnp.pad(x, ((DN_CONV - 1 - j, 0), (0, 0)))[:s] for j in range(DN_CONV - 1)] + [x]

    def unit(x):
        return x * lax.rsqrt(jnp.sum(x * x, axis=-1, keepdims=True) + RMS_EPS)

    def fn(i, x0_t, x1_t, x2_t, x3_t, ba_t, cw_t, pa_t, pd_t):
        tap = lax.broadcasted_iota(jnp.int32, cw_t.shape, 0)
        c_t = sum(x_t * jnp.sum(jnp.where(tap == j, cw_t, 0.0), axis=0, keepdims=True)
                  for j, x_t in enumerate((x0_t, x1_t, x2_t, x3_t)))
        act = _silu(c_t)
        q = jnp.concatenate([unit(x) for x in _heads(act[:, :w], DN_DK)], axis=1)
        k = jnp.concatenate([unit(x) for x in _heads(act[:, w:2 * w], DN_DK)], axis=1)
        beta = _sigmoid(ba_t)
        g = -jnp.exp(pa_t) * _softplus(ba_t + pd_t)
        src = lax.broadcasted_iota(jnp.int32, (LANES, w), 0)
        head = lax.broadcasted_iota(jnp.int32, (LANES, w), 1) >> int(np.log2(DN_DK))
        be = _hdot_nn(beta, (src == head).astype(F32))
        ge = _hdot_nn(g, (src == head + DN_HEADS).astype(F32))
        return q, k, act[:, 2 * w:], be, ge

    op = _tiled_op(name, fn, steps,
                   [_rows(xw, t)] * DN_CONV + [_rows(LANES, t), _whole((8, xw)), _whole((1, LANES)), _whole((1, LANES))],
                   ["row"] * (DN_CONV + 1) + ["param"] * 3, [((s, w),) + _rows(w, t)] * 5)
    return op(*taps, ba, cw, pa, pd)


def _dn_post(o, z, norm_w, name):
    s, w = o.shape
    t = _tile(s, 512, 8)

    def fn(i, o_t, z_t, nw_t):
        outs = [x * lax.rsqrt(jnp.mean(x * x, axis=-1, keepdims=True) + RMS_EPS) * nw_t * _silu(zz)
                for x, zz in zip(_heads(o_t, DN_DV), _heads(z_t, DN_DV))]
        return (jnp.concatenate(outs, axis=1),)

    op = _tiled_op(name, fn, s // t, [_rows(w, t), _rows(w, t), _whole((1, DN_DV))], ["row", "row", "param"],
                   [((s, w),) + _rows(w, t)])
    return op(o, z, norm_w.reshape(1, DN_DV))[0]


def _xattn(q, k_mem, v_mem, name):
    s, w = q.shape
    t = _tile(s, 512, 8)

    def fn(i, q_t, k_t, v_t):
        outs = []
        for qh, kh, vh in zip(_heads(q_t, XA_DH), _heads(k_t, XA_DH), _heads(v_t, XA_DH)):
            sc = _bdot_nt(qh, kh) * (XA_DH ** -0.5)
            e = jnp.exp(sc - jnp.max(sc, axis=-1, keepdims=True))
            outs.append(_bdot_nn(e / jnp.sum(e, axis=-1, keepdims=True), vh))
        return (jnp.concatenate(outs, axis=1),)

    op = _tiled_op(name, fn, s // t, [_rows(w, t), _whole(k_mem.shape), _whole(v_mem.shape)],
                   ["row", "param", "param"], [((s, w),) + _rows(w, t)])
    return op(q, k_mem, v_mem)[0]


def _swa(q, kv, sinks, name):
    s, w = q.shape
    wd = WINDOW
    grp = SWA_HEADS // SWA_KV_HEADS
    dh = SWA_DH
    steps = s // wd
    sink_rows = jnp.broadcast_to(sinks[:, None], (SWA_HEADS, LANES))

    def fn(i, q_t, kc_t, kp_t, sk_t):
        qr = lax.broadcasted_iota(jnp.int32, (grp * wd, 2 * wd), 0) & (wd - 1)
        kc = lax.broadcasted_iota(jnp.int32, (grp * wd, 2 * wd), 1)
        dist = qr + wd - kc
        mask = (dist >= 0) & (dist < wd) & ((kc >= wd) | (i > 0))
        head_row = lax.broadcasted_iota(jnp.int32, (SWA_HEADS, LANES), 0)
        outs = [None] * SWA_HEADS
        for hk in range(SWA_KV_HEADS):
            kh = jnp.concatenate([kp_t[:, hk * dh:(hk + 1) * dh], kc_t[:, hk * dh:(hk + 1) * dh]], axis=0)
            vh = jnp.concatenate([kp_t[:, (SWA_KV_HEADS + hk) * dh:(SWA_KV_HEADS + hk + 1) * dh],
                                  kc_t[:, (SWA_KV_HEADS + hk) * dh:(SWA_KV_HEADS + hk + 1) * dh]], axis=0)
            qg = jnp.concatenate([q_t[:, (hk * grp + g) * dh:(hk * grp + g + 1) * dh] for g in range(grp)], axis=0)
            sink = jnp.concatenate(
                [jnp.broadcast_to(jnp.sum(jnp.where(head_row == hk * grp + g, sk_t, 0.0), axis=0, keepdims=True),
                                  (wd, LANES)) for g in range(grp)], axis=0)
            sink = jnp.max(sink, axis=-1, keepdims=True)
            sc = jnp.where(mask, _bdot_nt(qg, kh) * (dh ** -0.5), NEG_INF)
            m = jnp.maximum(jnp.max(sc, axis=-1, keepdims=True), sink)
            p = jnp.exp(sc - m)
            p = p / (jnp.sum(p, axis=-1, keepdims=True) + jnp.exp(sink - m))
            o = _bdot_nn(p, vh)
            for g in range(grp):
                outs[hk * grp + g] = o[g * wd:(g + 1) * wd]
        return (jnp.concatenate(outs, axis=1),)

    kvw = kv.shape[1]
    prev_block = ((wd, kvw), lambda i: (jnp.maximum(i - 1, 0), 0))
    shift = lambda g: jnp.concatenate([g[wd:], jnp.zeros((wd, kvw), F32)], axis=0)
    op = _tiled_op(name, fn, steps, [_rows(w, wd), _rows(kvw, wd), prev_block, _whole((SWA_HEADS, LANES))],
                   ["row", "row", "row", "param"], [((s, w),) + _rows(w, wd)],
                   grad_blocks={2: (kv.shape,) + _rows(kvw, wd)}, grad_fix={2: shift})
    return op(q, kv, kv, sink_rows)[0]


def _dn_prepare(q, k, v, be, ge):
    b, c, dk = q.shape
    row = lax.broadcasted_iota(jnp.int32, (c, c), 0)
    col = lax.broadcasted_iota(jnp.int32, (c, c), 1)
    tril = row >= col
    gc = _hdot_nn(jnp.broadcast_to(tril.astype(F32), (b, c, c)), ge)
    g_col = _hdot_nn(gc, jnp.full((b, LANES, c), 1.0 / LANES, F32))
    g_row = _hdot_nt(jnp.full((b, c, LANES), 1.0 / LANES, F32), gc)
    decay = jnp.where(tril, jnp.exp(jnp.where(tril, g_col - g_row, 0.0)), 0.0)
    kb = k * be
    a = jnp.where(row > col, _bdot_nt(kb, k) * decay, 0.0)
    x = jnp.broadcast_to((row == col).astype(F32), (b, c, c))
    for lvl in range(int(np.log2(c))):
        off = ((row >> lvl) == (col >> lvl) + 1) & ((row >> (lvl + 1)) == (col >> (lvl + 1)))
        x = x - _mdot_nn(x, _mdot_nn(jnp.where(off, a, 0.0), x))
    eg = jnp.exp(gc)
    qs = q * (dk ** -0.5)
    qk = jnp.where(tril, _bdot_nt(qs, k) * decay, 0.0)
    gl_c = _hdot_nn(jnp.ones((b, c, c), F32), ge)
    gl_s = _hdot_nn(jnp.ones((b, dk, c), F32), ge)
    return _mdot_nn(x, v * be), _mdot_nn(x, kb * eg), qs * eg, k * jnp.exp(gl_c - gc), qk, jnp.exp(gl_s)


def _dn_stack(ref, chunks, heads):
    return jnp.stack([ref[ci * DN_CHUNK:(ci + 1) * DN_CHUNK, h * DN_DK:(h + 1) * DN_DK]
                      for ci in range(chunks) for h in range(heads)])


def _dn_step(u, w, q_dec, k_tail, qk, decay, s):
    v_new = u - _bdot_nn(w, s)
    o = _bdot_nn(q_dec, s) + _bdot_nn(qk, v_new)
    return o, s * decay + _bdot_tn(k_tail, v_new)


def _hdot(a, b, dims=NN):
    return lax.dot_general(a, b, dims, precision=HI, preferred_element_type=F32)


def _mdot(a, b, dims=NN):
    return lax.dot_general(a, b, dims, precision=lax.Precision.HIGH, preferred_element_type=F32)


def _dn_chunk_heads(q, k, v, be, ge, s):
    c, dk = q[0].shape
    row = lax.broadcasted_iota(jnp.int32, (c, c), 0)
    col = lax.broadcasted_iota(jnp.int32, (c, c), 1)
    tril = row >= col
    trilf = tril.astype(F32)
    mean_rows = jnp.full((c, LANES), 1.0 / LANES, F32)
    gc = [_hdot(trilf, g) for g in ge]
    g_row = [_hdot(mean_rows, g, NT) for g in gc]
    decay = [jnp.where(tril, jnp.exp(jnp.where(tril, g[:, :c] - gr, 0.0)), 0.0) for g, gr in zip(gc, g_row)]
    kb = [ki * bi for ki, bi in zip(k, be)]
    a = [jnp.where(row > col, _bdot_nt(x_, y_) * d, 0.0) for x_, y_, d in zip(kb, k, decay)]
    x = [(row == col).astype(F32)] * len(q)
    for lvl in range(int(np.log2(c))):
        off = ((row >> lvl) == (col >> lvl) + 1) & ((row >> (lvl + 1)) == (col >> (lvl + 1)))
        t = [_mdot(jnp.where(off, ai, 0.0), xi) for ai, xi in zip(a, x)]
        x = [xi - _mdot(xi, ti) for xi, ti in zip(x, t)]
    eg = [jnp.exp(g) for g in gc]
    u = [_mdot(xi, vi * bi) for xi, vi, bi in zip(x, v, be)]
    w = [_mdot(xi, ki * ei) for xi, ki, ei in zip(x, kb, eg)]
    qs = [qi * (dk ** -0.5) for qi in q]
    qk = [jnp.where(tril, _bdot_nt(x_, y_) * d, 0.0) for x_, y_, d in zip(qs, k, decay)]
    gl = [jnp.sum(g, axis=0, keepdims=True) for g in ge]
    v_new = [ui - _bdot_nn(wi, si) for ui, wi, si in zip(u, w, s)]
    o = [_bdot_nn(qi * ei, si) + _bdot_nn(ai, vi) for qi, ei, si, ai, vi in zip(qs, eg, s, qk, v_new)]
    s_next = [si * jnp.exp(g) + _bdot_tn(ki * jnp.exp(g - gi), vi)
              for si, g, ki, gi, vi in zip(s, gl, k, gc, v_new)]
    return o, s_next


def _dn_forward(q, k, v, be, ge, name, chunks_per_step=4):
    s_len, width = q.shape
    heads = width // DN_DK
    n = s_len // DN_CHUNK
    cb = min(chunks_per_step, n)
    rows = cb * DN_CHUNK

    def body(q_ref, k_ref, v_ref, be_ref, ge_ref, o_ref, st_ref, s_scr):
        @pl.when(pl.program_id(0) == 0)
        def _():
            s_scr[...] = jnp.zeros_like(s_scr)

        parts = _dn_prepare(*[_dn_stack(ref, cb, heads) for ref in (q_ref, k_ref, v_ref, be_ref, ge_ref)])
        for ci in range(cb):
            s = s_scr[...]
            o, s_next = _dn_step(*[t[ci * heads:(ci + 1) * heads] for t in parts], s)
            for h in range(heads):
                st_ref[ci, h] = s[h]
                o_ref[ci * DN_CHUNK:(ci + 1) * DN_CHUNK, h * DN_DK:(h + 1) * DN_DK] = o[h]
            s_scr[...] = s_next

    tok = pl.BlockSpec((rows, width), lambda i: (i, 0))
    return pl.pallas_call(
        body, name=name + "_fwd", grid=(n // cb,), in_specs=[tok] * 5,
        out_specs=[tok, pl.BlockSpec((cb, heads, DN_DK, DN_DV), lambda i: (i, 0, 0, 0))],
        out_shape=[jax.ShapeDtypeStruct((s_len, width), F32), jax.ShapeDtypeStruct((n, heads, DN_DK, DN_DV), F32)],
        scratch_shapes=[pltpu.VMEM((heads, DN_DK, DN_DV), F32)],
        compiler_params=pltpu.CompilerParams(dimension_semantics=("arbitrary",), vmem_limit_bytes=V7X_VMEM_LIMIT_BYTES),
    )(q, k, v, be, ge)


def _dn_backward(q, k, v, be, ge, states, do, name, chunks_per_step=4):
    s_len, width = q.shape
    heads = width // DN_DK
    n = s_len // DN_CHUNK
    cb = min(chunks_per_step, n)
    rows = cb * DN_CHUNK
    steps = n // cb

    def body(q_ref, k_ref, v_ref, be_ref, ge_ref, st_ref, do_ref, dq_ref, dk_ref, dv_ref, dbe_ref, dge_ref, ds_scr):
        @pl.when(pl.program_id(0) == 0)
        def _():
            ds_scr[...] = jnp.zeros_like(ds_scr)

        lanes = [slice(h * DN_DK, (h + 1) * DN_DK) for h in range(heads)]
        for ci in reversed(range(cb)):
            r = slice(ci * DN_CHUNK, (ci + 1) * DN_CHUNK)
            _, vjp = jax.vjp(_dn_chunk_heads, *[[ref[r, l] for l in lanes]
                                                for ref in (q_ref, k_ref, v_ref, be_ref, ge_ref)],
                             [st_ref[ci, h] for h in range(heads)])
            *grads, ds = vjp(([do_ref[r, l] for l in lanes], [ds_scr[h] for h in range(heads)]))
            for g_ref, g in zip((dq_ref, dk_ref, dv_ref, dbe_ref, dge_ref), grads):
                for l, g_head in zip(lanes, g):
                    g_ref[r, l] = g_head
            for h in range(heads):
                ds_scr[h] = ds[h]

    tok = pl.BlockSpec((rows, width), lambda i: (steps - 1 - i, 0))
    return pl.pallas_call(
        body, name=name + "_bwd", grid=(steps,),
        in_specs=[tok] * 5 + [pl.BlockSpec((cb, heads, DN_DK, DN_DV), lambda i: (steps - 1 - i, 0, 0, 0)), tok],
        out_specs=[tok] * 5,
        out_shape=[jax.ShapeDtypeStruct((s_len, width), F32)] * 5,
        scratch_shapes=[pltpu.VMEM((heads, DN_DK, DN_DV), F32)],
        compiler_params=pltpu.CompilerParams(dimension_semantics=("arbitrary",), vmem_limit_bytes=V7X_VMEM_LIMIT_BYTES),
    )(q, k, v, be, ge, states, do)


@functools.partial(jax.custom_vjp, nondiff_argnums=(5,))
def _dn_core(q, k, v, be, ge, name):
    return _dn_forward(q, k, v, be, ge, name)[0]


def _dn_core_fwd(q, k, v, be, ge, name):
    o, states = _dn_forward(q, k, v, be, ge, name)
    return o, (q, k, v, be, ge, states)


def _dn_core_bwd(name, res, do):
    return tuple(_dn_backward(*res, do, name))


_dn_core.defvjp(_dn_core_fwd, _dn_core_bwd)


def _layer_norm(x, g, b):
    mu = x.mean(-1, keepdims=True)
    var = jnp.square(x - mu).mean(-1, keepdims=True)
    return (x - mu) * lax.rsqrt(var + LN_EPS) * g + b


def _hybrid_layer(x, mem_n, p, p_bf16, l):
    tag = f"l{l}"
    ln_g, ln_b = p["ln_g"][l], p["ln_b"][l]

    def mm(a, name, tag_, *idx):
        return pmm(a, p[name][(l,) + idx], p_bf16[name][(l,) + idx], tag_)

    def swiglu(a, ffn):
        return mm(_ffn_act(mm(a, ffn + "_w_gu", f"{tag}_{ffn}_gu"), f"{tag}_{ffn}_act"), ffn + "_w_down",
                  f"{tag}_{ffn}_down")

    h = _res_ln(x, swiglu(x, "ffn1"), ln_g[0], ln_b[0], 0.5, tag + "_ln0")
    proj = mm(h, "w_in", tag + "_in")
    seg = {name: proj[:, lo:hi] for name, (lo, hi) in SEG.items()}
    q, k, v, be, ge = _dn_pre(seg["dn_qkv"], seg["dn_ba"], p["dn_conv_w"][l], p["dn_a_log"][l], p["dn_dt_bias"][l],
                              tag + "_dn_pre")
    o_dn = _dn_post(_dn_core(q, k, v, be, ge, tag + "_dn"), seg["dn_z"], p["dn_norm_w"][l], tag + "_dn_post")
    o_sw = _swa(seg["sw_q"], seg["sw_kv"], p["swa_sinks"][l], tag + "_swa")
    kv_mem = mm(mem_n, "w_mem_kv", tag + "_memkv")
    half = XA_HEADS * XA_DH
    o_xa = _xattn(seg["xa_q"], kv_mem[:, :half], kv_mem[:, half:], tag + "_xattn")
    branch = [mm(o, "w_branch", tag + f"_branch{i}", i) for i, o in enumerate((o_dn, o_sw, o_xa))]
    merged = _gate_merge(seg["gates"], *branch, tag + "_merge")
    h = _res_ln(h, mm(merged, "w_out", tag + "_out"), ln_g[1], ln_b[1], 1.0, tag + "_ln1")
    return _res_ln(h, swiglu(h, "ffn2"), ln_g[2], ln_b[2], 0.5, tag + "_ln2")


def _local_loss(p, x, mem, target, p_bf16):
    mem_n = _layer_norm(mem, p["mem_ln_g"], p["mem_ln_b"])
    y = x
    for l in range(DEPTH):
        y = _hybrid_layer(y, mem_n, p, p_bf16, l)
    return jnp.sum(_row_loss(y, target, "row_loss")[:, 0])


def _w_in_permute(w):
    pad = jnp.zeros(w.shape[:-1] + (D_IN_PAD - D_IN,), w.dtype)
    return jnp.concatenate([w[..., :1536], w[..., 1544:], w[..., 1536:1544], pad], axis=-1)


def _w_in_unpermute(w):
    return jnp.concatenate([w[..., :1536], w[..., 6400:6408], w[..., 1536:6400]], axis=-1)


def _join_shards(gathered, ax):
    shp = gathered.shape[1:]
    t = jnp.moveaxis(gathered, 0, ax)
    return t.reshape(shp[:ax] + (N_DEV * shp[ax],) + shp[ax + 1:])


def _split_shards(whole, ax):
    shp = whole.shape
    t = whole.reshape(shp[:ax] + (N_DEV, shp[ax] // N_DEV) + shp[ax + 1:])
    return jnp.moveaxis(t, ax, 0)


def _pack_small(parts):
    flat = jnp.concatenate([t.reshape(-1).astype(F32) for t in parts])
    rows = -(-flat.shape[0] // LANES)
    rows = -(-rows // 8) * 8
    return jnp.concatenate([flat, jnp.zeros((rows * LANES - flat.shape[0],), F32)]).reshape(rows, LANES)


def _unpack_small(packed, shapes):
    flat = packed.reshape(packed.shape[:-2] + (-1,))
    out, off = [], 0
    for shp in shapes:
        size = int(np.prod(shp))
        out.append(flat[..., off:off + size].reshape(flat.shape[:-1] + tuple(shp)))
        off += size
    return out


def kernel(x, mem, mem_ln_g, mem_ln_b, ln_g, ln_b, ffn1_w_gu, ffn1_w_down, w_in, dn_conv_w, dn_a_log, dn_dt_bias, dn_norm_w, swa_sinks, w_mem_kv, w_branch, w_out, ffn2_w_gu, ffn2_w_down, loss_target, m_mem_ln_g, m_mem_ln_b, m_ln_g, m_ln_b, m_ffn1_w_gu, m_ffn1_w_down, m_w_in, m_dn_conv_w, m_dn_a_log, m_dn_dt_bias, m_dn_norm_w, m_swa_sinks, m_w_mem_kv, m_w_branch, m_w_out, m_ffn2_w_gu, m_ffn2_w_down, v_mem_ln_g, v_mem_ln_b, v_ln_g, v_ln_b, v_ffn1_w_gu, v_ffn1_w_down, v_w_in, v_dn_conv_w, v_dn_a_log, v_dn_dt_bias, v_dn_norm_w, v_swa_sinks, v_w_mem_kv, v_w_branch, v_w_out, v_ffn2_w_gu, v_ffn2_w_down):
    given = dict(locals())
    w_loc = {n: given[n] for n in WEIGHTS}
    m_loc = {n: given["m_" + n] for n in WEIGHTS}
    v_loc = {n: given["v_" + n] for n in WEIGHTS}
    me = _my_index()

    params_bf16 = {n: _join_shards(_all_gather(w_loc[n].astype(BF16), "all_gather_" + n), BIG_AXIS[n]) for n in BIG}
    params_bf16["w_in"] = _w_in_permute(params_bf16["w_in"])
    params = {n: params_bf16[n].astype(F32) for n in BIG}
    small_shard_shapes = [w_loc[n].shape for n in SMALL_SHARDED]
    small_gathered = _small_all_gather(_pack_small([w_loc[n] for n in SMALL_SHARDED]), False, "all_gather_small")
    for n, t in zip(SMALL_SHARDED, _unpack_small(small_gathered, small_shard_shapes)):
        t = jnp.moveaxis(t, 0, -2)
        params[n] = t.reshape(t.shape[:-2] + (-1,))
    for n in SMALL_REPL:
        params[n] = w_loc[n]

    loss_local, (g_params, grad_x) = jax.value_and_grad(_local_loss, argnums=(0, 1))(
        params, x[0], mem[0], loss_target[0], params_bf16)
    grad_x = grad_x[None]
    g_params["w_in"] = _w_in_unpermute(g_params["w_in"])

    g_big = {n: _sum_slots(_exchange_slots(_split_shards(g_params[n], BIG_AXIS[n]).astype(BF16),
                                           "exchange_grad_" + n), "sum_grad_" + n) for n in BIG}
    small_names = SMALL_SHARDED + SMALL_REPL
    small_full_shapes = [g_params[n].shape for n in small_names] + [()]
    small_sum = _small_all_gather(_pack_small([g_params[n] for n in small_names] + [loss_local]), True, "all_sum_small")
    small_full = dict(zip(small_names + ("loss",), _unpack_small(small_sum, small_full_shapes)))
    loss = small_full["loss"]
    grads = dict(g_big)
    for n in SMALL_SHARDED:
        shard = w_loc[n].shape[-1]
        grads[n] = lax.dynamic_slice_in_dim(small_full[n], me * shard, shard, axis=small_full[n].ndim - 1)
    for n in SMALL_REPL:
        grads[n] = small_full[n]

    delta, new_m, new_v = {}, {}, {}
    for n in BIG:
        delta[n], new_m[n], new_v[n] = _adamw(w_loc[n], grads[n], m_loc[n], v_loc[n], "adamw_" + n)
    shapes = [w_loc[n].shape for n in small_names]
    packed = [_pack_small([src[n] for n in small_names]) for src in (w_loc, grads, m_loc, v_loc)]
    for dst, t in zip((delta, new_m, new_v), _adamw(*packed, "adamw_small")):
        dst.update(zip(small_names, _unpack_small(t, shapes)))

    return (loss, grad_x, *[grads[n] for n in WEIGHTS], *[delta[n] for n in WEIGHTS],
            *[new_m[n] for n in WEIGHTS], *[new_v[n] for n in WEIGHTS])
```

```python
import functools

import numpy as np
import jax
import jax.numpy as jnp
from jax import lax
from jax.experimental import pallas as pl
from jax.experimental.pallas import tpu as pltpu

F32 = jnp.float32
BF16 = jnp.bfloat16
MESH = pl.DeviceIdType.MESH
N_DEV = 8
V7X_VMEM_LIMIT_BYTES = 56 * 1024 * 1024
LANES = 128
PACK_W = 1024

D_MODEL = 1024
DEPTH = 2
DN_HEADS, DN_DK, DN_DV, DN_CONV, DN_CHUNK = 4, 128, 128, 4, 64
SWA_HEADS, SWA_KV_HEADS, SWA_DH, WINDOW = 8, 2, 64, 128
XA_HEADS, XA_DH = 4, 128
D_FF = 2816
N_BRANCH, BRANCH_W = 3, 512
DEEPNORM_ALPHA = (2 * DEPTH) ** 0.25
LN_EPS = 1e-5
RMS_EPS = 1e-6
NEG_INF = -1e30
D_IN = 6408
D_IN_PAD = 6528
SEG = dict(dn_qkv=(0, 1536), dn_z=(1536, 2048), sw_q=(2048, 2560), sw_kv=(2560, 2816), xa_q=(2816, 3328),
           gates=(3328, 6400), dn_ba=(6400, 6528))

ADAM_LR, ADAM_B1, ADAM_B2, ADAM_EPS, ADAM_WD, ADAM_STEP = 0.001, 0.9, 0.999, 1e-08, 0.01, 10

BIG = ("ffn1_w_gu", "ffn1_w_down", "w_in", "w_mem_kv", "w_branch", "w_out", "ffn2_w_gu", "ffn2_w_down")
BIG_AXIS = dict(ffn1_w_gu=2, ffn1_w_down=1, w_in=2, w_mem_kv=1, w_branch=3, w_out=1, ffn2_w_gu=2, ffn2_w_down=1)
SMALL_SHARDED = ("ln_g", "ln_b", "dn_conv_w")
SMALL_REPL = ("mem_ln_g", "mem_ln_b", "dn_a_log", "dn_dt_bias", "dn_norm_w", "swa_sinks")
WEIGHTS = ("mem_ln_g", "mem_ln_b", "ln_g", "ln_b", "ffn1_w_gu", "ffn1_w_down", "w_in", "dn_conv_w", "dn_a_log",
           "dn_dt_bias", "dn_norm_w", "swa_sinks", "w_mem_kv", "w_branch", "w_out", "ffn2_w_gu", "ffn2_w_down")

HI = lax.Precision.HIGHEST
NN = (((1,), (0,)), ((), ()))
NT = (((1,), (1,)), ((), ()))
TN = (((0,), (0,)), ((), ()))


def _tile(n, target, align):
    best = None
    for d in range(align, min(n, target) + 1, align):
        if n % d == 0:
            best = d
    return n if best is None else best


def _my_index():
    return 4 * lax.axis_index("x") + 2 * lax.axis_index("y") + lax.axis_index("c")


def _peer(k):
    x, y, c = lax.axis_index("x"), lax.axis_index("y"), lax.axis_index("c")
    px = 1 - x if (k >> 2) & 1 else x
    py = 1 - y if (k >> 1) & 1 else y
    pc = 1 - c if k & 1 else c
    return (px, py, pc), 4 * px + 2 * py + pc


def _matmul(a, b, *, ta=False, tb=False, name, tm=512, tn=2176, tk=2304):
    m, k = (a.shape[1], a.shape[0]) if ta else a.shape
    k2, n = (b.shape[1], b.shape[0]) if tb else b.shape
    assert k == k2, (a.shape, b.shape, ta, tb)
    tm = _tile(m, tm, LANES if ta else 8)
    tn = _tile(n, tn, LANES)
    tk = _tile(k, tk, LANES)
    nk = k // tk
    dims = (((0 if ta else 1,), (1 if tb else 0,)), ((), ()))

    def body(a_ref, b_ref, o_ref):
        kk = pl.program_id(2)
        p = lax.dot_general(a_ref[...].astype(BF16), b_ref[...].astype(BF16), dims, preferred_element_type=F32)

        @pl.when(kk == 0)
        def _():
            o_ref[...] = p

        @pl.when(kk != 0)
        def _():
            o_ref[...] += p

    a_spec = pl.BlockSpec((tk, tm), lambda j, i, kk: (kk, i)) if ta else pl.BlockSpec((tm, tk), lambda j, i, kk: (i, kk))
    b_spec = pl.BlockSpec((tn, tk), lambda j, i, kk: (j, kk)) if tb else pl.BlockSpec((tk, tn), lambda j, i, kk: (kk, j))
    return pl.pallas_call(
        body, name=name, grid=(n // tn, m // tm, nk), in_specs=[a_spec, b_spec],
        out_specs=pl.BlockSpec((tm, tn), lambda j, i, kk: (i, j)),
        out_shape=jax.ShapeDtypeStruct((m, n), F32),
        compiler_params=pltpu.CompilerParams(dimension_semantics=("parallel", "parallel", "arbitrary"),
                                             vmem_limit_bytes=V7X_VMEM_LIMIT_BYTES),
    )(a, b)


@functools.partial(jax.custom_vjp, nondiff_argnums=(3,))
def pmm(x, w, w_bf16, tag):
    return _pmm_forward(x, w_bf16, tag)


def _pmm_forward(x, w_bf16, tag):
    n = w_bf16.shape[1]
    return _matmul(x, w_bf16, name=tag + "_fwd", tm=256 if n > 2176 else 1024, tn=n)


def _pmm_fwd(x, w, w_bf16, tag):
    return _pmm_forward(x, w_bf16, tag), (x, w_bf16)


def _pmm_bwd(tag, res, dy):
    x, w_bf16 = res
    dx = _matmul(dy, w_bf16, tb=True, name=tag + "_dx", tm=1024)
    dw = _matmul(x, dy, ta=True, name=tag + "_dw", tm=1408, tn=2176, tk=1024 if dy.shape[1] == 2 * D_FF else 512)
    return dx, dw, jnp.zeros_like(w_bf16)


pmm.defvjp(_pmm_fwd, _pmm_bwd)


def _all_gather(block, name):
    def body(x_ref, out_ref, send_sems, recv_sems, local_sem):
        me = _my_index()
        mine = pltpu.make_async_copy(x_ref, out_ref.at[me], local_sem)
        mine.start()
        sends = []
        for k in range(1, N_DEV):
            peer, _ = _peer(k)
            cp = pltpu.make_async_remote_copy(src_ref=x_ref, dst_ref=out_ref.at[me], send_sem=send_sems.at[k - 1],
                                              recv_sem=recv_sems.at[k - 1], device_id=peer, device_id_type=MESH)
            cp.start()
            sends.append(cp)
        for k in range(1, N_DEV):
            peer, p = _peer(k)
            pltpu.make_async_remote_copy(src_ref=x_ref, dst_ref=out_ref.at[p], send_sem=send_sems.at[k - 1],
                                         recv_sem=recv_sems.at[k - 1], device_id=peer, device_id_type=MESH).wait_recv()
        for cp in sends:
            cp.wait_send()
        mine.wait()

    return pl.pallas_call(
        body, name=name,
        out_shape=jax.ShapeDtypeStruct((N_DEV,) + block.shape, block.dtype),
        in_specs=[pl.BlockSpec(memory_space=pl.ANY)], out_specs=pl.BlockSpec(memory_space=pl.ANY),
        scratch_shapes=[pltpu.SemaphoreType.DMA((N_DEV - 1,)), pltpu.SemaphoreType.DMA((N_DEV - 1,)),
                        pltpu.SemaphoreType.DMA],
    )(block)


def _exchange_slots(slots, name):
    def body(x_ref, out_ref, send_sems, recv_sems, local_sem):
        me = _my_index()
        mine = pltpu.make_async_copy(x_ref.at[me], out_ref.at[me], local_sem)
        mine.start()
        sends = []
        for k in range(1, N_DEV):
            peer, p = _peer(k)
            cp = pltpu.make_async_remote_copy(src_ref=x_ref.at[p], dst_ref=out_ref.at[me], send_sem=send_sems.at[k - 1],
                                              recv_sem=recv_sems.at[k - 1], device_id=peer, device_id_type=MESH)
            cp.start()
            sends.append(cp)
        for k in range(1, N_DEV):
            peer, p = _peer(k)
            pltpu.make_async_remote_copy(src_ref=x_ref.at[p], dst_ref=out_ref.at[p], send_sem=send_sems.at[k - 1],
                                         recv_sem=recv_sems.at[k - 1], device_id=peer, device_id_type=MESH).wait_recv()
        for cp in sends:
            cp.wait_send()
        mine.wait()

    return pl.pallas_call(
        body, name=name,
        out_shape=jax.ShapeDtypeStruct(slots.shape, slots.dtype),
        in_specs=[pl.BlockSpec(memory_space=pl.ANY)], out_specs=pl.BlockSpec(memory_space=pl.ANY),
        scratch_shapes=[pltpu.SemaphoreType.DMA((N_DEV - 1,)), pltpu.SemaphoreType.DMA((N_DEV - 1,)),
                        pltpu.SemaphoreType.DMA],
    )(slots)


def _sum_slots(slots, name):
    shape = slots.shape[1:]
    w = shape[-1]
    slots = slots.reshape(N_DEV, -1, w)
    r = slots.shape[1]
    tr = _tile(r, 256, 8)

    def body(x_ref, o_ref):
        acc = x_ref[0].astype(F32)
        for p in range(1, N_DEV):
            acc = acc + x_ref[p].astype(F32)
        o_ref[...] = acc

    return pl.pallas_call(
        body, name=name, grid=(r // tr,),
        in_specs=[pl.BlockSpec((N_DEV, tr, w), lambda i: (0, i, 0))],
        out_specs=pl.BlockSpec((tr, w), lambda i: (i, 0)),
        out_shape=jax.ShapeDtypeStruct((r, w), F32),
        compiler_params=pltpu.CompilerParams(dimension_semantics=("parallel",), vmem_limit_bytes=V7X_VMEM_LIMIT_BYTES),
    )(slots).reshape(shape)


def _small_all_gather(v, reduce, name):
    rows, w = v.shape

    def body(x_ref, out_ref, land_ref, send_sems, recv_sems):
        me = _my_index()
        sends = []
        for k in range(1, N_DEV):
            peer, _ = _peer(k)
            cp = pltpu.make_async_remote_copy(src_ref=x_ref, dst_ref=land_ref.at[me], send_sem=send_sems.at[k - 1],
                                              recv_sem=recv_sems.at[k - 1], device_id=peer, device_id_type=MESH)
            cp.start()
            sends.append(cp)
        land_ref[me] = x_ref[...]
        for k in range(1, N_DEV):
            peer, p = _peer(k)
            pltpu.make_async_remote_copy(src_ref=x_ref, dst_ref=land_ref.at[p], send_sem=send_sems.at[k - 1],
                                         recv_sem=recv_sems.at[k - 1], device_id=peer, device_id_type=MESH).wait_recv()
        for cp in sends:
            cp.wait_send()
        if reduce:
            acc = land_ref[0]
            for p in range(1, N_DEV):
                acc = acc + land_ref[p]
            out_ref[...] = acc
        else:
            out_ref[...] = land_ref[...]

    out_shape = (rows, w) if reduce else (N_DEV, rows, w)
    return pl.pallas_call(
        body, name=name,
        out_shape=jax.ShapeDtypeStruct(out_shape, F32),
        in_specs=[pl.BlockSpec(memory_space=pltpu.VMEM)], out_specs=pl.BlockSpec(memory_space=pltpu.VMEM),
        scratch_shapes=[pltpu.VMEM((N_DEV, rows, w), F32), pltpu.SemaphoreType.DMA((N_DEV - 1,)),
                        pltpu.SemaphoreType.DMA((N_DEV - 1,))],
    )(v)


def _adamw(w, g, m, v, name):
    shape = w.shape
    cols = shape[-1] if len(shape) > 1 else shape[0]
    w2, g2, m2, v2 = (t.reshape(-1, cols) for t in (w, g, m, v))
    rows = w2.shape[0]
    tr = _tile(rows, 256, 8)

    def body(w_ref, g_ref, m_ref, v_ref, d_ref, mo_ref, vo_ref):
        gv = g_ref[...]
        mn = ADAM_B1 * m_ref[...] + (1.0 - ADAM_B1) * gv
        vn = ADAM_B2 * v_ref[...] + (1.0 - ADAM_B2) * (gv * gv)
        m_hat = mn / (1.0 - ADAM_B1 ** ADAM_STEP)
        v_hat = vn / (1.0 - ADAM_B2 ** ADAM_STEP)
        d_ref[...] = -ADAM_LR * (m_hat / (jnp.sqrt(v_hat) + ADAM_EPS) + ADAM_WD * w_ref[...])
        mo_ref[...] = mn
        vo_ref[...] = vn

    spec = pl.BlockSpec((tr, cols), lambda i: (i, 0))
    outs = pl.pallas_call(
        body, name=name, grid=(rows // tr,), in_specs=[spec] * 4, out_specs=[spec] * 3,
        out_shape=[jax.ShapeDtypeStruct((rows, cols), F32)] * 3,
        compiler_params=pltpu.CompilerParams(dimension_semantics=("parallel",), vmem_limit_bytes=V7X_VMEM_LIMIT_BYTES),
    )(w2, g2, m2, v2)
    return tuple(o.reshape(shape) for o in outs)


_DOT_DIMS = {2: dict(nn=NN, nt=NT, tn=TN),
             3: dict(nn=(((2,), (1,)), ((0,), (0,))), nt=(((2,), (2,)), ((0,), (0,))), tn=(((1,), (1,)), ((0,), (0,))))}


def _bf16_dot(a, b, kind):
    if a.ndim == 3:
        return jnp.stack([_bf16_dot(a[i], b[i] if b.ndim == 3 else b, kind) for i in range(a.shape[0])])
    return lax.dot_general(a.astype(BF16), b.astype(BF16), _DOT_DIMS[2][kind], preferred_element_type=F32)


def _f32_dot(a, b, kind):
    if a.ndim == 3:
        return jnp.stack([_f32_dot(a[i], b[i] if b.ndim == 3 else b, kind) for i in range(a.shape[0])])
    return lax.dot_general(a, b, _DOT_DIMS[2][kind], precision=HI, preferred_element_type=F32)


def _dot_family(raw):
    @jax.custom_vjp
    def nn(a, b):
        return raw(a, b, "nn")

    @jax.custom_vjp
    def nt(a, b):
        return raw(a, b, "nt")

    @jax.custom_vjp
    def tn(a, b):
        return raw(a, b, "tn")

    nn.defvjp(lambda a, b: (raw(a, b, "nn"), (a, b)), lambda r, ct: (nt(ct, r[1]), tn(r[0], ct)))
    nt.defvjp(lambda a, b: (raw(a, b, "nt"), (a, b)), lambda r, ct: (nn(ct, r[1]), tn(ct, r[0])))
    tn.defvjp(lambda a, b: (raw(a, b, "tn"), (a, b)), lambda r, ct: (nt(r[1], ct), nn(r[0], ct)))
    return nn, nt, tn


_bdot_nn, _bdot_nt, _bdot_tn = _dot_family(_bf16_dot)
_hdot_nn, _hdot_nt, _hdot_tn = _dot_family(_f32_dot)


def _f32x3_dot(a, b, kind):
    if a.ndim == 3:
        return jnp.stack([_f32x3_dot(a[i], b[i], kind) for i in range(a.shape[0])])
    return lax.dot_general(a, b, _DOT_DIMS[2][kind], precision=lax.Precision.HIGH, preferred_element_type=F32)


_mdot_nn, _mdot_nt, _mdot_tn = _dot_family(_f32x3_dot)


def _rows(width, t):
    return (t, width), lambda i: (i, 0)


def _whole(shape):
    return tuple(shape), lambda i: (0,) * len(shape)


def _tiled_op(name, fn, steps, in_blocks, kinds, outs, grad_blocks=None, grad_fix=None):
    n_in, n_out = len(in_blocks), len(outs)
    diff = [j for j, kd in enumerate(kinds) if kd != "const"]
    has_param = any(kd == "param" for kd in kinds)
    in_specs = [pl.BlockSpec(b, m) for b, m in in_blocks]
    out_specs = [pl.BlockSpec(b, m) for _, b, m in outs]

    def forward(*arrays):
        def body(*refs):
            res = fn(pl.program_id(0), *[r[...] for r in refs[:n_in]])
            for o_ref, val in zip(refs[n_in:], res):
                o_ref[...] = val

        return pl.pallas_call(
            body, name=name + "_fwd", grid=(steps,), in_specs=in_specs, out_specs=out_specs,
            out_shape=[jax.ShapeDtypeStruct(s, F32) for s, _, _ in outs],
            compiler_params=pltpu.CompilerParams(dimension_semantics=("parallel",),
                                                 vmem_limit_bytes=V7X_VMEM_LIMIT_BYTES),
        )(*arrays)

    def backward(arrays, douts):
        g_specs = []
        for j in diff:
            if kinds[j] == "param":
                g_specs.append((arrays[j].shape,) + _whole(arrays[j].shape))
            else:
                g_specs.append((grad_blocks or {}).get(j, (arrays[j].shape,) + in_blocks[j]))

        def body(*refs):
            i = pl.program_id(0)
            vals = [r[...] for r in refs[:n_in]]
            dos = tuple(r[...] for r in refs[n_in:n_in + n_out])
            g_refs = refs[n_in + n_out:]

            def of_diff(*dv):
                full = list(vals)
                for j, val in zip(diff, dv):
                    full[j] = val
                return tuple(fn(i, *full))

            _, vjp = jax.vjp(of_diff, *[vals[j] for j in diff])
            grads = vjp(dos)
            for j, g_ref, g in zip(diff, g_refs, grads):
                if kinds[j] == "param":
                    @pl.when(i == 0)
                    def _(g_ref=g_ref, g=g):
                        g_ref[...] = g

                    @pl.when(i != 0)
                    def _(g_ref=g_ref, g=g):
                        g_ref[...] += g
                else:
                    g_ref[...] = g

        return pl.pallas_call(
            body, name=name + "_bwd", grid=(steps,), in_specs=in_specs + out_specs,
            out_specs=[pl.BlockSpec(b, m) for _, b, m in g_specs],
            out_shape=[jax.ShapeDtypeStruct(shp, F32) for shp, _, _ in g_specs],
            compiler_params=pltpu.CompilerParams(dimension_semantics=("arbitrary" if has_param else "parallel",),
                                                 vmem_limit_bytes=V7X_VMEM_LIMIT_BYTES),
        )(*arrays, *douts)

    @jax.custom_vjp
    def op(*arrays):
        return tuple(forward(*arrays))

    def op_fwd(*arrays):
        return tuple(forward(*arrays)), arrays

    def op_bwd(arrays, douts):
        grads = backward(arrays, douts)
        full = [jnp.zeros_like(a) for a in arrays]
        for j, g in zip(diff, grads):
            full[j] = grad_fix[j](g) if grad_fix and j in grad_fix else g
        return tuple(full)

    op.defvjp(op_fwd, op_bwd)
    return op


def _sigmoid(x):
    return jax.nn.sigmoid(x)


def _silu(x):
    return x * jax.nn.sigmoid(x)


def _softplus(x):
    return jnp.maximum(x, 0.0) + jnp.log(1.0 + jnp.exp(-jnp.abs(x)))


def _heads(t, width):
    return [t[:, h * width:(h + 1) * width] for h in range(t.shape[1] // width)]


def _ffn_act(gu, name):
    s, two_f = gu.shape
    f = two_f // 2
    t = _tile(s, 256, 8)

    def fn(i, gu_t):
        return (_silu(gu_t[:, :f]) * gu_t[:, f:],)

    return _tiled_op(name, fn, s // t, [_rows(two_f, t)], ["row"], [((s, f),) + _rows(f, t)])(gu)[0]


def _res_ln(x, f, g, b, scale, name):
    s, d = x.shape
    t = _tile(s, 512, 8)

    def fn(i, x_t, f_t, g_t, b_t):
        r = DEEPNORM_ALPHA * x_t + scale * f_t
        mu = jnp.mean(r, axis=-1, keepdims=True)
        var = jnp.mean(jnp.square(r - mu), axis=-1, keepdims=True)
        return ((r - mu) * lax.rsqrt(var + LN_EPS) * g_t + b_t,)

    op = _tiled_op(name, fn, s // t, [_rows(d, t), _rows(d, t), _whole((1, d)), _whole((1, d))],
                   ["row", "row", "param", "param"], [((s, d),) + _rows(d, t)])
    return op(x, f, g.reshape(1, d), b.reshape(1, d))[0]


def _row_loss(y, target, name):
    s, d = y.shape
    t = _tile(s, 512, 8)

    def fn(i, y_t, t_t):
        e = y_t - t_t
        return (jnp.broadcast_to(0.5 * jnp.mean(e * e, axis=-1, keepdims=True), (t, LANES)),)

    return _tiled_op(name, fn, s // t, [_rows(d, t), _rows(d, t)], ["row", "const"],
                     [((s, LANES),) + _rows(LANES, t)])(y, target)[0]


def _gate_merge(gates, b0, b1, b2, name):
    s, d = b0.shape
    t = _tile(s, 256, 8)

    def fn(i, g_t, b0_t, b1_t, b2_t):
        return (_sigmoid(g_t[:, :d]) * b0_t + _sigmoid(g_t[:, d:2 * d]) * b1_t + _sigmoid(g_t[:, 2 * d:]) * b2_t,)

    op = _tiled_op(name, fn, s // t, [_rows(3 * d, t)] + [_rows(d, t)] * 3, ["row"] * 4, [((s, d),) + _rows(d, t)])
    return op(gates, b0, b1, b2)[0]


def _dn_pre(x, ba, conv_w, a_log, dt_bias, name):
    s, xw = x.shape
    w = DN_HEADS * DN_DK
    t = _tile(s, 256, 8)
    steps = s // t
    lane_pad = jnp.zeros((LANES - 2 * DN_HEADS,), F32)
    pa = jnp.concatenate([jnp.zeros((DN_HEADS,), F32), a_log, lane_pad]).reshape(1, LANES)
    pd = jnp.concatenate([jnp.zeros((DN_HEADS,), F32), dt_bias, lane_pad]).reshape(1, LANES)
    cw = jnp.concatenate([conv_w, jnp.zeros((8 - DN_CONV, xw), F32)], axis=0)
    taps = [jnp.pad(x, ((DN_CONV - 1 - j, 0), (0, 0)))[:s] for j in range(DN_CONV - 1)] + [x]

    def unit(x):
        return x * lax.rsqrt(jnp.sum(x * x, axis=-1, keepdims=True) + RMS_EPS)

    def fn(i, x0_t, x1_t, x2_t, x3_t, ba_t, cw_t, pa_t, pd_t):
        tap = lax.broadcasted_iota(jnp.int32, cw_t.shape, 0)
        c_t = sum(x_t * jnp.sum(jnp.where(tap == j, cw_t, 0.0), axis=0, keepdims=True)
                  for j, x_t in enumerate((x0_t, x1_t, x2_t, x3_t)))
        act = _silu(c_t)
        q = jnp.concatenate([unit(x) for x in _heads(act[:, :w], DN_DK)], axis=1)
        k = jnp.concatenate([unit(x) for x in _heads(act[:, w:2 * w], DN_DK)], axis=1)
        beta = _sigmoid(ba_t)
        g = -jnp.exp(pa_t) * _softplus(ba_t + pd_t)
        src = lax.broadcasted_iota(jnp.int32, (LANES, w), 0)
        head = lax.broadcasted_iota(jnp.int32, (LANES, w), 1) >> int(np.log2(DN_DK))
        be = _hdot_nn(beta, (src == head).astype(F32))
        ge = _hdot_nn(g, (src == head + DN_HEADS).astype(F32))
        return q, k, act[:, 2 * w:], be, ge

    op = _tiled_op(name, fn, steps,
                   [_rows(xw, t)] * DN_CONV + [_rows(LANES, t), _whole((8, xw)), _whole((1, LANES)), _whole((1, LANES))],
                   ["row"] * (DN_CONV + 1) + ["param"] * 3, [((s, w),) + _rows(w, t)] * 5)
    return op(*taps, ba, cw, pa, pd)


def _dn_post(o, z, norm_w, name):
    s, w = o.shape
    t = _tile(s, 512, 8)

    def fn(i, o_t, z_t, nw_t):
        outs = [x * lax.rsqrt(jnp.mean(x * x, axis=-1, keepdims=True) + RMS_EPS) * nw_t * _silu(zz)
                for x, zz in zip(_heads(o_t, DN_DV), _heads(z_t, DN_DV))]
        return (jnp.concatenate(outs, axis=1),)

    op = _tiled_op(name, fn, s // t, [_rows(w, t), _rows(w, t), _whole((1, DN_DV))], ["row", "row", "param"],
                   [((s, w),) + _rows(w, t)])
    return op(o, z, norm_w.reshape(1, DN_DV))[0]


def _xattn(q, k_mem, v_mem, name):
    s, w = q.shape
    t = _tile(s, 512, 8)

    def fn(i, q_t, k_t, v_t):
        outs = []
        for qh, kh, vh in zip(_heads(q_t, XA_DH), _heads(k_t, XA_DH), _heads(v_t, XA_DH)):
            sc = _bdot_nt(qh, kh) * (XA_DH ** -0.5)
            e = jnp.exp(sc - jnp.max(sc, axis=-1, keepdims=True))
            outs.append(_bdot_nn(e / jnp.sum(e, axis=-1, keepdims=True), vh))
        return (jnp.concatenate(outs, axis=1),)

    op = _tiled_op(name, fn, s // t, [_rows(w, t), _whole(k_mem.shape), _whole(v_mem.shape)],
                   ["row", "param", "param"], [((s, w),) + _rows(w, t)])
    return op(q, k_mem, v_mem)[0]


def _swa(q, kv, sinks, name):
    s, w = q.shape
    wd = WINDOW
    grp = SWA_HEADS // SWA_KV_HEADS
    dh = SWA_DH
    steps = s // wd
    sink_rows = jnp.broadcast_to(sinks[:, None], (SWA_HEADS, LANES))

    def fn(i, q_t, kc_t, kp_t, sk_t):
        qr = lax.broadcasted_iota(jnp.int32, (grp * wd, 2 * wd), 0) & (wd - 1)
        kc = lax.broadcasted_iota(jnp.int32, (grp * wd, 2 * wd), 1)
        dist = qr + wd - kc
        mask = (dist >= 0) & (dist < wd) & ((kc >= wd) | (i > 0))
        head_row = lax.broadcasted_iota(jnp.int32, (SWA_HEADS, LANES), 0)
        outs = [None] * SWA_HEADS
        for hk in range(SWA_KV_HEADS):
            kh = jnp.concatenate([kp_t[:, hk * dh:(hk + 1) * dh], kc_t[:, hk * dh:(hk + 1) * dh]], axis=0)
            vh = jnp.concatenate([kp_t[:, (SWA_KV_HEADS + hk) * dh:(SWA_KV_HEADS + hk + 1) * dh],
                                  kc_t[:, (SWA_KV_HEADS + hk) * dh:(SWA_KV_HEADS + hk + 1) * dh]], axis=0)
            qg = jnp.concatenate([q_t[:, (hk * grp + g) * dh:(hk * grp + g + 1) * dh] for g in range(grp)], axis=0)
            sink = jnp.concatenate(
                [jnp.broadcast_to(jnp.sum(jnp.where(head_row == hk * grp + g, sk_t, 0.0), axis=0, keepdims=True),
                                  (wd, LANES)) for g in range(grp)], axis=0)
            sink = jnp.max(sink, axis=-1, keepdims=True)
            sc = jnp.where(mask, _bdot_nt(qg, kh) * (dh ** -0.5), NEG_INF)
            m = jnp.maximum(jnp.max(sc, axis=-1, keepdims=True), sink)
            p = jnp.exp(sc - m)
            p = p / (jnp.sum(p, axis=-1, keepdims=True) + jnp.exp(sink - m))
            o = _bdot_nn(p, vh)
            for g in range(grp):
                outs[hk * grp + g] = o[g * wd:(g + 1) * wd]
        return (jnp.concatenate(outs, axis=1),)

    kvw = kv.shape[1]
    prev_block = ((wd, kvw), lambda i: (jnp.maximum(i - 1, 0), 0))
    shift = lambda g: jnp.concatenate([g[wd:], jnp.zeros((wd, kvw), F32)], axis=0)
    op = _tiled_op(name, fn, steps, [_rows(w, wd), _rows(kvw, wd), prev_block, _whole((SWA_HEADS, LANES))],
                   ["row", "row", "row", "param"], [((s, w),) + _rows(w, wd)],
                   grad_blocks={2: (kv.shape,) + _rows(kvw, wd)}, grad_fix={2: shift})
    return op(q, kv, kv, sink_rows)[0]


def _dn_prepare(q, k, v, be, ge):
    b, c, dk = q.shape
    row = lax.broadcasted_iota(jnp.int32, (c, c), 0)
    col = lax.broadcasted_iota(jnp.int32, (c, c), 1)
    tril = row >= col
    gc = _hdot_nn(jnp.broadcast_to(tril.astype(F32), (b, c, c)), ge)
    g_col = _hdot_nn(gc, jnp.full((b, LANES, c), 1.0 / LANES, F32))
    g_row = _hdot_nt(jnp.full((b, c, LANES), 1.0 / LANES, F32), gc)
    decay = jnp.where(tril, jnp.exp(jnp.where(tril, g_col - g_row, 0.0)), 0.0)
    kb = k * be
    a = jnp.where(row > col, _bdot_nt(kb, k) * decay, 0.0)
    x = jnp.broadcast_to((row == col).astype(F32), (b, c, c))
    for lvl in range(int(np.log2(c))):
        off = ((row >> lvl) == (col >> lvl) + 1) & ((row >> (lvl + 1)) == (col >> (lvl + 1)))
        x = x - _mdot_nn(x, _mdot_nn(jnp.where(off, a, 0.0), x))
    eg = jnp.exp(gc)
    qs = q * (dk ** -0.5)
    qk = jnp.where(tril, _bdot_nt(qs, k) * decay, 0.0)
    gl_c = _hdot_nn(jnp.ones((b, c, c), F32), ge)
    gl_s = _hdot_nn(jnp.ones((b, dk, c), F32), ge)
    return _mdot_nn(x, v * be), _mdot_nn(x, kb * eg), qs * eg, k * jnp.exp(gl_c - gc), qk, jnp.exp(gl_s)


def _dn_stack(ref, chunks, heads):
    return jnp.stack([ref[ci * DN_CHUNK:(ci + 1) * DN_CHUNK, h * DN_DK:(h + 1) * DN_DK]
                      for ci in range(chunks) for h in range(heads)])


def _dn_step(u, w, q_dec, k_tail, qk, decay, s):
    v_new = u - _bdot_nn(w, s)
    o = _bdot_nn(q_dec, s) + _bdot_nn(qk, v_new)
    return o, s * decay + _bdot_tn(k_tail, v_new)


def _hdot(a, b, dims=NN):
    return lax.dot_general(a, b, dims, precision=HI, preferred_element_type=F32)


def _mdot(a, b, dims=NN):
    return lax.dot_general(a, b, dims, precision=lax.Precision.HIGH, preferred_element_type=F32)


def _dn_chunk_heads(q, k, v, be, ge, s):
    c, dk = q[0].shape
    row = lax.broadcasted_iota(jnp.int32, (c, c), 0)
    col = lax.broadcasted_iota(jnp.int32, (c, c), 1)
    tril = row >= col
    trilf = tril.astype(F32)
    mean_rows = jnp.full((c, LANES), 1.0 / LANES, F32)
    gc = [_hdot(trilf, g) for g in ge]
    g_row = [_hdot(mean_rows, g, NT) for g in gc]
    decay = [jnp.where(tril, jnp.exp(jnp.where(tril, g[:, :c] - gr, 0.0)), 0.0) for g, gr in zip(gc, g_row)]
    kb = [ki * bi for ki, bi in zip(k, be)]
    a = [jnp.where(row > col, _bdot_nt(x_, y_) * d, 0.0) for x_, y_, d in zip(kb, k, decay)]
    x = [(row == col).astype(F32)] * len(q)
    for lvl in range(int(np.log2(c))):
        off = ((row >> lvl) == (col >> lvl) + 1) & ((row >> (lvl + 1)) == (col >> (lvl + 1)))
        t = [_mdot(jnp.where(off, ai, 0.0), xi) for ai, xi in zip(a, x)]
        x = [xi - _mdot(xi, ti) for xi, ti in zip(x, t)]
    eg = [jnp.exp(g) for g in gc]
    u = [_mdot(xi, vi * bi) for xi, vi, bi in zip(x, v, be)]
    w = [_mdot(xi, ki * ei) for xi, ki, ei in zip(x, kb, eg)]
    qs = [qi * (dk ** -0.5) for qi in q]
    qk = [jnp.where(tril, _bdot_nt(x_, y_) * d, 0.0) for x_, y_, d in zip(qs, k, decay)]
    gl = [jnp.sum(g, axis=0, keepdims=True) for g in ge]
    v_new = [ui - _bdot_nn(wi, si) for ui, wi, si in zip(u, w, s)]
    o = [_bdot_nn(qi * ei, si) + _bdot_nn(ai, vi) for qi, ei, si, ai, vi in zip(qs, eg, s, qk, v_new)]
    s_next = [si * jnp.exp(g) + _bdot_tn(ki * jnp.exp(g - gi), vi)
              for si, g, ki, gi, vi in zip(s, gl, k, gc, v_new)]
    return o, s_next


def _dn_forward(q, k, v, be, ge, name, chunks_per_step=4):
    s_len, width = q.shape
    heads = width // DN_DK
    n = s_len // DN_CHUNK
    cb = min(chunks_per_step, n)
    rows = cb * DN_CHUNK

    def body(q_ref, k_ref, v_ref, be_ref, ge_ref, o_ref, st_ref, s_scr):
        @pl.when(pl.program_id(0) == 0)
        def _():
            s_scr[...] = jnp.zeros_like(s_scr)

        parts = _dn_prepare(*[_dn_stack(ref, cb, heads) for ref in (q_ref, k_ref, v_ref, be_ref, ge_ref)])
        for ci in range(cb):
            s = s_scr[...]
            o, s_next = _dn_step(*[t[ci * heads:(ci + 1) * heads] for t in parts], s)
            for h in range(heads):
                st_ref[ci, h] = s[h]
                o_ref[ci * DN_CHUNK:(ci + 1) * DN_CHUNK, h * DN_DK:(h + 1) * DN_DK] = o[h]
            s_scr[...] = s_next

    tok = pl.BlockSpec((rows, width), lambda i: (i, 0))
    return pl.pallas_call(
        body, name=name + "_fwd", grid=(n // cb,), in_specs=[tok] * 5,
        out_specs=[tok, pl.BlockSpec((cb, heads, DN_DK, DN_DV), lambda i: (i, 0, 0, 0))],
        out_shape=[jax.ShapeDtypeStruct((s_len, width), F32), jax.ShapeDtypeStruct((n, heads, DN_DK, DN_DV), F32)],
        scratch_shapes=[pltpu.VMEM((heads, DN_DK, DN_DV), F32)],
        compiler_params=pltpu.CompilerParams(dimension_semantics=("arbitrary",), vmem_limit_bytes=V7X_VMEM_LIMIT_BYTES),
    )(q, k, v, be, ge)


def _dn_backward(q, k, v, be, ge, states, do, name, chunks_per_step=4):
    s_len, width = q.shape
    heads = width // DN_DK
    n = s_len // DN_CHUNK
    cb = min(chunks_per_step, n)
    rows = cb * DN_CHUNK
    steps = n // cb

    def body(q_ref, k_ref, v_ref, be_ref, ge_ref, st_ref, do_ref, dq_ref, dk_ref, dv_ref, dbe_ref, dge_ref, ds_scr):
        @pl.when(pl.program_id(0) == 0)
        def _():
            ds_scr[...] = jnp.zeros_like(ds_scr)

        lanes = [slice(h * DN_DK, (h + 1) * DN_DK) for h in range(heads)]
        for ci in reversed(range(cb)):
            r = slice(ci * DN_CHUNK, (ci + 1) * DN_CHUNK)
            _, vjp = jax.vjp(_dn_chunk_heads, *[[ref[r, l] for l in lanes]
                                                for ref in (q_ref, k_ref, v_ref, be_ref, ge_ref)],
                             [st_ref[ci, h] for h in range(heads)])
            *grads, ds = vjp(([do_ref[r, l] for l in lanes], [ds_scr[h] for h in range(heads)]))
            for g_ref, g in zip((dq_ref, dk_ref, dv_ref, dbe_ref, dge_ref), grads):
                for l, g_head in zip(lanes, g):
                    g_ref[r, l] = g_head
            for h in range(heads):
                ds_scr[h] = ds[h]

    tok = pl.BlockSpec((rows, width), lambda i: (steps - 1 - i, 0))
    return pl.pallas_call(
        body, name=name + "_bwd", grid=(steps,),
        in_specs=[tok] * 5 + [pl.BlockSpec((cb, heads, DN_DK, DN_DV), lambda i: (steps - 1 - i, 0, 0, 0)), tok],
        out_specs=[tok] * 5,
        out_shape=[jax.ShapeDtypeStruct((s_len, width), F32)] * 5,
        scratch_shapes=[pltpu.VMEM((heads, DN_DK, DN_DV), F32)],
        compiler_params=pltpu.CompilerParams(dimension_semantics=("arbitrary",), vmem_limit_bytes=V7X_VMEM_LIMIT_BYTES),
    )(q, k, v, be, ge, states, do)


@functools.partial(jax.custom_vjp, nondiff_argnums=(5,))
def _dn_core(q, k, v, be, ge, name):
    return _dn_forward(q, k, v, be, ge, name)[0]


def _dn_core_fwd(q, k, v, be, ge, name):
    o, states = _dn_forward(q, k, v, be, ge, name)
    return o, (q, k, v, be, ge, states)


def _dn_core_bwd(name, res, do):
    return tuple(_dn_backward(*res, do, name))


_dn_core.defvjp(_dn_core_fwd, _dn_core_bwd)


def _layer_norm(x, g, b):
    mu = x.mean(-1, keepdims=True)
    var = jnp.square(x - mu).mean(-1, keepdims=True)
    return (x - mu) * lax.rsqrt(var + LN_EPS) * g + b


def _hybrid_layer(x, mem_n, p, p_bf16, l):
    tag = f"l{l}"
    ln_g, ln_b = p["ln_g"][l], p["ln_b"][l]

    def mm(a, name, tag_, *idx):
        return pmm(a, p[name][(l,) + idx], p_bf16[name][(l,) + idx], tag_)

    def swiglu(a, ffn):
        return mm(_ffn_act(mm(a, ffn + "_w_gu", f"{tag}_{ffn}_gu"), f"{tag}_{ffn}_act"), ffn + "_w_down",
                  f"{tag}_{ffn}_down")

    h = _res_ln(x, swiglu(x, "ffn1"), ln_g[0], ln_b[0], 0.5, tag + "_ln0")
    proj = mm(h, "w_in", tag + "_in")
    seg = {name: proj[:, lo:hi] for name, (lo, hi) in SEG.items()}
    q, k, v, be, ge = _dn_pre(seg["dn_qkv"], seg["dn_ba"], p["dn_conv_w"][l], p["dn_a_log"][l], p["dn_dt_bias"][l],
                              tag + "_dn_pre")
    o_dn = _dn_post(_dn_core(q, k, v, be, ge, tag + "_dn"), seg["dn_z"], p["dn_norm_w"][l], tag + "_dn_post")
    o_sw = _swa(seg["sw_q"], seg["sw_kv"], p["swa_sinks"][l], tag + "_swa")
    kv_mem = mm(mem_n, "w_mem_kv", tag + "_memkv")
    half = XA_HEADS * XA_DH
    o_xa = _xattn(seg["xa_q"], kv_mem[:, :half], kv_mem[:, half:], tag + "_xattn")
    branch = [mm(o, "w_branch", tag + f"_branch{i}", i) for i, o in enumerate((o_dn, o_sw, o_xa))]
    merged = _gate_merge(seg["gates"], *branch, tag + "_merge")
    h = _res_ln(h, mm(merged, "w_out", tag + "_out"), ln_g[1], ln_b[1], 1.0, tag + "_ln1")
    return _res_ln(h, swiglu(h, "ffn2"), ln_g[2], ln_b[2], 0.5, tag + "_ln2")


def _local_loss(p, x, mem, target, p_bf16):
    mem_n = _layer_norm(mem, p["mem_ln_g"], p["mem_ln_b"])
    y = x
    for l in range(DEPTH):
        y = _hybrid_layer(y, mem_n, p, p_bf16, l)
    return jnp.sum(_row_loss(y, target, "row_loss")[:, 0])


def _w_in_permute(w):
    pad = jnp.zeros(w.shape[:-1] + (D_IN_PAD - D_IN,), w.dtype)
    return jnp.concatenate([w[..., :1536], w[..., 1544:], w[..., 1536:1544], pad], axis=-1)


def _w_in_unpermute(w):
    return jnp.concatenate([w[..., :1536], w[..., 6400:6408], w[..., 1536:6400]], axis=-1)


def _join_shards(gathered, ax):
    shp = gathered.shape[1:]
    t = jnp.moveaxis(gathered, 0, ax)
    return t.reshape(shp[:ax] + (N_DEV * shp[ax],) + shp[ax + 1:])


def _split_shards(whole, ax):
    shp = whole.shape
    t = whole.reshape(shp[:ax] + (N_DEV, shp[ax] // N_DEV) + shp[ax + 1:])
    return jnp.moveaxis(t, ax, 0)


def _pack_small(parts):
    flat = jnp.concatenate([t.reshape(-1).astype(F32) for t in parts])
    rows = -(-flat.shape[0] // LANES)
    rows = -(-rows // 8) * 8
    return jnp.concatenate([flat, jnp.zeros((rows * LANES - flat.shape[0],), F32)]).reshape(rows, LANES)


def _unpack_small(packed, shapes):
    flat = packed.reshape(packed.shape[:-2] + (-1,))
    out, off = [], 0
    for shp in shapes:
        size = int(np.prod(shp))
        out.append(flat[..., off:off + size].reshape(flat.shape[:-1] + tuple(shp)))
        off += size
    return out


def kernel(x, mem, mem_ln_g, mem_ln_b, ln_g, ln_b, ffn1_w_gu, ffn1_w_down, w_in, dn_conv_w, dn_a_log, dn_dt_bias, dn_norm_w, swa_sinks, w_mem_kv, w_branch, w_out, ffn2_w_gu, ffn2_w_down, loss_target, m_mem_ln_g, m_mem_ln_b, m_ln_g, m_ln_b, m_ffn1_w_gu, m_ffn1_w_down, m_w_in, m_dn_conv_w, m_dn_a_log, m_dn_dt_bias, m_dn_norm_w, m_swa_sinks, m_w_mem_kv, m_w_branch, m_w_out, m_ffn2_w_gu, m_ffn2_w_down, v_mem_ln_g, v_mem_ln_b, v_ln_g, v_ln_b, v_ffn1_w_gu, v_ffn1_w_down, v_w_in, v_dn_conv_w, v_dn_a_log, v_dn_dt_bias, v_dn_norm_w, v_swa_sinks, v_w_mem_kv, v_w_branch, v_w_out, v_ffn2_w_gu, v_ffn2_w_down):
    given = dict(locals())
    w_loc = {n: given[n] for n in WEIGHTS}
    m_loc = {n: given["m_" + n] for n in WEIGHTS}
    v_loc = {n: given["v_" + n] for n in WEIGHTS}
    me = _my_index()

    params_bf16 = {n: _join_shards(_all_gather(w_loc[n].astype(BF16), "all_gather_" + n), BIG_AXIS[n]) for n in BIG}
    params_bf16["w_in"] = _w_in_permute(params_bf16["w_in"])
    params = {n: params_bf16[n].astype(F32) for n in BIG}
    small_shard_shapes = [w_loc[n].shape for n in SMALL_SHARDED]
    small_gathered = _small_all_gather(_pack_small([w_loc[n] for n in SMALL_SHARDED]), False, "all_gather_small")
    for n, t in zip(SMALL_SHARDED, _unpack_small(small_gathered, small_shard_shapes)):
        t = jnp.moveaxis(t, 0, -2)
        params[n] = t.reshape(t.shape[:-2] + (-1,))
    for n in SMALL_REPL:
        params[n] = w_loc[n]

    loss_local, (g_params, grad_x) = jax.value_and_grad(_local_loss, argnums=(0, 1))(
        params, x[0], mem[0], loss_target[0], params_bf16)
    grad_x = grad_x[None]
    g_params["w_in"] = _w_in_unpermute(g_params["w_in"])

    g_big = {n: _sum_slots(_exchange_slots(_split_shards(g_params[n], BIG_AXIS[n]).astype(BF16),
                                           "exchange_grad_" + n), "sum_grad_" + n) for n in BIG}
    small_names = SMALL_SHARDED + SMALL_REPL
    small_full_shapes = [g_params[n].shape for n in small_names] + [()]
    small_sum = _small_all_gather(_pack_small([g_params[n] for n in small_names] + [loss_local]), True, "all_sum_small")
    small_full = dict(zip(small_names + ("loss",), _unpack_small(small_sum, small_full_shapes)))
    loss = small_full["loss"]
    grads = dict(g_big)
    for n in SMALL_SHARDED:
        shard = w_loc[n].shape[-1]
        grads[n] = lax.dynamic_slice_in_dim(small_full[n], me * shard, shard, axis=small_full[n].ndim - 1)
    for n in SMALL_REPL:
        grads[n] = small_full[n]

    delta, new_m, new_v = {}, {}, {}
    for n in BIG:
        delta[n], new_m[n], new_v[n] = _adamw(w_loc[n], grads[n], m_loc[n], v_loc[n], "adamw_" + n)
    shapes = [w_loc[n].shape for n in small_names]
    packed = [_pack_small([src[n] for n in small_names]) for src in (w_loc, grads, m_loc, v_loc)]
    for dst, t in zip((delta, new_m, new_v), _adamw(*packed, "adamw_small")):
        dst.update(zip(small_names, _unpack_small(t, shapes)))

    return (loss, grad_x, *[grads[n] for n in WEIGHTS], *[delta[n] for n in WEIGHTS],
            *[new_m[n] for n in WEIGHTS], *[new_v[n] for n in WEIGHTS])
```
